```python
import jax, jax.numpy as jnp
from jax import lax
import numpy as np

D_MODEL = 1024
BATCH = 2
SEQ = 8192
DEPTH = 2

HEAD_DIM = 64
D_MIX = D_MODEL
N_HEADS_FOX = D_MIX // (2 * HEAD_DIM)
N_HEADS_DIL = D_MIX // (2 * HEAD_DIM)
D_FOX = N_HEADS_FOX * HEAD_DIM
D_DIL = N_HEADS_DIL * HEAD_DIM
N_IN = 4 * D_FOX + N_HEADS_FOX + 4 * D_DIL
PLE_DIM = 256
ROPE_THETA = 500000.0
ROPE_DIM = HEAD_DIM // 4
DILATED_PATTERNS = ((128, 1), (512, 4), (2048, 16))
BLOCK = 128
EPS = 1e-6
NEG = -1e30
FORGET_BIAS_INIT = 3.0

kernel_name = "fox_dilated_hybrid_heads"


def rms_norm(x, g):
    xf = x.astype(jnp.float32)
    y = xf * lax.rsqrt(jnp.mean(xf * xf, axis=-1, keepdims=True) + EPS)
    return (y * g.astype(jnp.float32)).astype(x.dtype)


def partial_rope(x, positions):
    half = ROPE_DIM // 2
    inv_freq = ROPE_THETA ** (-jnp.arange(half, dtype=jnp.float32) / half)
    ang = positions.astype(jnp.float32)[..., None] * inv_freq
    cos = jnp.cos(ang)[:, :, None, :]
    sin = jnp.sin(ang)[:, :, None, :]
    xr = x[..., :ROPE_DIM].astype(jnp.float32)
    x1, x2 = xr[..., :half], xr[..., half:]
    rot = jnp.concatenate([x1 * cos - x2 * sin, x2 * cos + x1 * sin], axis=-1)
    return jnp.concatenate([rot.astype(x.dtype), x[..., ROPE_DIM:]], axis=-1)


def forgetting_attention(q, k, v, log_f):
    B, S, H, Dh = q.shape
    nb = S // BLOCK
    scale = Dh ** -0.5
    c = jnp.cumsum(log_f.astype(jnp.float32), axis=1).transpose(0, 2, 1)
    qb = q.reshape(B, nb, BLOCK, H, Dh).transpose(1, 0, 2, 3, 4)
    cb = c.reshape(B, H, nb, BLOCK).transpose(2, 0, 1, 3)
    key_pos = jnp.arange(S)

    def one_block(args):
        i, q_i, c_i = args
        s = jnp.einsum('bqhd,bkhd->bhqk', q_i, k, preferred_element_type=jnp.float32) * scale
        s = s + c_i[..., :, None] - c[:, :, None, :]
        q_pos = i * BLOCK + jnp.arange(BLOCK)
        causal = key_pos[None, :] <= q_pos[:, None]
        s = jnp.where(causal, s, NEG)
        prob = jax.nn.softmax(s, axis=-1)
        return jnp.einsum('bhqk,bkhd->bqhd', prob.astype(v.dtype), v)

    out = lax.map(one_block, (jnp.arange(nb), qb, cb))
    return out.transpose(1, 0, 2, 3, 4).reshape(B, S, H, Dh)


def dilated_branch(q, k, v, window, dilation):
    B, S, H, Dh = q.shape
    L = S // dilation
    n_back = window // dilation
    n_prev = -(-n_back // BLOCK)
    nb = -(-L // BLOCK)
    Lp = nb * BLOCK
    N = B * dilation
    KW = (n_prev + 1) * BLOCK

    def to_streams(x):
        x = x.reshape(B, L, dilation, H, Dh).transpose(0, 2, 1, 3, 4).reshape(N, L, H, Dh)
        return jnp.pad(x, ((0, 0), (0, Lp - L), (0, 0), (0, 0)))

    def key_band(x):
        xs = jnp.pad(to_streams(x), ((0, 0), (n_prev * BLOCK, 0), (0, 0), (0, 0)))
        xs = xs.reshape(N, nb + n_prev, BLOCK, H, Dh)
        return jnp.concatenate([xs[:, j:j + nb] for j in range(n_prev + 1)], axis=2)

    qs = to_streams(q).reshape(N, nb, BLOCK, H, Dh)
    kb, vb = key_band(k), key_band(v)
    s = jnp.einsum('nbqhd,nbkhd->nbhqk', qs, kb, preferred_element_type=jnp.float32) * (Dh ** -0.5)
    qq = jnp.arange(BLOCK)[:, None]
    kk = jnp.arange(KW)[None, :]
    dist = qq + n_prev * BLOCK - kk
    key_idx = jnp.arange(nb)[:, None, None] * BLOCK - n_prev * BLOCK + kk[None]
    valid = (dist >= 0) & (dist <= n_back) & (key_idx >= 0)
    s = jnp.where(valid[None, :, None], s, NEG)
    lse = jax.nn.logsumexp(s, axis=-1)
    prob = jnp.exp(s - lse[..., None])
    o = jnp.einsum('nbhqk,nbkhd->nbqhd', prob.astype(v.dtype), vb)

    def from_streams(x):
        tail = x.shape[3:]
        x = x.reshape((N, Lp) + tail)[:, :L]
        x = x.reshape((B, dilation, L) + tail).swapaxes(1, 2)
        return x.reshape((B, S) + tail)

    return from_streams(o), from_streams(lse.transpose(0, 1, 3, 2))


def dilated_attention(q, k, v):
    outs, lses = [], []
    for window, dilation in DILATED_PATTERNS:
        o, l = dilated_branch(q, k, v, window, dilation)
        outs.append(o)
        lses.append(l)
    w = jax.nn.softmax(jnp.stack(lses, axis=0), axis=0)
    o = jnp.stack(outs, axis=0).astype(jnp.float32)
    return jnp.sum(w[..., None] * o, axis=0).astype(q.dtype)


def hybrid_layer(h, p_i, positions, norm_g, w_in, b_f, qk_g, w_out, w_ple, ple_norm_g, w_ple_gate):
    B, S, _ = h.shape
    u = rms_norm(h, norm_g)
    z = u @ w_in
    cuts = np.cumsum([D_FOX, D_FOX, D_FOX, D_FOX, N_HEADS_FOX, D_DIL, D_DIL, D_DIL])
    qa, ka, va, ga, fa, qb, kb, vb, gb = jnp.split(z, cuts.tolist(), axis=-1)
    heads = lambda t, n: t.reshape(B, S, n, HEAD_DIM)

    qa = rms_norm(heads(qa, N_HEADS_FOX), qk_g[0])
    ka = rms_norm(heads(ka, N_HEADS_FOX), qk_g[1])
    log_f = jax.nn.log_sigmoid(fa.astype(jnp.float32) + b_f.astype(jnp.float32))
    oa = forgetting_attention(qa, ka, heads(va, N_HEADS_FOX), log_f).reshape(B, S, D_FOX)
    oa = oa * jax.nn.silu(ga)

    qb = partial_rope(rms_norm(heads(qb, N_HEADS_DIL), qk_g[2]), positions)
    kb = partial_rope(rms_norm(heads(kb, N_HEADS_DIL), qk_g[3]), positions)
    ob = dilated_attention(qb, kb, heads(vb, N_HEADS_DIL)).reshape(B, S, D_DIL)
    ob = ob * jax.nn.silu(gb)

    h = h + jnp.concatenate([oa, ob], axis=-1) @ w_out

    gate = jax.nn.sigmoid(rms_norm(h, ple_norm_g) @ w_ple_gate)
    return h + (p_i @ w_ple) * gate


def setup_inputs(seed: int = 0) -> dict:
    key = jax.random.key(seed)
    ks = jax.random.split(key, 12)
    f32 = jnp.float32
    x = jax.random.normal(ks[0], (BATCH, SEQ, D_MODEL), f32)
    p = jax.random.normal(ks[1], (DEPTH, BATCH, SEQ, PLE_DIM), f32)
    positions = jnp.broadcast_to(jnp.arange(SEQ, dtype=jnp.int32), (BATCH, SEQ))
    norm_g = 1.0 + 0.05 * jax.random.normal(ks[2], (DEPTH, D_MODEL), f32)
    w_in = jax.random.normal(ks[3], (DEPTH, D_MODEL, N_IN), f32) * D_MODEL ** -0.5
    b_f = FORGET_BIAS_INIT + 0.1 * jax.random.normal(ks[4], (DEPTH, N_HEADS_FOX), f32)
    qk_norm_g = 1.0 + 0.05 * jax.random.normal(ks[5], (DEPTH, 4, HEAD_DIM), f32)
    w_out = jax.random.normal(ks[6], (DEPTH, D_MIX, D_MODEL), f32) * D_MIX ** -0.5
    w_ple = jax.random.normal(ks[7], (DEPTH, PLE_DIM, D_MODEL), f32) * PLE_DIM ** -0.5
    ple_norm_g = 1.0 + 0.05 * jax.random.normal(ks[8], (DEPTH, D_MODEL), f32)
    w_ple_gate = jax.random.normal(ks[9], (DEPTH, D_MODEL, D_MODEL), f32) * D_MODEL ** -0.5
    return {"x": x, "p": p, "positions": positions, "norm_g": norm_g, "w_in": w_in,
            "b_f": b_f, "qk_norm_g": qk_norm_g, "w_out": w_out, "w_ple": w_ple,
            "ple_norm_g": ple_norm_g, "w_ple_gate": w_ple_gate}


def reference(x, p, positions, norm_g, w_in, b_f, qk_norm_g, w_out, w_ple, ple_norm_g, w_ple_gate):
    h = x
    for i in range(DEPTH):
        h = hybrid_layer(h, p[i], positions, norm_g[i], w_in[i], b_f[i], qk_norm_g[i],
                         w_out[i], w_ple[i], ple_norm_g[i], w_ple_gate[i])
    return h
```

```python
import functools

import numpy as np
import jax
import jax.numpy as jnp
from jax import lax
from jax.experimental import pallas as pl
from jax.experimental.pallas import tpu as pltpu

D_MODEL = 1024
HEAD_DIM = 64
N_HEADS = 8
D_BRANCH = N_HEADS * HEAD_DIM
PLE_DIM = 256
ROPE_THETA = 500000.0
ROPE_DIM = HEAD_DIM // 4
ROPE_HALF = ROPE_DIM // 2
DILATED_PATTERNS = ((128, 1), (512, 4), (2048, 16))
N_BACK = 128
EPS = 1e-6
NEG = -1e30

LANES = 128
MXU_DIM = 256
VMEM_LIMIT_BYTES = 56 * 1024 * 1024

ROW_TILE = 512
FOX_TILE = 512
DIL_TILE = 256
CUMSUM_CHUNK = 256

F32 = jnp.float32
BF16 = jnp.bfloat16


def _dot(a, b):
    return jnp.dot(a, b, preferred_element_type=F32)


def _dot_nt(a, b):
    return lax.dot_general(a, b, (((1,), (1,)), ((), ())), preferred_element_type=F32)


def _split3(x):
    hi = x.astype(BF16)
    r = x - hi.astype(F32)
    mid = r.astype(BF16)
    lo = (r - mid.astype(F32)).astype(BF16)
    return hi, mid, lo


def _rope_table_kernel(pos_ref, invf_ref, sgn_ref, cos_ref, sin_ref):
    ang = pos_ref[...].astype(F32) * invf_ref[...]
    cos_ref[...] = jnp.cos(ang)
    sin_ref[...] = jnp.sin(ang) * sgn_ref[...]


def _rope_tables(positions):
    rows = positions.size
    pos_b = jnp.broadcast_to(positions.reshape(rows, 1), (rows, LANES))
    inv_freq = ROPE_THETA ** (-jnp.arange(ROPE_HALF, dtype=F32) / ROPE_HALF)
    j = np.arange(LANES) % HEAD_DIM
    invf = jnp.where(j < ROPE_DIM, inv_freq[j % ROPE_HALF], 0.0).reshape(1, LANES)
    sgn = jnp.asarray(np.where(j < ROPE_HALF, -1.0, np.where(j < ROPE_DIM, 1.0, 0.0)),
                      F32).reshape(1, LANES)
    row_spec = pl.BlockSpec((ROW_TILE, LANES), lambda i: (i, 0))
    const_spec = pl.BlockSpec((1, LANES), lambda i: (0, 0))
    return pl.pallas_call(
        _rope_table_kernel,
        out_shape=(jax.ShapeDtypeStruct((rows, LANES), F32),) * 2,
        grid=(rows // ROW_TILE,),
        in_specs=[row_spec, const_spec, const_spec],
        out_specs=(row_spec, row_spec),
        name="rope_tables",
    )(pos_b, invf, sgn)


def _proj_kernel(h_ref, g1_ref, wn_ref, wp_ref, wf_ref, bf_ref, bd_ref, gain_ref,
                 cos_ref, sin_ref,
                 qa_ref, ka_ref, qb_ref, kb_ref, va_ref, ga_ref, vb_ref, gb_ref, lf_ref):
    h = h_ref[...]
    ms = jnp.mean(h * h, axis=-1, keepdims=True)
    u = (h * lax.rsqrt(ms + EPS) * g1_ref[...]).astype(BF16)

    f = _dot(u, wf_ref[...]) + bf_ref[...]
    lf_ref[...] = jnp.minimum(f, 0.0) - jnp.log1p(jnp.exp(-jnp.abs(f)))

    bd = bd_ref[...]
    lane = lax.broadcasted_iota(jnp.int32, (1, LANES), 1)
    first_half = (lane % HEAD_DIM) < ROPE_HALF
    cos = cos_ref[...]
    sin = sin_ref[...]
    norm_refs = (qa_ref, ka_ref, qb_ref, kb_ref)
    n_slab = D_BRANCH // MXU_DIM
    for s in range(4 * n_slab):
        cols = slice(s * MXU_DIM, (s + 1) * MXU_DIM)
        z = _dot(u, wn_ref[:, cols])
        zz = z * z
        hi = zz.astype(BF16)
        lo = (zz - hi.astype(F32)).astype(BF16)
        ss = _dot(hi, bd) + _dot(lo, bd)
        y = z * lax.rsqrt(ss * (1.0 / HEAD_DIM) + EPS) * gain_ref[:, cols]
        t, half = divmod(s, n_slab)
        out_cols = slice(half * MXU_DIM, (half + 1) * MXU_DIM)
        if t >= 2:
            parts = []
            for jj in range(MXU_DIM // LANES):
                x = y[:, jj * LANES:(jj + 1) * LANES]
                partner = jnp.where(first_half, pltpu.roll(x, LANES - ROPE_HALF, 1),
                                    pltpu.roll(x, ROPE_HALF, 1))
                parts.append(x * cos + partner * sin)
            y = jnp.concatenate(parts, axis=1)
        norm_refs[t][:, out_cols] = y.astype(BF16)

    plain_refs = (va_ref, ga_ref, vb_ref, gb_ref)
    for s in range(4 * n_slab):
        cols = slice(s * MXU_DIM, (s + 1) * MXU_DIM)
        z = _dot(u, wp_ref[:, cols])
        t, half = divmod(s, n_slab)
        out_cols = slice(half * MXU_DIM, (half + 1) * MXU_DIM)
        if t % 2 == 1:
            z = z * jax.nn.sigmoid(z)
        plain_refs[t][:, out_cols] = z.astype(BF16)


def _proj(h2, g1, wn, wp, wf, bf_row, bd, gains, cos_t, sin_t):
    rows = h2.shape[0]
    row = lambda w: pl.BlockSpec((ROW_TILE, w), lambda i: (i, 0))
    full = lambda a: pl.BlockSpec(a.shape, lambda i: (0, 0))
    out_bf = jax.ShapeDtypeStruct((rows, D_BRANCH), BF16)
    return pl.pallas_call(
        _proj_kernel,
        out_shape=(out_bf,) * 8 + (jax.ShapeDtypeStruct((rows, LANES), F32),),
        grid=(rows // ROW_TILE,),
        in_specs=[row(D_MODEL), full(g1), full(wn), full(wp), full(wf), full(bf_row),
                  full(bd), full(gains), row(LANES), row(LANES)],
        out_specs=(row(D_BRANCH),) * 8 + (row(LANES),),
        compiler_params=pltpu.CompilerParams(
            dimension_semantics=("arbitrary",), vmem_limit_bytes=VMEM_LIMIT_BYTES),
        name="in_proj",
    )(h2, g1, wn, wp, wf, bf_row, bd, gains, cos_t, sin_t)


def _cumsum_kernel(x_ref, tri_ref, c_ref):
    n = x_ref.shape[1] // CUMSUM_CHUNK
    tri = tri_ref[...]
    carry = jnp.zeros((x_ref.shape[0], 1), F32)
    for j in range(n):
        cols = slice(j * CUMSUM_CHUNK, (j + 1) * CUMSUM_CHUNK)
        hi, mid, lo = _split3(x_ref[:, cols])
        c = _dot(hi, tri) + _dot(mid, tri) + _dot(lo, tri) + carry
        c_ref[:, cols] = c
        carry = c[:, CUMSUM_CHUNK - 1:CUMSUM_CHUNK]


def _cumsum(x):
    idx = np.arange(CUMSUM_CHUNK)
    tri = jnp.asarray(idx[:, None] <= idx[None, :], BF16)
    return pl.pallas_call(
        _cumsum_kernel,
        out_shape=jax.ShapeDtypeStruct(x.shape, F32),
        name="forget_cumsum",
    )(x, tri)


def _fox_kernel(q_ref, k_ref, v_ref, g_ref, c_ref, o_ref):
    i = pl.program_id(2)
    t = FOX_TILE
    q = q_ref[0]
    lane = lax.broadcasted_iota(jnp.int32, (1, LANES), 1)
    row = lax.broadcasted_iota(jnp.int32, (t, t), 0)
    col = lax.broadcasted_iota(jnp.int32, (t, t), 1)
    causal = col <= row
    outs = []
    for hh in range(2):
        in_head = (lane >= hh * HEAD_DIM) & (lane < (hh + 1) * HEAD_DIM)
        qm = jnp.where(in_head, q, jnp.zeros_like(q))

        def chunk(j, carry, masked):
            m, l, acc = carry
            start = pl.multiple_of(j * t, t)
            kc = k_ref[0, pl.ds(start, t), :]
            vc = v_ref[0, pl.ds(start, t), :]
            s = _dot_nt(qm, kc) - c_ref[0, 0, j, hh:hh + 1, :]
            if masked:
                s = jnp.where(causal, s, NEG)
            m_new = jnp.maximum(m, jnp.max(s, axis=-1, keepdims=True))
            alpha = jnp.exp(m - m_new)
            p = jnp.exp(s - m_new)
            l = alpha * l + jnp.sum(p, axis=-1, keepdims=True)
            acc = alpha * acc + _dot(p.astype(BF16), vc)
            return m_new, l, acc

        init = (jnp.full((t, 1), NEG, F32), jnp.zeros((t, 1), F32), jnp.zeros((t, LANES), F32))
        carry = lax.fori_loop(0, i, functools.partial(chunk, masked=False), init)
        _, l, acc = chunk(i, carry, True)
        outs.append(acc / l)
    o = jnp.where(lane < HEAD_DIM, outs[0], outs[1])
    o_ref[0] = (o * g_ref[0].astype(F32)).astype(BF16)


def _fox(qa, ka, va, ga, c5):
    b, s, _ = qa.shape
    groups = D_BRANCH // LANES
    q_spec = pl.BlockSpec((1, FOX_TILE, LANES), lambda bi, p, i: (bi, i, p))
    kv_spec = pl.BlockSpec((1, s, LANES), lambda bi, p, i: (bi, 0, p))
    c_spec = pl.BlockSpec((1, 1, s // FOX_TILE, 2, FOX_TILE), lambda bi, p, i: (bi, p, 0, 0, 0))
    return pl.pallas_call(
        _fox_kernel,
        out_shape=jax.ShapeDtypeStruct((b, s, D_BRANCH), BF16),
        grid=(b, groups, s // FOX_TILE),
        in_specs=[q_spec, kv_spec, kv_spec, q_spec, c_spec],
        out_specs=q_spec,
        compiler_params=pltpu.CompilerParams(
            dimension_semantics=("arbitrary",) * 3, vmem_limit_bytes=VMEM_LIMIT_BYTES),
        name="fox_attention",
    )(qa, ka, va, ga, c5)


def _dil_kernel(q_ref, k_ref, v_ref, o_ref, lse_ref, *, length):
    t = DIL_TILE
    w = t + N_BACK
    lane = lax.broadcasted_iota(jnp.int32, (1, LANES), 1)
    rel = (lax.broadcasted_iota(jnp.int32, (t, w), 0)
           - lax.broadcasted_iota(jnp.int32, (t, w), 1))

    def body(jb, carry):
        q0 = pl.multiple_of(jb * t, t)
        ws = pl.multiple_of(jnp.maximum(q0 - N_BACK, 0), N_BACK)
        q = q_ref[0, pl.ds(q0, t), :]
        kw = k_ref[0, pl.ds(ws, w), :]
        vw = v_ref[0, pl.ds(ws, w), :]
        dist = rel + (q0 - ws)
        valid = (dist >= 0) & (dist <= N_BACK)
        outs, lses = [], []
        for hh in range(2):
            in_head = (lane >= hh * HEAD_DIM) & (lane < (hh + 1) * HEAD_DIM)
            qm = jnp.where(in_head, q, jnp.zeros_like(q))
            s = jnp.where(valid, _dot_nt(qm, kw), NEG)
            m = jnp.max(s, axis=-1, keepdims=True)
            p = jnp.exp(s - m)
            l = jnp.sum(p, axis=-1, keepdims=True)
            outs.append(_dot(p.astype(BF16), vw) / l)
            lses.append(jnp.broadcast_to(m + jnp.log(l), (t, LANES)))
        o_ref[0, pl.ds(q0, t), :] = jnp.where(lane < HEAD_DIM, outs[0], outs[1])
        lse_ref[0, pl.ds(q0, t), :] = jnp.where(lane < HEAD_DIM, lses[0], lses[1])
        return carry

    lax.fori_loop(0, length // t, body, 0)


def _dilated(qb, kb, vb, dilation):
    b, s, _ = qb.shape
    length = s // dilation
    groups = D_BRANCH // LANES
    view = lambda a: a.reshape(b, length, dilation * D_BRANCH)
    spec = pl.BlockSpec((1, length, LANES), lambda bi, r, p: (bi, 0, r * groups + p))
    out = jax.ShapeDtypeStruct((b, length, dilation * D_BRANCH), F32)
    o, lse = pl.pallas_call(
        functools.partial(_dil_kernel, length=length),
        out_shape=(out, out),
        grid=(b, dilation, groups),
        in_specs=[spec, spec, spec],
        out_specs=(spec, spec),
        compiler_params=pltpu.CompilerParams(
            dimension_semantics=("arbitrary",) * 3, vmem_limit_bytes=VMEM_LIMIT_BYTES),
        name=f"dilated_attention_d{dilation}",
    )(view(qb), view(kb), view(vb))
    return o.reshape(b * s, D_BRANCH), lse.reshape(b * s, D_BRANCH)


def _post_kernel(h_ref, oa_ref, o1_ref, o2_ref, o3_ref, l1_ref, l2_ref, l3_ref, gb_ref, p_ref,
                 woa_ref, wob_ref, g2_ref, wg_ref, wple_ref, out_ref):
    l1, l2, l3 = l1_ref[...], l2_ref[...], l3_ref[...]
    mx = jnp.maximum(jnp.maximum(l1, l2), l3)
    e1, e2, e3 = jnp.exp(l1 - mx), jnp.exp(l2 - mx), jnp.exp(l3 - mx)
    ob = (e1 * o1_ref[...] + e2 * o2_ref[...] + e3 * o3_ref[...]) / (e1 + e2 + e3)
    ob = (ob * gb_ref[...].astype(F32)).astype(BF16)
    h1 = h_ref[...] + _dot(oa_ref[...], woa_ref[...]) + _dot(ob, wob_ref[...])
    ms = jnp.mean(h1 * h1, axis=-1, keepdims=True)
    u2 = (h1 * lax.rsqrt(ms + EPS) * g2_ref[...]).astype(BF16)
    gate = jax.nn.sigmoid(_dot(u2, wg_ref[...]))
    ple = _dot(p_ref[...].astype(BF16), wple_ref[...])
    out_ref[...] = h1 + ple * gate


def _post(h2, oa, dil, gb, p2, woa, wob, g2, wg, wple):
    rows = h2.shape[0]
    row = lambda w: pl.BlockSpec((ROW_TILE, w), lambda i: (i, 0))
    full = lambda a: pl.BlockSpec(a.shape, lambda i: (0, 0))
    (o1, l1), (o2, l2), (o3, l3) = dil
    return pl.pallas_call(
        _post_kernel,
        out_shape=jax.ShapeDtypeStruct((rows, D_MODEL), F32),
        grid=(rows // ROW_TILE,),
        in_specs=[row(D_MODEL)] + [row(D_BRANCH)] * 8 + [row(PLE_DIM),
                  full(woa), full(wob), full(g2), full(wg), full(wple)],
        out_specs=row(D_MODEL),
        compiler_params=pltpu.CompilerParams(
            dimension_semantics=("arbitrary",), vmem_limit_bytes=VMEM_LIMIT_BYTES),
        name="out_stage",
    )(h2, oa, o1, o2, o3, l1, l2, l3, gb, p2, woa, wob, g2, wg, wple)


def _layer(h2, p2, tables, batch, seq, norm_g, w_in, b_f, qk_g, w_out, w_ple, ple_norm_g,
           w_ple_gate):
    d = D_BRANCH
    cuts = np.cumsum([0, d, d, d, d, N_HEADS, d, d, d, d])
    qa_w, ka_w, va_w, ga_w, f_w, qb_w, kb_w, vb_w, gb_w = (
        w_in[:, cuts[i]:cuts[i + 1]] for i in range(9))
    wn = jnp.concatenate([qa_w, ka_w, qb_w, kb_w], axis=1).astype(BF16)
    wp = jnp.concatenate([va_w, ga_w, vb_w, gb_w], axis=1).astype(BF16)
    wf = jnp.pad(f_w, ((0, 0), (0, LANES - N_HEADS))).astype(BF16)
    bf_row = jnp.pad(b_f.astype(F32), (0, LANES - N_HEADS)).reshape(1, LANES)
    scale = HEAD_DIM ** -0.5
    gains = jnp.concatenate([jnp.tile(qk_g[0] * scale, N_HEADS), jnp.tile(qk_g[1], N_HEADS),
                             jnp.tile(qk_g[2] * scale, N_HEADS), jnp.tile(qk_g[3], N_HEADS)]
                            ).astype(F32).reshape(1, 4 * d)
    head_of = np.arange(MXU_DIM) // HEAD_DIM
    bd = jnp.asarray(head_of[:, None] == head_of[None, :], BF16)

    cos_t, sin_t = tables
    qa, ka, qb, kb, va, ga, vb, gb, lf = _proj(
        h2, norm_g.reshape(1, D_MODEL), wn, wp, wf, bf_row, bd, gains, cos_t, sin_t)

    lf_t = lf[:, :N_HEADS].reshape(batch, seq, N_HEADS).transpose(0, 2, 1)
    c = _cumsum(lf_t.reshape(batch * N_HEADS, seq))
    groups = d // LANES
    c5 = c.reshape(batch, groups, 2, seq // FOX_TILE, FOX_TILE).transpose(0, 1, 3, 2, 4)

    as3 = lambda a: a.reshape(batch, seq, d)
    oa = _fox(as3(qa), as3(ka), as3(va), as3(ga), c5).reshape(batch * seq, d)
    dil = [_dilated(as3(qb), as3(kb), as3(vb), dilation) for _, dilation in DILATED_PATTERNS]

    w_out_b = w_out.astype(BF16)
    return _post(h2, oa, dil, gb, p2, w_out_b[:d], w_out_b[d:], ple_norm_g.reshape(1, D_MODEL),
                 w_ple_gate.astype(BF16), w_ple.astype(BF16))


def kernel(x, p, positions, norm_g, w_in, b_f, qk_norm_g, w_out, w_ple, ple_norm_g, w_ple_gate):
    batch, seq, _ = x.shape
    depth = p.shape[0]
    assert all(w // dil == N_BACK for w, dil in DILATED_PATTERNS)
    tables = _rope_tables(positions)
    h2 = x.reshape(batch * seq, D_MODEL)
    for i in range(depth):
        h2 = _layer(h2, p[i].reshape(batch * seq, PLE_DIM), tables, batch, seq, norm_g[i],
                    w_in[i], b_f[i], qk_norm_g[i], w_out[i], w_ple[i], ple_norm_g[i],
                    w_ple_gate[i])
    return h2.reshape(batch, seq, D_MODEL)
```

```python
import functools

import numpy as np
import jax
import jax.numpy as jnp
from jax import lax
from jax.experimental import pallas as pl
from jax.experimental.pallas import tpu as pltpu

D_MODEL = 1024
HEAD_DIM = 64
N_HEADS = 8
D_BRANCH = N_HEADS * HEAD_DIM
PLE_DIM = 256
ROPE_THETA = 500000.0
ROPE_DIM = HEAD_DIM // 4
ROPE_HALF = ROPE_DIM // 2
DILATED_PATTERNS = ((128, 1), (512, 4), (2048, 16))
N_BACK = 128
EPS = 1e-6
NEG = -1e30

LANES = 128
SUBLANES = 8
BF16_ROWS = 16
VMEM_LIMIT_BYTES = 56 * 1024 * 1024

ROW_TILE = 512
FOX_T = 1024
FOX_UK = 256
FOX_UQ = 256
FOX_AHEAD = 8
FOX_VROWS = HEAD_DIM + BF16_ROWS
LOG2E = 1.4426950408889634
DIL_TQ = N_BACK
DIL_TW = DIL_TQ + N_BACK
DIL_SUPER = max(d for _, d in DILATED_PATTERNS) * DIL_TQ
DIL_GROUP = 8
DIL_AHEAD = 4
F_ROWS = BF16_ROWS

F32 = jnp.float32
BF16 = jnp.bfloat16


def _dot(a, b):
    return jnp.dot(a, b, preferred_element_type=F32)


def _dot_nt(a, b):
    return lax.dot_general(a, b, (((1,), (1,)), ((), ())), preferred_element_type=F32)


def _dot_tn(a, b):
    return lax.dot_general(a, b, (((0,), (0,)), ((), ())), preferred_element_type=F32)


def _split3(x):
    hi = x.astype(BF16).astype(F32)
    r = x - hi
    mid = r.astype(BF16).astype(F32)
    lo = (r - mid).astype(BF16).astype(F32)
    return hi, mid, lo


def _rms_norm_rows(h, g_row):
    ms = jnp.mean(h * h, axis=-1, keepdims=True)
    return h * lax.rsqrt(ms + EPS) * g_row


def _proj_kernel(h_ref, pos_ref, g1_ref, wt_ref, wn_ref, gcol_ref, bf_ref, invf_ref, tri_ref,
                 qt_ref, k_ref, vt_ref, gt_ref, qb_ref, kb_ref, vb_ref, gb_ref, carry_ref):
    t = pl.program_id(1)
    tm = ROW_TILE
    d = D_BRANCH
    u = _rms_norm_rows(h_ref[...], g1_ref[...]).astype(BF16)

    zn = _dot(u, wn_ref[...])
    vb_ref[...] = zn[:, :d]
    gate_b = zn[:, d:]
    gb_ref[...] = (gate_b * jax.nn.sigmoid(gate_b)).astype(BF16)

    def feat_major(group):
        return _dot_nt(wt_ref[group * d:(group + 1) * d, :], u)

    def head_norm(z, group):
        z3 = z.reshape(N_HEADS, HEAD_DIM, tm)
        ss = jnp.sum(z3 * z3, axis=1, keepdims=True)
        y3 = z3 * lax.rsqrt(ss * (1.0 / HEAD_DIM) + EPS)
        return y3.reshape(d, tm) * gcol_ref[group * d:(group + 1) * d, :]

    ang = invf_ref[...] * pos_ref[...].astype(F32)
    cos = jnp.cos(ang)
    sin = jnp.sin(ang)

    def rope(y):
        y3 = y.reshape(N_HEADS, HEAD_DIM, tm)
        x1 = y3[:, :ROPE_HALF, :]
        x2 = y3[:, ROPE_HALF:ROPE_DIM, :]
        rot = [x1 * cos - x2 * sin, x2 * cos + x1 * sin, y3[:, ROPE_DIM:, :]]
        return jnp.concatenate(rot, axis=1).reshape(d, tm)

    f = _dot_nt(wt_ref[6 * d:6 * d + F_ROWS, :], u) + bf_ref[...]
    lf = jnp.minimum(f, 0.0) - jnp.log1p(jnp.exp(-jnp.abs(f)))
    stacked = jnp.concatenate([x.astype(BF16) for x in _split3(lf)], axis=0)
    cs = _dot(stacked, tri_ref[...])
    carry = jnp.where(t == 0, 0.0, carry_ref[:, 0:1])
    c = cs[:F_ROWS] + cs[F_ROWS:2 * F_ROWS] + cs[2 * F_ROWS:] + carry
    carry_ref[...] = jnp.broadcast_to(c[:, tm - 1:tm], (F_ROWS, LANES))
    c2 = c * LOG2E

    qa = head_norm(feat_major(0), 0)
    ka = head_norm(feat_major(1), 1)
    row = lax.broadcasted_iota(jnp.int32, (SUBLANES, tm), 0)
    pad = jnp.zeros((HEAD_DIM - SUBLANES, tm), F32)
    for hd in range(N_HEADS):
        c_hi, c_mid, c_lo = _split3(c2[hd:hd + 1, :])
        q_aug = jnp.where(row == 0, c_hi, jnp.where(row == 1, c_mid, jnp.where(
            row == 2, c_lo, jnp.where(row < 6, 1.0, 0.0))))
        k_aug = jnp.where(row < 3, 1.0, jnp.where(row == 3, -c_hi, jnp.where(
            row == 4, -c_mid, jnp.where(row == 5, -c_lo, 0.0))))
        rows = slice(hd * HEAD_DIM, (hd + 1) * HEAD_DIM)
        qt_ref[0, hd] = jnp.concatenate([qa[rows], q_aug, pad], axis=0).astype(BF16)
        k_ref[0, hd] = jnp.concatenate([ka[rows], k_aug, pad], axis=0).T.astype(BF16)

    ones_row = jnp.where(lax.broadcasted_iota(jnp.int32, (N_HEADS, BF16_ROWS, tm), 1) == 0,
                         1.0, 0.0)
    va = feat_major(2).reshape(N_HEADS, HEAD_DIM, tm)
    vt_ref[0, :, 0] = jnp.concatenate([va, ones_row], axis=1).astype(BF16)
    gate_a = feat_major(3)
    gt_ref[0] = (gate_a * jax.nn.sigmoid(gate_a)).reshape(N_HEADS, HEAD_DIM, tm).astype(BF16)

    qb_ref[...] = rope(head_norm(feat_major(4), 2)).T
    kb_ref[...] = rope(head_norm(feat_major(5), 3)).T


def _proj(h2, pos_row, batch, seq, g1, wt, wn, gcol, bf_col, invf, tri):
    nt = seq // ROW_TILE
    rows = batch * seq
    d = D_BRANCH
    row = lambda w: pl.BlockSpec((ROW_TILE, w), lambda b, t: (b * nt + t, 0))
    full = lambda a: pl.BlockSpec(a.shape, lambda b, t: (0,) * a.ndim)
    out_shape = (
        jax.ShapeDtypeStruct((batch, N_HEADS, 2 * HEAD_DIM, seq), BF16),
        jax.ShapeDtypeStruct((batch, N_HEADS, seq, 2 * HEAD_DIM), BF16),
        jax.ShapeDtypeStruct((batch, N_HEADS, nt, FOX_VROWS, ROW_TILE), BF16),
        jax.ShapeDtypeStruct((batch, N_HEADS, HEAD_DIM, seq), BF16),
        jax.ShapeDtypeStruct((rows, d), F32),
        jax.ShapeDtypeStruct((rows, d), F32),
        jax.ShapeDtypeStruct((rows, d), F32),
        jax.ShapeDtypeStruct((rows, d), BF16),
    )
    out_specs = (
        pl.BlockSpec((1, N_HEADS, 2 * HEAD_DIM, ROW_TILE), lambda b, t: (b, 0, 0, t)),
        pl.BlockSpec((1, N_HEADS, ROW_TILE, 2 * HEAD_DIM), lambda b, t: (b, 0, t, 0)),
        pl.BlockSpec((1, N_HEADS, 1, FOX_VROWS, ROW_TILE), lambda b, t: (b, 0, t, 0, 0)),
        pl.BlockSpec((1, N_HEADS, HEAD_DIM, ROW_TILE), lambda b, t: (b, 0, 0, t)),
        row(d), row(d), row(d), row(d),
    )
    return pl.pallas_call(
        _proj_kernel,
        out_shape=out_shape,
        grid=(batch, nt),
        in_specs=[row(D_MODEL), pl.BlockSpec((1, ROW_TILE), lambda b, t: (0, b * nt + t)),
                  full(g1), full(wt), full(wn), full(gcol), full(bf_col), full(invf), full(tri)],
        out_specs=out_specs,
        scratch_shapes=[pltpu.VMEM((F_ROWS, LANES), F32)],
        compiler_params=pltpu.CompilerParams(
            dimension_semantics=("arbitrary", "arbitrary"), vmem_limit_bytes=VMEM_LIMIT_BYTES),
        name="in_proj",
    )(h2, pos_row, g1, wt, wn, gcol, bf_col, invf, tri)


def _fox_kernel(q_ref, k_ref, v_ref, g_ref, o_ref):
    i = pl.program_id(2)
    uk, uq = FOX_UK, FOX_UQ
    n_q = FOX_T // uq
    q = [q_ref[0, 0, :, a * uq:(a + 1) * uq] for a in range(n_q)]
    rel = (lax.broadcasted_iota(jnp.int32, (uk, uq), 0)
           - lax.broadcasted_iota(jnp.int32, (uk, uq), 1))

    def scores(j, ks, a):
        start = pl.multiple_of(j * FOX_T, FOX_T)
        return _dot(k_ref[0, 0, pl.ds(start + ks * uk, uk), :], q[a])

    def consume(s, v_sub, state, max_rel):
        m, acc = state
        if max_rel is not None:
            s = jnp.where(rel <= max_rel, s, NEG)
        m_new = jnp.maximum(m, jnp.max(s, axis=0, keepdims=True))
        alpha = jnp.exp2(m - m_new)
        p = jnp.exp2((s - m_new).astype(BF16))
        acc = alpha * acc + _dot(v_sub, p)
        return m_new, acc

    def step(j, carry, diagonal):
        s_first, states = carry
        states = list(states)
        units = []
        for ks in range(FOX_T // uk):
            for a in range(n_q):
                max_rel = None
                if diagonal:
                    if ks * uk >= (a + 1) * uq:
                        continue
                    if (ks + 1) * uk - 1 > a * uq:
                        max_rel = a * uq - ks * uk
                units.append((ks, a, max_rel))
        pending = {0: s_first}
        for n in range(1, min(FOX_AHEAD, len(units))):
            pending[n] = scores(j, units[n][0], units[n][1])
        s_next = s_first
        for n, (ks, a, max_rel) in enumerate(units):
            ahead = n + FOX_AHEAD
            if ahead < len(units):
                pending[ahead] = scores(j, units[ahead][0], units[ahead][1])
            elif ahead == len(units) and not diagonal:
                s_next = scores(j + 1, 0, 0)
            chunk, off = divmod(ks * uk, ROW_TILE)
            v_sub = v_ref[0, 0, j * (FOX_T // ROW_TILE) + chunk, :, off:off + uk]
            states[a] = consume(pending.pop(n), v_sub, states[a], max_rel)
        return s_next, tuple(states)

    init = tuple((jnp.full((1, uq), NEG, F32), jnp.zeros((FOX_VROWS, uq), F32))
                 for _ in range(n_q))
    carry = lax.fori_loop(0, i, functools.partial(step, diagonal=False), (scores(0, 0, 0), init))
    _, states = step(i, carry, True)
    out = jnp.concatenate([acc[:HEAD_DIM] / acc[HEAD_DIM:HEAD_DIM + 1] for _, acc in states],
                          axis=1)
    o_ref[0, 0] = (out * g_ref[0, 0].astype(F32)).astype(BF16)


def _fox(qt, k, vt, gt):
    b, _, _, s = qt.shape
    q_spec = pl.BlockSpec((1, 1, 2 * HEAD_DIM, FOX_T), lambda bi, h, i: (bi, h, 0, i))
    k_spec = pl.BlockSpec((1, 1, s, 2 * HEAD_DIM), lambda bi, h, i: (bi, h, 0, 0))
    v_spec = pl.BlockSpec((1, 1, s // ROW_TILE, FOX_VROWS, ROW_TILE),
                          lambda bi, h, i: (bi, h, 0, 0, 0))
    o_spec = pl.BlockSpec((1, 1, HEAD_DIM, FOX_T), lambda bi, h, i: (bi, h, 0, i))
    return pl.pallas_call(
        _fox_kernel,
        out_shape=jax.ShapeDtypeStruct((b, N_HEADS, HEAD_DIM, s), BF16),
        grid=(b, N_HEADS, s // FOX_T),
        in_specs=[q_spec, k_spec, v_spec, o_spec],
        out_specs=o_spec,
        compiler_params=pltpu.CompilerParams(
            dimension_semantics=("arbitrary",) * 3, vmem_limit_bytes=VMEM_LIMIT_BYTES),
        name="fox_attention",
    )(qt, k, vt, gt)


def _dil_kernel(q_ref, k_ref, v_ref, g_ref, o_ref, acc_s, m_s, l_s, *, seq):
    tq, tw, sup = DIL_TQ, DIL_TW, DIL_SUPER
    lane = lax.broadcasted_iota(jnp.int32, (1, LANES), 1)
    low = lane < HEAD_DIM
    rel = (lax.broadcasted_iota(jnp.int32, (tq, tw), 0)
           - lax.broadcasted_iota(jnp.int32, (tq, tw), 1))

    def issue(d, base, idx, nblk):
        r = idx // nblk
        jb = idx % nblk
        qs = base // d + jb * tq
        ws = jnp.maximum(qs - N_BACK, 0)
        q = q_ref[0, pl.ds(d * qs + r, tq, stride=d), :].astype(BF16)
        kw = k_ref[0, pl.ds(d * ws + r, tw, stride=d), :].astype(BF16)
        zero = jnp.zeros_like(q)
        s = [_dot_nt(jnp.where(low, q, zero), kw), _dot_nt(jnp.where(low, zero, q), kw)]
        return s, qs - ws, d * ws + r, d * (qs - base // d) + r

    def consume(g, d, s, q_minus_w, w_row, out_row):
        vw = v_ref[0, pl.ds(w_row, tw, stride=d), :].astype(BF16)
        dist = rel + q_minus_w
        valid = (dist >= 0) & (dist <= N_BACK)
        accs, ms, ls = [], [], []
        for hh in range(2):
            sm = jnp.where(valid, s[hh], NEG)
            m = jnp.max(sm, axis=-1, keepdims=True)
            p = jnp.exp2(sm - m)
            accs.append(_dot(p.astype(BF16), vw))
            ms.append(jnp.broadcast_to(m, (tq, LANES)))
            ls.append(jnp.broadcast_to(jnp.sum(p, axis=-1, keepdims=True), (tq, LANES)))
        rows = pl.ds(out_row, tq, stride=d)
        acc_s[g, rows, :] = jnp.where(low, accs[0], accs[1])
        m_s[g, rows, :] = jnp.where(low, ms[0], ms[1])
        l_s[g, rows, :] = jnp.where(low, ls[0], ls[1])

    def superblock(sb, carry):
        base = pl.multiple_of(sb * sup, sup)
        for g, (_, d) in enumerate(DILATED_PATTERNS):
            nblk = sup // (d * tq)

            def body(it, c, g=g, d=d, nblk=nblk):
                first = it * DIL_GROUP
                pending = {n: issue(d, base, first + n, nblk) for n in range(DIL_AHEAD)}
                for n in range(DIL_GROUP):
                    if n + DIL_AHEAD < DIL_GROUP:
                        pending[n + DIL_AHEAD] = issue(d, base, first + n + DIL_AHEAD, nblk)
                    consume(g, d, *pending.pop(n))
                return c

            lax.fori_loop(0, d * nblk // DIL_GROUP, body, 0)

        rows = pl.ds(base, sup)
        m1, m2, m3 = m_s[0], m_s[1], m_s[2]
        mx = jnp.maximum(jnp.maximum(m1, m2), m3)
        e1, e2, e3 = jnp.exp2(m1 - mx), jnp.exp2(m2 - mx), jnp.exp2(m3 - mx)
        num = e1 * acc_s[0] + e2 * acc_s[1] + e3 * acc_s[2]
        den = e1 * l_s[0] + e2 * l_s[1] + e3 * l_s[2]
        o_ref[0, rows, :] = (num / den * g_ref[0, rows, :].astype(F32)).astype(BF16)
        return carry

    lax.fori_loop(0, seq // sup, superblock, 0)


def _dilated(qb, kb, vb, gb):
    b, s, _ = qb.shape
    spec = pl.BlockSpec((1, s, LANES), lambda bi, p: (bi, 0, p))
    scratch = pltpu.VMEM((len(DILATED_PATTERNS), DIL_SUPER, LANES), F32)
    return pl.pallas_call(
        functools.partial(_dil_kernel, seq=s),
        out_shape=jax.ShapeDtypeStruct((b, s, D_BRANCH), BF16),
        grid=(b, D_BRANCH // LANES),
        in_specs=[spec, spec, spec, spec],
        out_specs=spec,
        scratch_shapes=[scratch, scratch, scratch],
        compiler_params=pltpu.CompilerParams(
            dimension_semantics=("arbitrary",) * 2, vmem_limit_bytes=VMEM_LIMIT_BYTES),
        name="dilated_attn",
    )(qb, kb, vb, gb)


def _post_kernel(h_ref, oat_ref, ob_ref, p_ref, woa_ref, wob_ref, g2_ref, wg_ref, wple_ref,
                 out_ref):
    h1 = h_ref[...] + _dot_tn(oat_ref[0], woa_ref[...]) + _dot(ob_ref[...], wob_ref[...])
    u2 = _rms_norm_rows(h1, g2_ref[...]).astype(BF16)
    gate = jax.nn.sigmoid(_dot(u2, wg_ref[...]))
    ple = _dot(p_ref[...].astype(BF16), wple_ref[...])
    out_ref[...] = h1 + ple * gate


def _post(h2, oat, ob, p2, batch, seq, woa, wob, g2, wg, wple):
    nt = seq // ROW_TILE
    row = lambda w: pl.BlockSpec((ROW_TILE, w), lambda b, t: (b * nt + t, 0))
    full = lambda a: pl.BlockSpec(a.shape, lambda b, t: (0,) * a.ndim)
    return pl.pallas_call(
        _post_kernel,
        out_shape=jax.ShapeDtypeStruct((batch * seq, D_MODEL), F32),
        grid=(batch, nt),
        in_specs=[row(D_MODEL), pl.BlockSpec((1, D_BRANCH, ROW_TILE), lambda b, t: (b, 0, t)),
                  row(D_BRANCH), row(PLE_DIM),
                  full(woa), full(wob), full(g2), full(wg), full(wple)],
        out_specs=row(D_MODEL),
        compiler_params=pltpu.CompilerParams(
            dimension_semantics=("arbitrary", "arbitrary"), vmem_limit_bytes=VMEM_LIMIT_BYTES),
        name="out_stage",
    )(h2, oat, ob, p2, woa, wob, g2, wg, wple)


def _layer(h2, p2, pos_row, invf, tri, batch, seq, norm_g, w_in, b_f, qk_g, w_out, w_ple,
           ple_norm_g, w_ple_gate):
    d = D_BRANCH
    cuts = np.cumsum([0, d, d, d, d, N_HEADS, d, d, d, d])
    qa_w, ka_w, va_w, ga_w, f_w, qb_w, kb_w, vb_w, gb_w = (
        w_in[:, cuts[i]:cuts[i + 1]] for i in range(9))
    f_w = jnp.pad(f_w, ((0, 0), (0, F_ROWS - N_HEADS)))
    wt = jnp.concatenate([qa_w, ka_w, va_w, ga_w, qb_w, kb_w, f_w], axis=1).T.astype(BF16)
    wn = jnp.concatenate([vb_w, gb_w], axis=1).astype(BF16)
    bf_col = jnp.pad(b_f.astype(F32), (0, F_ROWS - N_HEADS)).reshape(F_ROWS, 1)
    scale = HEAD_DIM ** -0.5
    gcol = jnp.concatenate([jnp.tile(qk_g[0] * (scale * LOG2E), N_HEADS), jnp.tile(qk_g[1], N_HEADS),
                            jnp.tile(qk_g[2] * (scale * LOG2E), N_HEADS), jnp.tile(qk_g[3], N_HEADS)]
                           ).astype(F32).reshape(4 * d, 1)

    qt, k, vt, gt, qb, kb, vb, gb = _proj(
        h2, pos_row, batch, seq, norm_g.reshape(1, D_MODEL), wt, wn, gcol, bf_col, invf, tri)

    oat = _fox(qt, k, vt, gt).reshape(batch, d, seq)
    as3 = lambda a: a.reshape(batch, seq, d)
    ob = _dilated(as3(qb), as3(kb), as3(vb), as3(gb)).reshape(batch * seq, d)

    w_out_b = w_out.astype(BF16)
    return _post(h2, oat, ob, p2, batch, seq, w_out_b[:d], w_out_b[d:],
                 ple_norm_g.reshape(1, D_MODEL), w_ple_gate.astype(BF16), w_ple.astype(BF16))


def kernel(x, p, positions, norm_g, w_in, b_f, qk_norm_g, w_out, w_ple, ple_norm_g, w_ple_gate):
    batch, seq, _ = x.shape
    depth = p.shape[0]
    assert all(w // dil == N_BACK for w, dil in DILATED_PATTERNS)
    assert seq % DIL_SUPER == 0 and seq % ROW_TILE == 0 and FOX_T % FOX_UQ == 0
    pos_row = positions.reshape(1, batch * seq)
    inv_freq = ROPE_THETA ** (-jnp.arange(ROPE_HALF, dtype=F32) / ROPE_HALF)
    invf = inv_freq.reshape(ROPE_HALF, 1)
    idx = np.arange(ROW_TILE)
    tri = jnp.asarray(idx[:, None] <= idx[None, :], BF16)
    h2 = x.reshape(batch * seq, D_MODEL)
    for i in range(depth):
        h2 = _layer(h2, p[i].reshape(batch * seq, PLE_DIM), pos_row, invf, tri, batch, seq,
                    norm_g[i], w_in[i], b_f[i], qk_norm_g[i], w_out[i], w_ple[i],
                    ple_norm_g[i], w_ple_gate[i])
    return h2.reshape(batch, seq, D_MODEL)
```

```python
import functools

import numpy as np
import jax
import jax.numpy as jnp
from jax import lax
from jax.experimental import pallas as pl
from jax.experimental.pallas import tpu as pltpu

D_MODEL = 1024
HEAD_DIM = 64
N_HEADS = 8
D_BRANCH = N_HEADS * HEAD_DIM
PLE_DIM = 256
ROPE_THETA = 500000.0
ROPE_DIM = HEAD_DIM // 4
ROPE_HALF = ROPE_DIM // 2
DILATED_PATTERNS = ((128, 1), (512, 4), (2048, 16))
N_BACK = 128
EPS = 1e-6
NEG = -1e30

LANES = 128
SUBLANES = 8
BF16_ROWS = 16
VMEM_LIMIT_BYTES = 56 * 1024 * 1024

ROW_TILE = 512
FOX_T = 1024
FOX_UK = 256
FOX_UQ = 256
FOX_AHEAD = 8
FOX_VROWS = HEAD_DIM + BF16_ROWS
FOX_SKIP_BITS = 150.0
LOG2E = 1.4426950408889634
DIL_TQ = N_BACK
DIL_TW = DIL_TQ + N_BACK
DIL_SUPER = max(d for _, d in DILATED_PATTERNS) * DIL_TQ
DIL_GROUP = 8
DIL_AHEAD = 4
F_ROWS = BF16_ROWS

F32 = jnp.float32
BF16 = jnp.bfloat16


def _dot(a, b):
    return jnp.dot(a, b, preferred_element_type=F32)


def _dot_nt(a, b):
    return lax.dot_general(a, b, (((1,), (1,)), ((), ())), preferred_element_type=F32)


def _dot_tn(a, b):
    return lax.dot_general(a, b, (((0,), (0,)), ((), ())), preferred_element_type=F32)


def _split3(x):
    hi = x.astype(BF16).astype(F32)
    r = x - hi
    mid = r.astype(BF16).astype(F32)
    lo = (r - mid).astype(BF16).astype(F32)
    return hi, mid, lo


def _rms_norm_rows(h, g_row):
    ms = jnp.mean(h * h, axis=-1, keepdims=True)
    return h * lax.rsqrt(ms + EPS) * g_row


def _proj_kernel(h_ref, pos_ref, g1_ref, wt_ref, wn_ref, gcol_ref, bf_ref, invf_ref, tri_ref,
                 qt_ref, k_ref, vt_ref, gt_ref, qb_ref, kb_ref, vb_ref, gb_ref, stats_ref,
                 carry_ref):
    t = pl.program_id(1)
    tm = ROW_TILE
    d = D_BRANCH
    u = _rms_norm_rows(h_ref[...], g1_ref[...]).astype(BF16)

    zn = _dot(u, wn_ref[...])
    vb_ref[...] = zn[:, :d]
    gate_b = zn[:, d:]
    gb_ref[...] = (gate_b * jax.nn.sigmoid(gate_b)).astype(BF16)

    def feat_major(group):
        return _dot_nt(wt_ref[group * d:(group + 1) * d, :], u)

    def head_norm(z, group):
        z3 = z.reshape(N_HEADS, HEAD_DIM, tm)
        ss = jnp.sum(z3 * z3, axis=1, keepdims=True)
        y3 = z3 * lax.rsqrt(ss * (1.0 / HEAD_DIM) + EPS)
        return y3.reshape(d, tm) * gcol_ref[group * d:(group + 1) * d, :]

    ang = invf_ref[...] * pos_ref[...].astype(F32)
    cos = jnp.cos(ang)
    sin = jnp.sin(ang)

    def rope(y):
        y3 = y.reshape(N_HEADS, HEAD_DIM, tm)
        x1 = y3[:, :ROPE_HALF, :]
        x2 = y3[:, ROPE_HALF:ROPE_DIM, :]
        rot = [x1 * cos - x2 * sin, x2 * cos + x1 * sin, y3[:, ROPE_DIM:, :]]
        return jnp.concatenate(rot, axis=1).reshape(d, tm)

    f = _dot_nt(wt_ref[6 * d:6 * d + F_ROWS, :], u) + bf_ref[...]
    lf = jnp.minimum(f, 0.0) - jnp.log1p(jnp.exp(-jnp.abs(f)))
    stacked = jnp.concatenate([x.astype(BF16) for x in _split3(lf)], axis=0)
    cs = _dot(stacked, tri_ref[...])
    carry = jnp.where(t == 0, 0.0, carry_ref[:, 0:1])
    c = cs[:F_ROWS] + cs[F_ROWS:2 * F_ROWS] + cs[2 * F_ROWS:] + carry
    carry_ref[...] = jnp.broadcast_to(c[:, tm - 1:tm], (F_ROWS, LANES))
    c2 = c * LOG2E

    qa = head_norm(feat_major(0), 0)
    ka = head_norm(feat_major(1), 1)
    row = lax.broadcasted_iota(jnp.int32, (SUBLANES, tm), 0)
    pad = jnp.zeros((HEAD_DIM - SUBLANES, tm), F32)
    for hd in range(N_HEADS):
        c_hi, c_mid, c_lo = _split3(c2[hd:hd + 1, :])
        q_aug = jnp.where(row == 0, c_hi, jnp.where(row == 1, c_mid, jnp.where(
            row == 2, c_lo, jnp.where(row < 6, 1.0, 0.0))))
        k_aug = jnp.where(row < 3, 1.0, jnp.where(row == 3, -c_hi, jnp.where(
            row == 4, -c_mid, jnp.where(row == 5, -c_lo, 0.0))))
        rows = slice(hd * HEAD_DIM, (hd + 1) * HEAD_DIM)
        qt_ref[0, hd] = jnp.concatenate([qa[rows], q_aug, pad], axis=0).astype(BF16)
        k_ref[0, hd] = jnp.concatenate([ka[rows], k_aug, pad], axis=0).T.astype(BF16)

    ones_row = jnp.where(lax.broadcasted_iota(jnp.int32, (N_HEADS, BF16_ROWS, tm), 1) == 0,
                         1.0, 0.0)
    va = feat_major(2).reshape(N_HEADS, HEAD_DIM, tm)
    vt_ref[0, :, 0] = jnp.concatenate([va, ones_row], axis=1).astype(BF16)
    gate_a = feat_major(3)
    gt_ref[0] = (gate_a * jax.nn.sigmoid(gate_a)).reshape(N_HEADS, HEAD_DIM, tm).astype(BF16)

    qb_ref[...] = rope(head_norm(feat_major(4), 2)).T
    kb_ref[...] = rope(head_norm(feat_major(5), 3)).T

    def max_sq_norm(y):
        y3 = y.reshape(N_HEADS, HEAD_DIM, tm)
        return jnp.max(jnp.sum(y3 * y3, axis=1), axis=-1, keepdims=True)

    lane = lax.broadcasted_iota(jnp.int32, (N_HEADS, LANES), 1)
    stats_ref[0] = jnp.where(lane == 0, max_sq_norm(qa), jnp.where(
        lane == 1, max_sq_norm(ka), jnp.where(
            lane == 2, c2[:N_HEADS, 0:1], jnp.where(lane == 3, c2[:N_HEADS, tm - 1:tm], 0.0))))


def _proj(h2, pos_row, batch, seq, g1, wt, wn, gcol, bf_col, invf, tri):
    nt = seq // ROW_TILE
    rows = batch * seq
    d = D_BRANCH
    row = lambda w: pl.BlockSpec((ROW_TILE, w), lambda b, t: (b * nt + t, 0))
    full = lambda a: pl.BlockSpec(a.shape, lambda b, t: (0,) * a.ndim)
    out_shape = (
        jax.ShapeDtypeStruct((batch, N_HEADS, 2 * HEAD_DIM, seq), BF16),
        jax.ShapeDtypeStruct((batch, N_HEADS, seq, 2 * HEAD_DIM), BF16),
        jax.ShapeDtypeStruct((batch, N_HEADS, nt, FOX_VROWS, ROW_TILE), BF16),
        jax.ShapeDtypeStruct((batch, N_HEADS, HEAD_DIM, seq), BF16),
        jax.ShapeDtypeStruct((rows, d), F32),
        jax.ShapeDtypeStruct((rows, d), F32),
        jax.ShapeDtypeStruct((rows, d), F32),
        jax.ShapeDtypeStruct((rows, d), BF16),
        jax.ShapeDtypeStruct((batch * nt, N_HEADS, LANES), F32),
    )
    out_specs = (
        pl.BlockSpec((1, N_HEADS, 2 * HEAD_DIM, ROW_TILE), lambda b, t: (b, 0, 0, t)),
        pl.BlockSpec((1, N_HEADS, ROW_TILE, 2 * HEAD_DIM), lambda b, t: (b, 0, t, 0)),
        pl.BlockSpec((1, N_HEADS, 1, FOX_VROWS, ROW_TILE), lambda b, t: (b, 0, t, 0, 0)),
        pl.BlockSpec((1, N_HEADS, HEAD_DIM, ROW_TILE), lambda b, t: (b, 0, 0, t)),
        row(d), row(d), row(d), row(d),
        pl.BlockSpec((1, N_HEADS, LANES), lambda b, t: (b * nt + t, 0, 0)),
    )
    return pl.pallas_call(
        _proj_kernel,
        out_shape=out_shape,
        grid=(batch, nt),
        in_specs=[row(D_MODEL), pl.BlockSpec((1, ROW_TILE), lambda b, t: (0, b * nt + t)),
                  full(g1), full(wt), full(wn), full(gcol), full(bf_col), full(invf), full(tri)],
        out_specs=out_specs,
        scratch_shapes=[pltpu.VMEM((F_ROWS, LANES), F32)],
        compiler_params=pltpu.CompilerParams(
            dimension_semantics=("arbitrary", "arbitrary"), vmem_limit_bytes=VMEM_LIMIT_BYTES),
        name="in_proj",
    )(h2, pos_row, g1, wt, wn, gcol, bf_col, invf, tri)


def _fox_kernel(first_ref, q_ref, k_ref, v_ref, g_ref, o_ref):
    i = pl.program_id(2)
    first = first_ref[(pl.program_id(0) * N_HEADS + pl.program_id(1)) * pl.num_programs(2) + i]
    uk, uq = FOX_UK, FOX_UQ
    n_q = FOX_T // uq
    q = [q_ref[0, 0, :, a * uq:(a + 1) * uq] for a in range(n_q)]
    rel = (lax.broadcasted_iota(jnp.int32, (uk, uq), 0)
           - lax.broadcasted_iota(jnp.int32, (uk, uq), 1))

    def scores(j, ks, a):
        start = pl.multiple_of(j * FOX_T, FOX_T)
        return _dot(k_ref[0, 0, pl.ds(start + ks * uk, uk), :], q[a])

    def consume(s, v_sub, state, max_rel):
        m, acc = state
        if max_rel is not None:
            s = jnp.where(rel <= max_rel, s, NEG)
        m_new = jnp.maximum(m, jnp.max(s, axis=0, keepdims=True))
        alpha = jnp.exp2(m - m_new)
        p = jnp.exp2((s - m_new).astype(BF16))
        acc = alpha * acc + _dot(v_sub, p)
        return m_new, acc

    def step(j, carry, diagonal):
        s_first, states = carry
        states = list(states)
        units = []
        for ks in range(FOX_T // uk):
            for a in range(n_q):
                max_rel = None
                if diagonal:
                    if ks * uk >= (a + 1) * uq:
                        continue
                    if (ks + 1) * uk - 1 > a * uq:
                        max_rel = a * uq - ks * uk
                units.append((ks, a, max_rel))
        pending = {0: s_first}
        for n in range(1, min(FOX_AHEAD, len(units))):
            pending[n] = scores(j, units[n][0], units[n][1])
        s_next = s_first
        for n, (ks, a, max_rel) in enumerate(units):
            ahead = n + FOX_AHEAD
            if ahead < len(units):
                pending[ahead] = scores(j, units[ahead][0], units[ahead][1])
            elif ahead == len(units) and not diagonal:
                s_next = scores(j + 1, 0, 0)
            chunk, off = divmod(ks * uk, ROW_TILE)
            v_sub = v_ref[0, 0, j * (FOX_T // ROW_TILE) + chunk, :, off:off + uk]
            states[a] = consume(pending.pop(n), v_sub, states[a], max_rel)
        return s_next, tuple(states)

    init = tuple((jnp.full((1, uq), NEG, F32), jnp.zeros((FOX_VROWS, uq), F32))
                 for _ in range(n_q))
    carry = lax.fori_loop(first, i, functools.partial(step, diagonal=False),
                          (scores(first, 0, 0), init))
    _, states = step(i, carry, True)
    out = jnp.concatenate([acc[:HEAD_DIM] / acc[HEAD_DIM:HEAD_DIM + 1] for _, acc in states],
                          axis=1)
    o_ref[0, 0] = (out * g_ref[0, 0].astype(F32)).astype(BF16)


def _fox_first_step(stats, batch, seq):
    nt, per, nq = seq // ROW_TILE, FOX_T // ROW_TILE, seq // FOX_T
    st = stats.reshape(batch, nt, N_HEADS, LANES)
    q_sq = st[..., 0].reshape(batch, nq, per, N_HEADS).max(axis=2)
    k_sq = st[..., 1].max(axis=1, keepdims=True)
    c_first = st[:, ::per, :, 2]
    c_last = st[:, per - 1::per, :, 3]
    bound = jnp.sqrt(q_sq * k_sq) * 1.01 + 1.0
    gap = c_first[:, :, None, :] - c_last[:, None, :, :]
    skip = (2.0 * bound[:, :, None, :] + gap) < -FOX_SKIP_BITS
    steps = jnp.arange(nq, dtype=jnp.int32)
    first = jnp.min(jnp.where(skip, nq, steps[None, None, :, None]), axis=2)
    first = jnp.minimum(first, steps[None, :, None])
    return first.transpose(0, 2, 1).reshape(-1)


def _fox(first, qt, k, vt, gt):
    b, _, _, s = qt.shape
    q_spec = pl.BlockSpec((1, 1, 2 * HEAD_DIM, FOX_T), lambda bi, h, i, f: (bi, h, 0, i))
    k_spec = pl.BlockSpec((1, 1, s, 2 * HEAD_DIM), lambda bi, h, i, f: (bi, h, 0, 0))
    v_spec = pl.BlockSpec((1, 1, s // ROW_TILE, FOX_VROWS, ROW_TILE),
                          lambda bi, h, i, f: (bi, h, 0, 0, 0))
    o_spec = pl.BlockSpec((1, 1, HEAD_DIM, FOX_T), lambda bi, h, i, f: (bi, h, 0, i))
    return pl.pallas_call(
        _fox_kernel,
        out_shape=jax.ShapeDtypeStruct((b, N_HEADS, HEAD_DIM, s), BF16),
        grid_spec=pltpu.PrefetchScalarGridSpec(
            num_scalar_prefetch=1,
            grid=(b, N_HEADS, s // FOX_T),
            in_specs=[q_spec, k_spec, v_spec, o_spec],
            out_specs=o_spec),
        compiler_params=pltpu.CompilerParams(
            dimension_semantics=("arbitrary",) * 3, vmem_limit_bytes=VMEM_LIMIT_BYTES),
        name="fox_attention",
    )(first, qt, k, vt, gt)


def _dil_kernel(q_ref, k_ref, v_ref, g_ref, o_ref, acc_s, m_s, l_s, *, seq):
    tq, tw, sup = DIL_TQ, DIL_TW, DIL_SUPER
    lane = lax.broadcasted_iota(jnp.int32, (1, LANES), 1)
    low = lane < HEAD_DIM
    rel = (lax.broadcasted_iota(jnp.int32, (tq, tw), 0)
           - lax.broadcasted_iota(jnp.int32, (tq, tw), 1))

    def band_bias(q_minus_w):
        dist = rel + q_minus_w
        return jnp.where(dist >= 0, jnp.where(dist <= N_BACK, 0.0, NEG), NEG)

    bias_inner = band_bias(N_BACK)
    bias_start = band_bias(0)

    def issue(d, base, idx, nblk):
        r = idx // nblk
        jb = idx % nblk
        qs = base // d + jb * tq
        ws = jnp.maximum(qs - N_BACK, 0)
        q = q_ref[0, pl.ds(d * qs + r, tq, stride=d), :].astype(BF16)
        kw = k_ref[0, pl.ds(d * ws + r, tw, stride=d), :].astype(BF16)
        zero = jnp.zeros_like(q)
        q2 = jnp.concatenate([jnp.where(low, q, zero), jnp.where(low, zero, q)], axis=0)
        return _dot_nt(q2, kw), qs - ws, d * ws + r, d * (qs - base // d) + r

    def consume(g, d, s, q_minus_w, w_row, out_row):
        vw = v_ref[0, pl.ds(w_row, tw, stride=d), :].astype(BF16)
        bias = jnp.where(q_minus_w == 0, bias_start, bias_inner)
        ps, ms, ls = [], [], []
        for hh in range(2):
            sm = s[hh * tq:(hh + 1) * tq] + bias
            m = jnp.max(sm, axis=-1, keepdims=True)
            p = jnp.exp2(sm - m)
            ps.append(p.astype(BF16))
            ms.append(jnp.broadcast_to(m, (tq, LANES)))
            ls.append(jnp.broadcast_to(jnp.sum(p, axis=-1, keepdims=True), (tq, LANES)))
        acc = _dot(jnp.concatenate(ps, axis=0), vw)
        accs = [acc[:tq], acc[tq:]]
        rows = pl.ds(out_row, tq, stride=d)
        acc_s[g, rows, :] = jnp.where(low, accs[0], accs[1])
        m_s[g, rows, :] = jnp.where(low, ms[0], ms[1])
        l_s[g, rows, :] = jnp.where(low, ls[0], ls[1])

    def superblock(sb, carry):
        base = pl.multiple_of(sb * sup, sup)
        for g, (_, d) in enumerate(DILATED_PATTERNS):
            nblk = sup // (d * tq)

            def body(it, c, g=g, d=d, nblk=nblk):
                first = it * DIL_GROUP
                pending = {n: issue(d, base, first + n, nblk) for n in range(DIL_AHEAD)}
                for n in range(DIL_GROUP):
                    if n + DIL_AHEAD < DIL_GROUP:
                        pending[n + DIL_AHEAD] = issue(d, base, first + n + DIL_AHEAD, nblk)
                    consume(g, d, *pending.pop(n))
                return c

            lax.fori_loop(0, d * nblk // DIL_GROUP, body, 0)

        rows = pl.ds(base, sup)
        m1, m2, m3 = m_s[0], m_s[1], m_s[2]
        mx = jnp.maximum(jnp.maximum(m1, m2), m3)
        e1, e2, e3 = jnp.exp2(m1 - mx), jnp.exp2(m2 - mx), jnp.exp2(m3 - mx)
        num = e1 * acc_s[0] + e2 * acc_s[1] + e3 * acc_s[2]
        den = e1 * l_s[0] + e2 * l_s[1] + e3 * l_s[2]
        o_ref[0, rows, :] = (num / den * g_ref[0, rows, :].astype(F32)).astype(BF16)
        return carry

    lax.fori_loop(0, seq // sup, superblock, 0)


def _dilated(qb, kb, vb, gb):
    b, s, _ = qb.shape
    spec = pl.BlockSpec((1, s, LANES), lambda bi, p: (bi, 0, p))
    scratch = pltpu.VMEM((len(DILATED_PATTERNS), DIL_SUPER, LANES), F32)
    return pl.pallas_call(
        functools.partial(_dil_kernel, seq=s),
        out_shape=jax.ShapeDtypeStruct((b, s, D_BRANCH), BF16),
        grid=(b, D_BRANCH // LANES),
        in_specs=[spec, spec, spec, spec],
        out_specs=spec,
        scratch_shapes=[scratch, scratch, scratch],
        compiler_params=pltpu.CompilerParams(
            dimension_semantics=("arbitrary",) * 2, vmem_limit_bytes=VMEM_LIMIT_BYTES),
        name="dilated_attn",
    )(qb, kb, vb, gb)


def _post_kernel(h_ref, oat_ref, ob_ref, p_ref, woa_ref, wob_ref, g2_ref, wg_ref, wple_ref,
                 out_ref):
    h1 = h_ref[...] + _dot_tn(oat_ref[0], woa_ref[...]) + _dot(ob_ref[...], wob_ref[...])
    u2 = _rms_norm_rows(h1, g2_ref[...]).astype(BF16)
    gate = jax.nn.sigmoid(_dot(u2, wg_ref[...]))
    ple = _dot(p_ref[...].astype(BF16), wple_ref[...])
    out_ref[...] = h1 + ple * gate


def _post(h2, oat, ob, p2, batch, seq, woa, wob, g2, wg, wple):
    nt = seq // ROW_TILE
    row = lambda w: pl.BlockSpec((ROW_TILE, w), lambda b, t: (b * nt + t, 0))
    full = lambda a: pl.BlockSpec(a.shape, lambda b, t: (0,) * a.ndim)
    return pl.pallas_call(
        _post_kernel,
        out_shape=jax.ShapeDtypeStruct((batch * seq, D_MODEL), F32),
        grid=(batch, nt),
        in_specs=[row(D_MODEL), pl.BlockSpec((1, D_BRANCH, ROW_TILE), lambda b, t: (b, 0, t)),
                  row(D_BRANCH), row(PLE_DIM),
                  full(woa), full(wob), full(g2), full(wg), full(wple)],
        out_specs=row(D_MODEL),
        compiler_params=pltpu.CompilerParams(
            dimension_semantics=("arbitrary", "arbitrary"), vmem_limit_bytes=VMEM_LIMIT_BYTES),
        name="out_stage",
    )(h2, oat, ob, p2, woa, wob, g2, wg, wple)


def _layer(h2, p2, pos_row, invf, tri, batch, seq, norm_g, w_in, b_f, qk_g, w_out, w_ple,
           ple_norm_g, w_ple_gate):
    d = D_BRANCH
    cuts = np.cumsum([0, d, d, d, d, N_HEADS, d, d, d, d])
    qa_w, ka_w, va_w, ga_w, f_w, qb_w, kb_w, vb_w, gb_w = (
        w_in[:, cuts[i]:cuts[i + 1]] for i in range(9))
    f_w = jnp.pad(f_w, ((0, 0), (0, F_ROWS - N_HEADS)))
    wt = jnp.concatenate([qa_w, ka_w, va_w, ga_w, qb_w, kb_w, f_w], axis=1).T.astype(BF16)
    wn = jnp.concatenate([vb_w, gb_w], axis=1).astype(BF16)
    bf_col = jnp.pad(b_f.astype(F32), (0, F_ROWS - N_HEADS)).reshape(F_ROWS, 1)
    scale = HEAD_DIM ** -0.5
    gcol = jnp.concatenate([jnp.tile(qk_g[0] * (scale * LOG2E), N_HEADS), jnp.tile(qk_g[1], N_HEADS),
                            jnp.tile(qk_g[2] * (scale * LOG2E), N_HEADS), jnp.tile(qk_g[3], N_HEADS)]
                           ).astype(F32).reshape(4 * d, 1)

    qt, k, vt, gt, qb, kb, vb, gb, stats = _proj(
        h2, pos_row, batch, seq, norm_g.reshape(1, D_MODEL), wt, wn, gcol, bf_col, invf, tri)

    oat = _fox(_fox_first_step(stats, batch, seq), qt, k, vt, gt).reshape(batch, d, seq)
    as3 = lambda a: a.reshape(batch, seq, d)
    ob = _dilated(as3(qb), as3(kb), as3(vb), as3(gb)).reshape(batch * seq, d)

    w_out_b = w_out.astype(BF16)
    return _post(h2, oat, ob, p2, batch, seq, w_out_b[:d], w_out_b[d:],
                 ple_norm_g.reshape(1, D_MODEL), w_ple_gate.astype(BF16), w_ple.astype(BF16))


def kernel(x, p, positions, norm_g, w_in, b_f, qk_norm_g, w_out, w_ple, ple_norm_g, w_ple_gate):
    batch, seq, _ = x.shape
    depth = p.shape[0]
    assert all(w // dil == N_BACK for w, dil in DILATED_PATTERNS)
    assert seq % DIL_SUPER == 0 and seq % ROW_TILE == 0 and FOX_T % FOX_UQ == 0
    pos_row = positions.reshape(1, batch * seq)
    inv_freq = ROPE_THETA ** (-jnp.arange(ROPE_HALF, dtype=F32) / ROPE_HALF)
    invf = inv_freq.reshape(ROPE_HALF, 1)
    idx = np.arange(ROW_TILE)
    tri = jnp.asarray(idx[:, None] <= idx[None, :], BF16)
    h2 = x.reshape(batch * seq, D_MODEL)
    for i in range(depth):
        h2 = _layer(h2, p[i].reshape(batch * seq, PLE_DIM), pos_row, invf, tri, batch, seq,
                    norm_g[i], w_in[i], b_f[i], qk_norm_g[i], w_out[i], w_ple[i],
                    ple_norm_g[i], w_ple_gate[i])
    return h2.reshape(batch, seq, D_MODEL)
```

```python
import functools

import numpy as np
import jax
import jax.numpy as jnp
from jax import lax
from jax.experimental import pallas as pl
from jax.experimental.pallas import tpu as pltpu

D_MODEL = 1024
HEAD_DIM = 64
N_HEADS = 8
D_BRANCH = N_HEADS * HEAD_DIM
PLE_DIM = 256
ROPE_THETA = 500000.0
ROPE_DIM = HEAD_DIM // 4
ROPE_HALF = ROPE_DIM // 2
DILATED_PATTERNS = ((128, 1), (512, 4), (2048, 16))
N_BACK = 128
EPS = 1e-6
NEG = -1e30

LANES = 128
SUBLANES = 8
BF16_ROWS = 16
VMEM_LIMIT_BYTES = 56 * 1024 * 1024

ROW_TILE = 512
FOX_T = 1024
FOX_UK = 256
FOX_UQ = 256
FOX_AHEAD = 8
FOX_VROWS = HEAD_DIM + BF16_ROWS
FOX_SKIP_BITS = 150.0
LOG2E = 1.4426950408889634
DIL_TQ = N_BACK
DIL_TW = DIL_TQ + N_BACK
DIL_SUPER = max(d for _, d in DILATED_PATTERNS) * DIL_TQ
DIL_SPLIT = 4
DIL_COPY = 512
DIL_GROUP = 8
DIL_AHEAD = 4
F_ROWS = BF16_ROWS

F32 = jnp.float32
BF16 = jnp.bfloat16


def _dot(a, b):
    return jnp.dot(a, b, preferred_element_type=F32)


def _dot_nt(a, b):
    return lax.dot_general(a, b, (((1,), (1,)), ((), ())), preferred_element_type=F32)


def _dot_tn(a, b):
    return lax.dot_general(a, b, (((0,), (0,)), ((), ())), preferred_element_type=F32)


def _split3(x):
    hi = x.astype(BF16).astype(F32)
    r = x - hi
    mid = r.astype(BF16).astype(F32)
    lo = (r - mid).astype(BF16).astype(F32)
    return hi, mid, lo


def _rms_norm_rows(h, g_row):
    ms = jnp.mean(h * h, axis=-1, keepdims=True)
    return h * lax.rsqrt(ms + EPS) * g_row


def _proj_kernel(h_ref, pos_ref, g1_ref, wt_ref, wn_ref, gcol_ref, bf_ref, invf_ref, tri_ref,
                 qt_ref, k_ref, vt_ref, gt_ref, qb_ref, kb_ref, vb_ref, gb_ref, stats_ref,
                 carry_ref):
    t = pl.program_id(1)
    tm = ROW_TILE
    d = D_BRANCH
    u = _rms_norm_rows(h_ref[...], g1_ref[...]).astype(BF16)

    zn = _dot(u, wn_ref[...])
    vb_ref[...] = zn[:, :d]
    gate_b = zn[:, d:]
    gb_ref[...] = (gate_b * jax.nn.sigmoid(gate_b)).astype(BF16)

    def feat_major(group):
        return _dot_nt(wt_ref[group * d:(group + 1) * d, :], u)

    def head_norm(z, group):
        z3 = z.reshape(N_HEADS, HEAD_DIM, tm)
        ss = jnp.sum(z3 * z3, axis=1, keepdims=True)
        y3 = z3 * lax.rsqrt(ss * (1.0 / HEAD_DIM) + EPS)
        return y3.reshape(d, tm) * gcol_ref[group * d:(group + 1) * d, :]

    ang = invf_ref[...] * pos_ref[...].astype(F32)
    cos = jnp.cos(ang)
    sin = jnp.sin(ang)

    def rope(y):
        y3 = y.reshape(N_HEADS, HEAD_DIM, tm)
        x1 = y3[:, :ROPE_HALF, :]
        x2 = y3[:, ROPE_HALF:ROPE_DIM, :]
        rot = [x1 * cos - x2 * sin, x2 * cos + x1 * sin, y3[:, ROPE_DIM:, :]]
        return jnp.concatenate(rot, axis=1).reshape(d, tm)

    f = _dot_nt(wt_ref[6 * d:6 * d + F_ROWS, :], u) + bf_ref[...]
    lf = jnp.minimum(f, 0.0) - jnp.log1p(jnp.exp(-jnp.abs(f)))
    stacked = jnp.concatenate([x.astype(BF16) for x in _split3(lf)], axis=0)
    cs = _dot(stacked, tri_ref[...])
    carry = jnp.where(t == 0, 0.0, carry_ref[:, 0:1])
    c = cs[:F_ROWS] + cs[F_ROWS:2 * F_ROWS] + cs[2 * F_ROWS:] + carry
    carry_ref[...] = jnp.broadcast_to(c[:, tm - 1:tm], (F_ROWS, LANES))
    c2 = c * LOG2E

    qa = head_norm(feat_major(0), 0)
    ka = head_norm(feat_major(1), 1)
    row = lax.broadcasted_iota(jnp.int32, (SUBLANES, tm), 0)
    pad = jnp.zeros((HEAD_DIM - SUBLANES, tm), F32)
    for hd in range(N_HEADS):
        c_hi, c_mid, c_lo = _split3(c2[hd:hd + 1, :])
        q_aug = jnp.where(row == 0, c_hi, jnp.where(row == 1, c_mid, jnp.where(
            row == 2, c_lo, jnp.where(row < 6, 1.0, 0.0))))
        k_aug = jnp.where(row < 3, 1.0, jnp.where(row == 3, -c_hi, jnp.where(
            row == 4, -c_mid, jnp.where(row == 5, -c_lo, 0.0))))
        rows = slice(hd * HEAD_DIM, (hd + 1) * HEAD_DIM)
        qt_ref[0, hd] = jnp.concatenate([qa[rows], q_aug, pad], axis=0).astype(BF16)
        k_ref[0, hd] = jnp.concatenate([ka[rows], k_aug, pad], axis=0).T.astype(BF16)

    ones_row = jnp.where(lax.broadcasted_iota(jnp.int32, (N_HEADS, BF16_ROWS, tm), 1) == 0,
                         1.0, 0.0)
    va = feat_major(2).reshape(N_HEADS, HEAD_DIM, tm)
    vt_ref[0, :, 0] = jnp.concatenate([va, ones_row], axis=1).astype(BF16)
    gate_a = feat_major(3)
    gt_ref[0] = (gate_a * jax.nn.sigmoid(gate_a)).reshape(N_HEADS, HEAD_DIM, tm).astype(BF16)

    qb_ref[...] = rope(head_norm(feat_major(4), 2)).T
    kb_ref[...] = rope(head_norm(feat_major(5), 3)).T

    def max_sq_norm(y):
        y3 = y.reshape(N_HEADS, HEAD_DIM, tm)
        return jnp.max(jnp.sum(y3 * y3, axis=1), axis=-1, keepdims=True)

    lane = lax.broadcasted_iota(jnp.int32, (N_HEADS, LANES), 1)
    stats_ref[0] = jnp.where(lane == 0, max_sq_norm(qa), jnp.where(
        lane == 1, max_sq_norm(ka), jnp.where(
            lane == 2, c2[:N_HEADS, 0:1], jnp.where(lane == 3, c2[:N_HEADS, tm - 1:tm], 0.0))))


def _proj(h2, pos_row, batch, seq, g1, wt, wn, gcol, bf_col, invf, tri):
    nt = seq // ROW_TILE
    rows = batch * seq
    d = D_BRANCH
    row = lambda w: pl.BlockSpec((ROW_TILE, w), lambda b, t: (b * nt + t, 0))
    full = lambda a: pl.BlockSpec(a.shape, lambda b, t: (0,) * a.ndim)
    out_shape = (
        jax.ShapeDtypeStruct((batch, N_HEADS, 2 * HEAD_DIM, seq), BF16),
        jax.ShapeDtypeStruct((batch, N_HEADS, seq, 2 * HEAD_DIM), BF16),
        jax.ShapeDtypeStruct((batch, N_HEADS, nt, FOX_VROWS, ROW_TILE), BF16),
        jax.ShapeDtypeStruct((batch, N_HEADS, HEAD_DIM, seq), BF16),
        jax.ShapeDtypeStruct((rows, d), F32),
        jax.ShapeDtypeStruct((rows, d), F32),
        jax.ShapeDtypeStruct((rows, d), F32),
        jax.ShapeDtypeStruct((rows, d), BF16),
        jax.ShapeDtypeStruct((batch * nt, N_HEADS, LANES), F32),
    )
    out_specs = (
        pl.BlockSpec((1, N_HEADS, 2 * HEAD_DIM, ROW_TILE), lambda b, t: (b, 0, 0, t)),
        pl.BlockSpec((1, N_HEADS, ROW_TILE, 2 * HEAD_DIM), lambda b, t: (b, 0, t, 0)),
        pl.BlockSpec((1, N_HEADS, 1, FOX_VROWS, ROW_TILE), lambda b, t: (b, 0, t, 0, 0)),
        pl.BlockSpec((1, N_HEADS, HEAD_DIM, ROW_TILE), lambda b, t: (b, 0, 0, t)),
        row(d), row(d), row(d), row(d),
        pl.BlockSpec((1, N_HEADS, LANES), lambda b, t: (b * nt + t, 0, 0)),
    )
    return pl.pallas_call(
        _proj_kernel,
        out_shape=out_shape,
        grid=(batch, nt),
        in_specs=[row(D_MODEL), pl.BlockSpec((1, ROW_TILE), lambda b, t: (0, b * nt + t)),
                  full(g1), full(wt), full(wn), full(gcol), full(bf_col), full(invf), full(tri)],
        out_specs=out_specs,
        scratch_shapes=[pltpu.VMEM((F_ROWS, LANES), F32)],
        compiler_params=pltpu.CompilerParams(
            dimension_semantics=("arbitrary", "arbitrary"), vmem_limit_bytes=VMEM_LIMIT_BYTES),
        name="in_proj",
    )(h2, pos_row, g1, wt, wn, gcol, bf_col, invf, tri)


def _fox_kernel(first_ref, q_ref, k_ref, v_ref, g_ref, o_ref):
    i = pl.program_id(2)
    first = first_ref[(pl.program_id(0) * N_HEADS + pl.program_id(1)) * pl.num_programs(2) + i]
    uk, uq = FOX_UK, FOX_UQ
    n_q = FOX_T // uq
    q = [q_ref[0, 0, :, a * uq:(a + 1) * uq] for a in range(n_q)]
    rel = (lax.broadcasted_iota(jnp.int32, (uk, uq), 0)
           - lax.broadcasted_iota(jnp.int32, (uk, uq), 1))

    def scores(j, ks, a):
        start = pl.multiple_of(j * FOX_T, FOX_T)
        return _dot(k_ref[0, 0, pl.ds(start + ks * uk, uk), :], q[a])

    def consume(s, v_sub, state, max_rel):
        m, acc = state
        if max_rel is not None:
            s = jnp.where(rel <= max_rel, s, NEG)
        m_new = jnp.maximum(m, jnp.max(s, axis=0, keepdims=True))
        alpha = jnp.exp2(m - m_new)
        p = jnp.exp2((s - m_new).astype(BF16))
        acc = alpha * acc + _dot(v_sub, p)
        return m_new, acc

    def step(j, carry, diagonal):
        s_first, states = carry
        states = list(states)
        units = []
        for ks in range(FOX_T // uk):
            for a in range(n_q):
                max_rel = None
                if diagonal:
                    if ks * uk >= (a + 1) * uq:
                        continue
                    if (ks + 1) * uk - 1 > a * uq:
                        max_rel = a * uq - ks * uk
                units.append((ks, a, max_rel))
        pending = {0: s_first}
        for n in range(1, min(FOX_AHEAD, len(units))):
            pending[n] = scores(j, units[n][0], units[n][1])
        s_next = s_first
        for n, (ks, a, max_rel) in enumerate(units):
            ahead = n + FOX_AHEAD
            if ahead < len(units):
                pending[ahead] = scores(j, units[ahead][0], units[ahead][1])
            elif ahead == len(units) and not diagonal:
                s_next = scores(j + 1, 0, 0)
            chunk, off = divmod(ks * uk, ROW_TILE)
            v_sub = v_ref[0, 0, j * (FOX_T // ROW_TILE) + chunk, :, off:off + uk]
            states[a] = consume(pending.pop(n), v_sub, states[a], max_rel)
        return s_next, tuple(states)

    init = tuple((jnp.full((1, uq), NEG, F32), jnp.zeros((FOX_VROWS, uq), F32))
                 for _ in range(n_q))
    carry = lax.fori_loop(first, i, functools.partial(step, diagonal=False),
                          (scores(first, 0, 0), init))
    _, states = step(i, carry, True)
    out = jnp.concatenate([acc[:HEAD_DIM] / acc[HEAD_DIM:HEAD_DIM + 1] for _, acc in states],
                          axis=1)
    o_ref[0, 0] = (out * g_ref[0, 0].astype(F32)).astype(BF16)


def _fox_first_step(stats, batch, seq):
    nt, per, nq = seq // ROW_TILE, FOX_T // ROW_TILE, seq // FOX_T
    st = stats.reshape(batch, nt, N_HEADS, LANES)
    q_sq = st[..., 0].reshape(batch, nq, per, N_HEADS).max(axis=2)
    k_sq = st[..., 1].max(axis=1, keepdims=True)
    c_first = st[:, ::per, :, 2]
    c_last = st[:, per - 1::per, :, 3]
    bound = jnp.sqrt(q_sq * k_sq) * 1.01 + 1.0
    gap = c_first[:, :, None, :] - c_last[:, None, :, :]
    skip = (2.0 * bound[:, :, None, :] + gap) < -FOX_SKIP_BITS
    steps = jnp.arange(nq, dtype=jnp.int32)
    first = jnp.min(jnp.where(skip, nq, steps[None, None, :, None]), axis=2)
    first = jnp.minimum(first, steps[None, :, None])
    return first.transpose(0, 2, 1).reshape(-1)


def _fox(first, qt, k, vt, gt):
    b, _, _, s = qt.shape
    q_spec = pl.BlockSpec((1, 1, 2 * HEAD_DIM, FOX_T), lambda bi, h, i, f: (bi, h, 0, i))
    k_spec = pl.BlockSpec((1, 1, s, 2 * HEAD_DIM), lambda bi, h, i, f: (bi, h, 0, 0))
    v_spec = pl.BlockSpec((1, 1, s // ROW_TILE, FOX_VROWS, ROW_TILE),
                          lambda bi, h, i, f: (bi, h, 0, 0, 0))
    o_spec = pl.BlockSpec((1, 1, HEAD_DIM, FOX_T), lambda bi, h, i, f: (bi, h, 0, i))
    return pl.pallas_call(
        _fox_kernel,
        out_shape=jax.ShapeDtypeStruct((b, N_HEADS, HEAD_DIM, s), BF16),
        grid_spec=pltpu.PrefetchScalarGridSpec(
            num_scalar_prefetch=1,
            grid=(b, N_HEADS, s // FOX_T),
            in_specs=[q_spec, k_spec, v_spec, o_spec],
            out_specs=o_spec),
        compiler_params=pltpu.CompilerParams(
            dimension_semantics=("arbitrary",) * 3, vmem_limit_bytes=VMEM_LIMIT_BYTES),
        name="fox_attention",
    )(first, qt, k, vt, gt)


def _dil_kernel(q_ref, k_ref, v_ref, g_ref, o_ref, q4_s, k4_s, v4_s, acc_s, m_s, l_s, *, seq):
    tq, tw, sup = DIL_TQ, DIL_TW, DIL_SUPER
    part = seq // DIL_SPLIT

    def regroup(c, carry):
        for src, dst in ((q_ref, q4_s), (k_ref, k4_s), (v_ref, v4_s)):
            for r in range(DIL_SPLIT):
                rows = src[0, pl.ds(c * DIL_SPLIT * DIL_COPY + r, DIL_COPY, stride=DIL_SPLIT), :]
                dst[pl.ds(r * part + c * DIL_COPY, DIL_COPY), :] = rows
        return carry

    lax.fori_loop(0, part // DIL_COPY, regroup, 0)

    def stream_rows(nat_ref, split_ref, d, r, pos, n):
        if d % DIL_SPLIT:
            x = nat_ref[0, pl.ds(d * pos + r, n, stride=d), :]
        else:
            sd = d // DIL_SPLIT
            start = (r % DIL_SPLIT) * part + sd * pos + r // DIL_SPLIT
            x = split_ref[pl.ds(start, n, stride=sd), :]
        return x.astype(BF16)

    lane = lax.broadcasted_iota(jnp.int32, (1, LANES), 1)
    low = lane < HEAD_DIM
    rel = (lax.broadcasted_iota(jnp.int32, (tq, tw), 0)
           - lax.broadcasted_iota(jnp.int32, (tq, tw), 1))

    def band_bias(q_minus_w):
        dist = rel + q_minus_w
        return jnp.where(dist >= 0, jnp.where(dist <= N_BACK, 0.0, NEG), NEG)

    bias_inner = band_bias(N_BACK)
    bias_start = band_bias(0)

    def issue(d, base, idx, nblk):
        r = idx // nblk
        jb = idx % nblk
        qs = base // d + jb * tq
        ws = jnp.maximum(qs - N_BACK, 0)
        q = stream_rows(q_ref, q4_s, d, r, qs, tq)
        kw = stream_rows(k_ref, k4_s, d, r, ws, tw)
        zero = jnp.zeros_like(q)
        q2 = jnp.concatenate([jnp.where(low, q, zero), jnp.where(low, zero, q)], axis=0)
        return _dot_nt(q2, kw), r, qs, ws, d * (qs - base // d) + r

    def consume(g, d, s, r, qs, ws, out_row):
        vw = stream_rows(v_ref, v4_s, d, r, ws, tw)
        bias = jnp.where(qs == ws, bias_start, bias_inner)
        ps, ms, ls = [], [], []
        for hh in range(2):
            sm = s[hh * tq:(hh + 1) * tq] + bias
            m = jnp.max(sm, axis=-1, keepdims=True)
            p = jnp.exp2(sm - m)
            ps.append(p.astype(BF16))
            ms.append(jnp.broadcast_to(m, (tq, LANES)))
            ls.append(jnp.broadcast_to(jnp.sum(p, axis=-1, keepdims=True), (tq, LANES)))
        acc = _dot(jnp.concatenate(ps, axis=0), vw)
        accs = [acc[:tq], acc[tq:]]
        rows = pl.ds(out_row, tq, stride=d)
        acc_s[g, rows, :] = jnp.where(low, accs[0], accs[1])
        m_s[g, rows, :] = jnp.where(low, ms[0], ms[1])
        l_s[g, rows, :] = jnp.where(low, ls[0], ls[1])

    def superblock(sb, carry):
        base = pl.multiple_of(sb * sup, sup)
        for g, (_, d) in enumerate(DILATED_PATTERNS):
            nblk = sup // (d * tq)

            def body(it, c, g=g, d=d, nblk=nblk):
                first = it * DIL_GROUP
                pending = {n: issue(d, base, first + n, nblk) for n in range(DIL_AHEAD)}
                for n in range(DIL_GROUP):
                    if n + DIL_AHEAD < DIL_GROUP:
                        pending[n + DIL_AHEAD] = issue(d, base, first + n + DIL_AHEAD, nblk)
                    consume(g, d, *pending.pop(n))
                return c

            lax.fori_loop(0, d * nblk // DIL_GROUP, body, 0)

        rows = pl.ds(base, sup)
        m1, m2, m3 = m_s[0], m_s[1], m_s[2]
        mx = jnp.maximum(jnp.maximum(m1, m2), m3)
        e1, e2, e3 = jnp.exp2(m1 - mx), jnp.exp2(m2 - mx), jnp.exp2(m3 - mx)
        num = e1 * acc_s[0] + e2 * acc_s[1] + e3 * acc_s[2]
        den = e1 * l_s[0] + e2 * l_s[1] + e3 * l_s[2]
        o_ref[0, rows, :] = (num / den * g_ref[0, rows, :].astype(F32)).astype(BF16)
        return carry

    lax.fori_loop(0, seq // sup, superblock, 0)


def _dilated(qb, kb, vb, gb):
    b, s, _ = qb.shape
    spec = pl.BlockSpec((1, s, LANES), lambda bi, p: (bi, 0, p))
    once = pl.BlockSpec((1, s, LANES), lambda bi, p: (bi, 0, p), pipeline_mode=pl.Buffered(1))
    regrouped = pltpu.VMEM((s, LANES), F32)
    scratch = pltpu.VMEM((len(DILATED_PATTERNS), DIL_SUPER, LANES), F32)
    return pl.pallas_call(
        functools.partial(_dil_kernel, seq=s),
        out_shape=jax.ShapeDtypeStruct((b, s, D_BRANCH), BF16),
        grid=(b, D_BRANCH // LANES),
        in_specs=[once, once, once, spec],
        out_specs=spec,
        scratch_shapes=[regrouped, regrouped, regrouped, scratch, scratch, scratch],
        compiler_params=pltpu.CompilerParams(
            dimension_semantics=("arbitrary",) * 2, vmem_limit_bytes=VMEM_LIMIT_BYTES),
        name="dilated_attn",
    )(qb, kb, vb, gb)


def _post_kernel(h_ref, oat_ref, ob_ref, p_ref, woa_ref, wob_ref, g2_ref, wg_ref, wple_ref,
                 out_ref):
    h1 = h_ref[...] + _dot_tn(oat_ref[0], woa_ref[...]) + _dot(ob_ref[...], wob_ref[...])
    u2 = _rms_norm_rows(h1, g2_ref[...]).astype(BF16)
    gate = jax.nn.sigmoid(_dot(u2, wg_ref[...]))
    ple = _dot(p_ref[...].astype(BF16), wple_ref[...])
    out_ref[...] = h1 + ple * gate


def _post(h2, oat, ob, p2, batch, seq, woa, wob, g2, wg, wple):
    nt = seq // ROW_TILE
    row = lambda w: pl.BlockSpec((ROW_TILE, w), lambda b, t: (b * nt + t, 0))
    full = lambda a: pl.BlockSpec(a.shape, lambda b, t: (0,) * a.ndim)
    return pl.pallas_call(
        _post_kernel,
        out_shape=jax.ShapeDtypeStruct((batch * seq, D_MODEL), F32),
        grid=(batch, nt),
        in_specs=[row(D_MODEL), pl.BlockSpec((1, D_BRANCH, ROW_TILE), lambda b, t: (b, 0, t)),
                  row(D_BRANCH), row(PLE_DIM),
                  full(woa), full(wob), full(g2), full(wg), full(wple)],
        out_specs=row(D_MODEL),
        compiler_params=pltpu.CompilerParams(
            dimension_semantics=("arbitrary", "arbitrary"), vmem_limit_bytes=VMEM_LIMIT_BYTES),
        name="out_stage",
    )(h2, oat, ob, p2, woa, wob, g2, wg, wple)


def _layer(h2, p2, pos_row, invf, tri, batch, seq, norm_g, w_in, b_f, qk_g, w_out, w_ple,
           ple_norm_g, w_ple_gate):
    d = D_BRANCH
    cuts = np.cumsum([0, d, d, d, d, N_HEADS, d, d, d, d])
    qa_w, ka_w, va_w, ga_w, f_w, qb_w, kb_w, vb_w, gb_w = (
        w_in[:, cuts[i]:cuts[i + 1]] for i in range(9))
    f_w = jnp.pad(f_w, ((0, 0), (0, F_ROWS - N_HEADS)))
    wt = jnp.concatenate([qa_w, ka_w, va_w, ga_w, qb_w, kb_w, f_w], axis=1).T.astype(BF16)
    wn = jnp.concatenate([vb_w, gb_w], axis=1).astype(BF16)
    bf_col = jnp.pad(b_f.astype(F32), (0, F_ROWS - N_HEADS)).reshape(F_ROWS, 1)
    scale = HEAD_DIM ** -0.5
    gcol = jnp.concatenate([jnp.tile(qk_g[0] * (scale * LOG2E), N_HEADS), jnp.tile(qk_g[1], N_HEADS),
                            jnp.tile(qk_g[2] * (scale * LOG2E), N_HEADS), jnp.tile(qk_g[3], N_HEADS)]
                           ).astype(F32).reshape(4 * d, 1)

    qt, k, vt, gt, qb, kb, vb, gb, stats = _proj(
        h2, pos_row, batch, seq, norm_g.reshape(1, D_MODEL), wt, wn, gcol, bf_col, invf, tri)

    oat = _fox(_fox_first_step(stats, batch, seq), qt, k, vt, gt).reshape(batch, d, seq)
    as3 = lambda a: a.reshape(batch, seq, d)
    ob = _dilated(as3(qb), as3(kb), as3(vb), as3(gb)).reshape(batch * seq, d)

    w_out_b = w_out.astype(BF16)
    return _post(h2, oat, ob, p2, batch, seq, w_out_b[:d], w_out_b[d:],
                 ple_norm_g.reshape(1, D_MODEL), w_ple_gate.astype(BF16), w_ple.astype(BF16))


def kernel(x, p, positions, norm_g, w_in, b_f, qk_norm_g, w_out, w_ple, ple_norm_g, w_ple_gate):
    batch, seq, _ = x.shape
    depth = p.shape[0]
    assert all(w // dil == N_BACK for w, dil in DILATED_PATTERNS)
    assert seq % DIL_SUPER == 0 and seq % ROW_TILE == 0 and FOX_T % FOX_UQ == 0
    pos_row = positions.reshape(1, batch * seq)
    inv_freq = ROPE_THETA ** (-jnp.arange(ROPE_HALF, dtype=F32) / ROPE_HALF)
    invf = inv_freq.reshape(ROPE_HALF, 1)
    idx = np.arange(ROW_TILE)
    tri = jnp.asarray(idx[:, None] <= idx[None, :], BF16)
    h2 = x.reshape(batch * seq, D_MODEL)
    for i in range(depth):
        h2 = _layer(h2, p[i].reshape(batch * seq, PLE_DIM), pos_row, invf, tri, batch, seq,
                    norm_g[i], w_in[i], b_f[i], qk_norm_g[i], w_out[i], w_ple[i],
                    ple_norm_g[i], w_ple_gate[i])
    return h2.reshape(batch, seq, D_MODEL)
```

```python
import functools

import numpy as np
import jax
import jax.numpy as jnp
from jax import lax
from jax.experimental import pallas as pl
from jax.experimental.pallas import tpu as pltpu

D_MODEL = 1024
HEAD_DIM = 64
N_HEADS = 8
D_BRANCH = N_HEADS * HEAD_DIM
PLE_DIM = 256
ROPE_THETA = 500000.0
ROPE_DIM = HEAD_DIM // 4
ROPE_HALF = ROPE_DIM // 2
DILATED_PATTERNS = ((128, 1), (512, 4), (2048, 16))
N_BACK = 128
EPS = 1e-6
NEG = -1e30

LANES = 128
SUBLANES = 8
BF16_ROWS = 16
VMEM_LIMIT_BYTES = 56 * 1024 * 1024

ROW_TILE = 512
FOX_T = 1024
FOX_UK = 256
FOX_UQ = 256
FOX_AHEAD = 8
FOX_VROWS = HEAD_DIM + BF16_ROWS
FOX_SKIP_BITS = 150.0
LOG2E = 1.4426950408889634
DIL_TQ = N_BACK
DIL_TW = DIL_TQ + N_BACK
DIL_SUPER = max(d for _, d in DILATED_PATTERNS) * DIL_TQ
DIL_SPLIT = 4
DIL_COPY = 512
DIL_GROUP = 8
DIL_AHEAD = 4
F_ROWS = BF16_ROWS

F32 = jnp.float32
BF16 = jnp.bfloat16


def _dot(a, b):
    return jnp.dot(a, b, preferred_element_type=F32)


def _dot_nt(a, b):
    return lax.dot_general(a, b, (((1,), (1,)), ((), ())), preferred_element_type=F32)


def _dot_tn(a, b):
    return lax.dot_general(a, b, (((0,), (0,)), ((), ())), preferred_element_type=F32)


def _split3(x):
    hi = x.astype(BF16).astype(F32)
    r = x - hi
    mid = r.astype(BF16).astype(F32)
    lo = (r - mid).astype(BF16).astype(F32)
    return hi, mid, lo


def _rms_norm_rows(h, g_row):
    ms = jnp.mean(h * h, axis=-1, keepdims=True)
    return h * lax.rsqrt(ms + EPS) * g_row


def _proj_kernel(h_ref, pos_ref, g1_ref, wt_ref, wn_ref, gcol_ref, bf_ref, invf_ref, tri_ref,
                 qt_ref, k_ref, vt_ref, gt_ref, qb_ref, kb_ref, vb_ref, gb_ref, stats_ref,
                 carry_ref):
    t = pl.program_id(1)
    tm = ROW_TILE
    d = D_BRANCH
    u = _rms_norm_rows(h_ref[...], g1_ref[...]).astype(BF16)

    def feat_major(group):
        return _dot_nt(wt_ref[group * d:(group + 1) * d, :], u)

    zn = _dot(u, wn_ref[...])
    f = _dot_nt(wt_ref[6 * d:6 * d + F_ROWS, :], u) + bf_ref[...]
    z_qa = feat_major(0)

    vb_ref[...] = zn[:, :d]
    gate_b = zn[:, d:]
    gb_ref[...] = (gate_b * jax.nn.sigmoid(gate_b)).astype(BF16)
    z_ka = feat_major(1)

    def head_norm(z, group):
        z3 = z.reshape(N_HEADS, HEAD_DIM, tm)
        ss = jnp.sum(z3 * z3, axis=1, keepdims=True)
        y3 = z3 * lax.rsqrt(ss * (1.0 / HEAD_DIM) + EPS)
        return y3.reshape(d, tm) * gcol_ref[group * d:(group + 1) * d, :]

    ang = invf_ref[...] * pos_ref[...].astype(F32)
    cos = jnp.cos(ang)
    sin = jnp.sin(ang)

    def rope(y):
        y3 = y.reshape(N_HEADS, HEAD_DIM, tm)
        x1 = y3[:, :ROPE_HALF, :]
        x2 = y3[:, ROPE_HALF:ROPE_DIM, :]
        rot = [x1 * cos - x2 * sin, x2 * cos + x1 * sin, y3[:, ROPE_DIM:, :]]
        return jnp.concatenate(rot, axis=1).reshape(d, tm)

    lf = jnp.minimum(f, 0.0) - jnp.log1p(jnp.exp(-jnp.abs(f)))
    stacked = jnp.concatenate([x.astype(BF16) for x in _split3(lf)], axis=0)
    cs = _dot(stacked, tri_ref[...])
    carry = jnp.where(t == 0, 0.0, carry_ref[:, 0:1])
    c = cs[:F_ROWS] + cs[F_ROWS:2 * F_ROWS] + cs[2 * F_ROWS:] + carry
    carry_ref[...] = jnp.broadcast_to(c[:, tm - 1:tm], (F_ROWS, LANES))
    c2 = c * LOG2E

    qa = head_norm(z_qa, 0)
    z_va = feat_major(2)
    ka = head_norm(z_ka, 1)
    z_ga = feat_major(3)
    row = lax.broadcasted_iota(jnp.int32, (SUBLANES, tm), 0)
    pad = jnp.zeros((HEAD_DIM - SUBLANES, tm), F32)
    for hd in range(N_HEADS):
        c_hi, c_mid, c_lo = _split3(c2[hd:hd + 1, :])
        q_aug = jnp.where(row == 0, c_hi, jnp.where(row == 1, c_mid, jnp.where(
            row == 2, c_lo, jnp.where(row < 6, 1.0, 0.0))))
        k_aug = jnp.where(row < 3, 1.0, jnp.where(row == 3, -c_hi, jnp.where(
            row == 4, -c_mid, jnp.where(row == 5, -c_lo, 0.0))))
        rows = slice(hd * HEAD_DIM, (hd + 1) * HEAD_DIM)
        qt_ref[0, hd] = jnp.concatenate([qa[rows], q_aug, pad], axis=0).astype(BF16)
        k_ref[0, hd] = jnp.concatenate([ka[rows], k_aug, pad], axis=0).T.astype(BF16)

    ones_row = jnp.where(lax.broadcasted_iota(jnp.int32, (N_HEADS, BF16_ROWS, tm), 1) == 0,
                         1.0, 0.0)
    z_qb = feat_major(4)
    va = z_va.reshape(N_HEADS, HEAD_DIM, tm)
    vt_ref[0, :, 0] = jnp.concatenate([va, ones_row], axis=1).astype(BF16)
    z_kb = feat_major(5)
    gt_ref[0] = (z_ga * jax.nn.sigmoid(z_ga)).reshape(N_HEADS, HEAD_DIM, tm).astype(BF16)

    qb_ref[...] = rope(head_norm(z_qb, 2)).T
    kb_ref[...] = rope(head_norm(z_kb, 3)).T

    def max_sq_norm(y):
        y3 = y.reshape(N_HEADS, HEAD_DIM, tm)
        return jnp.max(jnp.sum(y3 * y3, axis=1), axis=-1, keepdims=True)

    lane = lax.broadcasted_iota(jnp.int32, (N_HEADS, LANES), 1)
    stats_ref[0] = jnp.where(lane == 0, max_sq_norm(qa), jnp.where(
        lane == 1, max_sq_norm(ka), jnp.where(
            lane == 2, c2[:N_HEADS, 0:1], jnp.where(lane == 3, c2[:N_HEADS, tm - 1:tm], 0.0))))


def _proj(h2, pos_row, batch, seq, g1, wt, wn, gcol, bf_col, invf, tri):
    nt = seq // ROW_TILE
    rows = batch * seq
    d = D_BRANCH
    row = lambda w: pl.BlockSpec((ROW_TILE, w), lambda b, t: (b * nt + t, 0))
    full = lambda a: pl.BlockSpec(a.shape, lambda b, t: (0,) * a.ndim)
    out_shape = (
        jax.ShapeDtypeStruct((batch, N_HEADS, 2 * HEAD_DIM, seq), BF16),
        jax.ShapeDtypeStruct((batch, N_HEADS, seq, 2 * HEAD_DIM), BF16),
        jax.ShapeDtypeStruct((batch, N_HEADS, nt, FOX_VROWS, ROW_TILE), BF16),
        jax.ShapeDtypeStruct((batch, N_HEADS, HEAD_DIM, seq), BF16),
        jax.ShapeDtypeStruct((rows, d), F32),
        jax.ShapeDtypeStruct((rows, d), F32),
        jax.ShapeDtypeStruct((rows, d), F32),
        jax.ShapeDtypeStruct((rows, d), BF16),
        jax.ShapeDtypeStruct((batch * nt, N_HEADS, LANES), F32),
    )
    out_specs = (
        pl.BlockSpec((1, N_HEADS, 2 * HEAD_DIM, ROW_TILE), lambda b, t: (b, 0, 0, t)),
        pl.BlockSpec((1, N_HEADS, ROW_TILE, 2 * HEAD_DIM), lambda b, t: (b, 0, t, 0)),
        pl.BlockSpec((1, N_HEADS, 1, FOX_VROWS, ROW_TILE), lambda b, t: (b, 0, t, 0, 0)),
        pl.BlockSpec((1, N_HEADS, HEAD_DIM, ROW_TILE), lambda b, t: (b, 0, 0, t)),
        row(d), row(d), row(d), row(d),
        pl.BlockSpec((1, N_HEADS, LANES), lambda b, t: (b * nt + t, 0, 0)),
    )
    return pl.pallas_call(
        _proj_kernel,
        out_shape=out_shape,
        grid=(batch, nt),
        in_specs=[row(D_MODEL), pl.BlockSpec((1, ROW_TILE), lambda b, t: (0, b * nt + t)),
                  full(g1), full(wt), full(wn), full(gcol), full(bf_col), full(invf), full(tri)],
        out_specs=out_specs,
        scratch_shapes=[pltpu.VMEM((F_ROWS, LANES), F32)],
        compiler_params=pltpu.CompilerParams(
            dimension_semantics=("arbitrary", "arbitrary"), vmem_limit_bytes=VMEM_LIMIT_BYTES),
        name="in_proj",
    )(h2, pos_row, g1, wt, wn, gcol, bf_col, invf, tri)


def _fox_kernel(first_ref, q_ref, k_ref, v_ref, g_ref, o_ref):
    i = pl.program_id(2)
    first = first_ref[(pl.program_id(0) * N_HEADS + pl.program_id(1)) * pl.num_programs(2) + i]
    uk, uq = FOX_UK, FOX_UQ
    n_q = FOX_T // uq
    q = [q_ref[0, 0, :, a * uq:(a + 1) * uq] for a in range(n_q)]
    rel = (lax.broadcasted_iota(jnp.int32, (uk, uq), 0)
           - lax.broadcasted_iota(jnp.int32, (uk, uq), 1))

    def scores(j, ks, a):
        start = pl.multiple_of(j * FOX_T, FOX_T)
        return _dot(k_ref[0, 0, pl.ds(start + ks * uk, uk), :], q[a])

    def consume(s, v_sub, state, max_rel):
        m, acc = state
        if max_rel is not None:
            s = jnp.where(rel <= max_rel, s, NEG)
        m_new = jnp.maximum(m, jnp.max(s, axis=0, keepdims=True))
        alpha = jnp.exp2(m - m_new)
        p = jnp.exp2((s - m_new).astype(BF16))
        acc = alpha * acc + _dot(v_sub, p)
        return m_new, acc

    def step(j, carry, diagonal):
        s_first, states = carry
        states = list(states)
        units = []
        for ks in range(FOX_T // uk):
            for a in range(n_q):
                max_rel = None
                if diagonal:
                    if ks * uk >= (a + 1) * uq:
                        continue
                    if (ks + 1) * uk - 1 > a * uq:
                        max_rel = a * uq - ks * uk
                units.append((ks, a, max_rel))
        pending = {0: s_first}
        for n in range(1, min(FOX_AHEAD, len(units))):
            pending[n] = scores(j, units[n][0], units[n][1])
        s_next = s_first
        for n, (ks, a, max_rel) in enumerate(units):
            ahead = n + FOX_AHEAD
            if ahead < len(units):
                pending[ahead] = scores(j, units[ahead][0], units[ahead][1])
            elif ahead == len(units) and not diagonal:
                s_next = scores(j + 1, 0, 0)
            chunk, off = divmod(ks * uk, ROW_TILE)
            v_sub = v_ref[0, 0, j * (FOX_T // ROW_TILE) + chunk, :, off:off + uk]
            states[a] = consume(pending.pop(n), v_sub, states[a], max_rel)
        return s_next, tuple(states)

    init = tuple((jnp.full((1, uq), NEG, F32), jnp.zeros((FOX_VROWS, uq), F32))
                 for _ in range(n_q))
    carry = lax.fori_loop(first, i, functools.partial(step, diagonal=False),
                          (scores(first, 0, 0), init))
    _, states = step(i, carry, True)
    out = jnp.concatenate([acc[:HEAD_DIM] / acc[HEAD_DIM:HEAD_DIM + 1] for _, acc in states],
                          axis=1)
    o_ref[0, 0] = (out * g_ref[0, 0].astype(F32)).astype(BF16)


def _fox_first_step(stats, batch, seq):
    nt, per, nq = seq // ROW_TILE, FOX_T // ROW_TILE, seq // FOX_T
    st = stats.reshape(batch, nt, N_HEADS, LANES)
    q_sq = st[..., 0].reshape(batch, nq, per, N_HEADS).max(axis=2)
    k_sq = st[..., 1].max(axis=1, keepdims=True)
    c_first = st[:, ::per, :, 2]
    c_last = st[:, per - 1::per, :, 3]
    bound = jnp.sqrt(q_sq * k_sq) * 1.01 + 1.0
    gap = c_first[:, :, None, :] - c_last[:, None, :, :]
    skip = (2.0 * bound[:, :, None, :] + gap) < -FOX_SKIP_BITS
    steps = jnp.arange(nq, dtype=jnp.int32)
    first = jnp.min(jnp.where(skip, nq, steps[None, None, :, None]), axis=2)
    first = jnp.minimum(first, steps[None, :, None])
    return first.transpose(0, 2, 1).reshape(-1)


def _fox(first, qt, k, vt, gt):
    b, _, _, s = qt.shape
    q_spec = pl.BlockSpec((1, 1, 2 * HEAD_DIM, FOX_T), lambda bi, h, i, f: (bi, h, 0, i))
    k_spec = pl.BlockSpec((1, 1, s, 2 * HEAD_DIM), lambda bi, h, i, f: (bi, h, 0, 0))
    v_spec = pl.BlockSpec((1, 1, s // ROW_TILE, FOX_VROWS, ROW_TILE),
                          lambda bi, h, i, f: (bi, h, 0, 0, 0))
    o_spec = pl.BlockSpec((1, 1, HEAD_DIM, FOX_T), lambda bi, h, i, f: (bi, h, 0, i))
    return pl.pallas_call(
        _fox_kernel,
        out_shape=jax.ShapeDtypeStruct((b, N_HEADS, HEAD_DIM, s), BF16),
        grid_spec=pltpu.PrefetchScalarGridSpec(
            num_scalar_prefetch=1,
            grid=(b, N_HEADS, s // FOX_T),
            in_specs=[q_spec, k_spec, v_spec, o_spec],
            out_specs=o_spec),
        compiler_params=pltpu.CompilerParams(
            dimension_semantics=("arbitrary",) * 3, vmem_limit_bytes=VMEM_LIMIT_BYTES),
        name="fox_attention",
    )(first, qt, k, vt, gt)


def _dil_kernel(q_ref, k_ref, v_ref, g_ref, o_ref, q4_s, k4_s, v4_s, acc_s, m_s, l_s, *, seq):
    tq, tw, sup = DIL_TQ, DIL_TW, DIL_SUPER
    part = seq // DIL_SPLIT

    def regroup(c, carry):
        for src, dst in ((q_ref, q4_s), (k_ref, k4_s), (v_ref, v4_s)):
            for r in range(DIL_SPLIT):
                rows = src[0, pl.ds(c * DIL_SPLIT * DIL_COPY + r, DIL_COPY, stride=DIL_SPLIT), :]
                dst[pl.ds(r * part + c * DIL_COPY, DIL_COPY), :] = rows
        return carry

    lax.fori_loop(0, part // DIL_COPY, regroup, 0)

    def stream_rows(nat_ref, split_ref, d, r, pos, n):
        if d % DIL_SPLIT:
            x = nat_ref[0, pl.ds(d * pos + r, n, stride=d), :]
        else:
            sd = d // DIL_SPLIT
            start = (r % DIL_SPLIT) * part + sd * pos + r // DIL_SPLIT
            x = split_ref[pl.ds(start, n, stride=sd), :]
        return x.astype(BF16)

    lane = lax.broadcasted_iota(jnp.int32, (1, LANES), 1)
    low = lane < HEAD_DIM
    rel = (lax.broadcasted_iota(jnp.int32, (tq, tw), 0)
           - lax.broadcasted_iota(jnp.int32, (tq, tw), 1))

    def band_bias(q_minus_w):
        dist = rel + q_minus_w
        return jnp.where(dist >= 0, jnp.where(dist <= N_BACK, 0.0, NEG), NEG)

    bias_inner = band_bias(N_BACK)
    bias_start = band_bias(0)

    def issue(d, base, idx, nblk):
        r = idx // nblk
        jb = idx % nblk
        qs = base // d + jb * tq
        ws = jnp.maximum(qs - N_BACK, 0)
        q = stream_rows(q_ref, q4_s, d, r, qs, tq)
        kw = stream_rows(k_ref, k4_s, d, r, ws, tw)
        zero = jnp.zeros_like(q)
        q2 = jnp.concatenate([jnp.where(low, q, zero), jnp.where(low, zero, q)], axis=0)
        return _dot_nt(q2, kw), r, qs, ws, d * (qs - base // d) + r

    def consume(g, d, s, r, qs, ws, out_row):
        vw = stream_rows(v_ref, v4_s, d, r, ws, tw)
        bias = jnp.where(qs == ws, bias_start, bias_inner)
        ps, ms, ls = [], [], []
        for hh in range(2):
            sm = s[hh * tq:(hh + 1) * tq] + bias
            m = jnp.max(sm, axis=-1, keepdims=True)
            p = jnp.exp2(sm - m)
            ps.append(p.astype(BF16))
            ms.append(jnp.broadcast_to(m, (tq, LANES)))
            ls.append(jnp.broadcast_to(jnp.sum(p, axis=-1, keepdims=True), (tq, LANES)))
        acc = _dot(jnp.concatenate(ps, axis=0), vw)
        accs = [acc[:tq], acc[tq:]]
        rows = pl.ds(out_row, tq, stride=d)
        acc_s[g, rows, :] = jnp.where(low, accs[0], accs[1])
        m_s[g, rows, :] = jnp.where(low, ms[0], ms[1])
        l_s[g, rows, :] = jnp.where(low, ls[0], ls[1])

    def superblock(sb, carry):
        base = pl.multiple_of(sb * sup, sup)
        for g, (_, d) in enumerate(DILATED_PATTERNS):
            nblk = sup // (d * tq)

            def body(it, c, g=g, d=d, nblk=nblk):
                first = it * DIL_GROUP
                pending = {n: issue(d, base, first + n, nblk) for n in range(DIL_AHEAD)}
                for n in range(DIL_GROUP):
                    if n + DIL_AHEAD < DIL_GROUP:
                        pending[n + DIL_AHEAD] = issue(d, base, first + n + DIL_AHEAD, nblk)
                    consume(g, d, *pending.pop(n))
                return c

            lax.fori_loop(0, d * nblk // DIL_GROUP, body, 0)

        rows = pl.ds(base, sup)
        m1, m2, m3 = m_s[0], m_s[1], m_s[2]
        mx = jnp.maximum(jnp.maximum(m1, m2), m3)
        e1, e2, e3 = jnp.exp2(m1 - mx), jnp.exp2(m2 - mx), jnp.exp2(m3 - mx)
        num = e1 * acc_s[0] + e2 * acc_s[1] + e3 * acc_s[2]
        den = e1 * l_s[0] + e2 * l_s[1] + e3 * l_s[2]
        o_ref[0, rows, :] = (num / den * g_ref[0, rows, :].astype(F32)).astype(BF16)
        return carry

    lax.fori_loop(0, seq // sup, superblock, 0)


def _dilated(qb, kb, vb, gb):
    b, s, _ = qb.shape
    spec = pl.BlockSpec((1, s, LANES), lambda bi, p: (bi, 0, p))
    regrouped = pltpu.VMEM((s, LANES), F32)
    scratch = pltpu.VMEM((len(DILATED_PATTERNS), DIL_SUPER, LANES), F32)
    return pl.pallas_call(
        functools.partial(_dil_kernel, seq=s),
        out_shape=jax.ShapeDtypeStruct((b, s, D_BRANCH), BF16),
        grid=(b, D_BRANCH // LANES),
        in_specs=[spec, spec, spec, spec],
        out_specs=spec,
        scratch_shapes=[regrouped, regrouped, regrouped, scratch, scratch, scratch],
        compiler_params=pltpu.CompilerParams(
            dimension_semantics=("arbitrary",) * 2, vmem_limit_bytes=VMEM_LIMIT_BYTES),
        name="dilated_attn",
    )(qb, kb, vb, gb)


def _post_kernel(h_ref, oat_ref, ob_ref, p_ref, woa_ref, wob_ref, g2_ref, wg_ref, wple_ref,
                 out_ref):
    h1 = h_ref[...] + _dot_tn(oat_ref[0], woa_ref[...]) + _dot(ob_ref[...], wob_ref[...])
    u2 = _rms_norm_rows(h1, g2_ref[...]).astype(BF16)
    gate = jax.nn.sigmoid(_dot(u2, wg_ref[...]))
    ple = _dot(p_ref[...].astype(BF16), wple_ref[...])
    out_ref[...] = h1 + ple * gate


def _post(h2, oat, ob, p2, batch, seq, woa, wob, g2, wg, wple):
    nt = seq // ROW_TILE
    row = lambda w: pl.BlockSpec((ROW_TILE, w), lambda b, t: (b * nt + t, 0))
    full = lambda a: pl.BlockSpec(a.shape, lambda b, t: (0,) * a.ndim)
    return pl.pallas_call(
        _post_kernel,
        out_shape=jax.ShapeDtypeStruct((batch * seq, D_MODEL), F32),
        grid=(batch, nt),
        in_specs=[row(D_MODEL), pl.BlockSpec((1, D_BRANCH, ROW_TILE), lambda b, t: (b, 0, t)),
                  row(D_BRANCH), row(PLE_DIM),
                  full(woa), full(wob), full(g2), full(wg), full(wple)],
        out_specs=row(D_MODEL),
        compiler_params=pltpu.CompilerParams(
            dimension_semantics=("arbitrary", "arbitrary"), vmem_limit_bytes=VMEM_LIMIT_BYTES),
        name="out_stage",
    )(h2, oat, ob, p2, woa, wob, g2, wg, wple)


def _layer(h2, p2, pos_row, invf, tri, batch, seq, norm_g, w_in, b_f, qk_g, w_out, w_ple,
           ple_norm_g, w_ple_gate):
    d = D_BRANCH
    cuts = np.cumsum([0, d, d, d, d, N_HEADS, d, d, d, d])
    qa_w, ka_w, va_w, ga_w, f_w, qb_w, kb_w, vb_w, gb_w = (
        w_in[:, cuts[i]:cuts[i + 1]] for i in range(9))
    f_w = jnp.pad(f_w, ((0, 0), (0, F_ROWS - N_HEADS)))
    wt = jnp.concatenate([qa_w, ka_w, va_w, ga_w, qb_w, kb_w, f_w], axis=1).T.astype(BF16)
    wn = jnp.concatenate([vb_w, gb_w], axis=1).astype(BF16)
    bf_col = jnp.pad(b_f.astype(F32), (0, F_ROWS - N_HEADS)).reshape(F_ROWS, 1)
    scale = HEAD_DIM ** -0.5
    gcol = jnp.concatenate([jnp.tile(qk_g[0] * (scale * LOG2E), N_HEADS), jnp.tile(qk_g[1], N_HEADS),
                            jnp.tile(qk_g[2] * (scale * LOG2E), N_HEADS), jnp.tile(qk_g[3], N_HEADS)]
                           ).astype(F32).reshape(4 * d, 1)

    qt, k, vt, gt, qb, kb, vb, gb, stats = _proj(
        h2, pos_row, batch, seq, norm_g.reshape(1, D_MODEL), wt, wn, gcol, bf_col, invf, tri)

    oat = _fox(_fox_first_step(stats, batch, seq), qt, k, vt, gt).reshape(batch, d, seq)
    as3 = lambda a: a.reshape(batch, seq, d)
    ob = _dilated(as3(qb), as3(kb), as3(vb), as3(gb)).reshape(batch * seq, d)

    w_out_b = w_out.astype(BF16)
    return _post(h2, oat, ob, p2, batch, seq, w_out_b[:d], w_out_b[d:],
                 ple_norm_g.reshape(1, D_MODEL), w_ple_gate.astype(BF16), w_ple.astype(BF16))


def kernel(x, p, positions, norm_g, w_in, b_f, qk_norm_g, w_out, w_ple, ple_norm_g, w_ple_gate):
    batch, seq, _ = x.shape
    depth = p.shape[0]
    assert all(w // dil == N_BACK for w, dil in DILATED_PATTERNS)
    assert seq % DIL_SUPER == 0 and seq % ROW_TILE == 0 and FOX_T % FOX_UQ == 0
    pos_row = positions.reshape(1, batch * seq)
    inv_freq = ROPE_THETA ** (-jnp.arange(ROPE_HALF, dtype=F32) / ROPE_HALF)
    invf = inv_freq.reshape(ROPE_HALF, 1)
    idx = np.arange(ROW_TILE)
    tri = jnp.asarray(idx[:, None] <= idx[None, :], BF16)
    h2 = x.reshape(batch * seq, D_MODEL)
    for i in range(depth):
        h2 = _layer(h2, p[i].reshape(batch * seq, PLE_DIM), pos_row, invf, tri, batch, seq,
                    norm_g[i], w_in[i], b_f[i], qk_norm_g[i], w_out[i], w_ple[i],
                    ple_norm_g[i], w_ple_gate[i])
    return h2.reshape(batch, seq, D_MODEL)
```

```python
import functools

import numpy as np
import jax
import jax.numpy as jnp
from jax import lax
from jax.experimental import pallas as pl
from jax.experimental.pallas import tpu as pltpu

D_MODEL = 1024
HEAD_DIM = 64
N_HEADS = 8
D_BRANCH = N_HEADS * HEAD_DIM
PLE_DIM = 256
ROPE_THETA = 500000.0
ROPE_DIM = HEAD_DIM // 4
ROPE_HALF = ROPE_DIM // 2
DILATED_PATTERNS = ((128, 1), (512, 4), (2048, 16))
N_BACK = 128
EPS = 1e-6
NEG = -1e30

LANES = 128
SUBLANES = 8
BF16_ROWS = 16
VMEM_LIMIT_BYTES = 56 * 1024 * 1024

ROW_TILE = 512
FOX_T = 1024
FOX_UK = 256
FOX_UQ = 256
FOX_AHEAD = 8
FOX_VROWS = HEAD_DIM + BF16_ROWS
FOX_SKIP_BITS = 150.0
FOX_EDGE_OFFSETS = (None, 0, 1, 2)
LOG2E = 1.4426950408889634
DIL_TQ = N_BACK
DIL_TW = DIL_TQ + N_BACK
DIL_SUPER = max(d for _, d in DILATED_PATTERNS) * DIL_TQ
DIL_SPLIT = 4
DIL_COPY = 512
DIL_GROUP = 8
DIL_AHEAD = 4
F_ROWS = BF16_ROWS

F32 = jnp.float32
BF16 = jnp.bfloat16


def _dot(a, b):
    return jnp.dot(a, b, preferred_element_type=F32)


def _dot_nt(a, b):
    return lax.dot_general(a, b, (((1,), (1,)), ((), ())), preferred_element_type=F32)


def _dot_tn(a, b):
    return lax.dot_general(a, b, (((0,), (0,)), ((), ())), preferred_element_type=F32)


def _split3(x):
    hi = x.astype(BF16).astype(F32)
    r = x - hi
    mid = r.astype(BF16).astype(F32)
    lo = (r - mid).astype(BF16).astype(F32)
    return hi, mid, lo


def _rms_norm_rows(h, g_row):
    ms = jnp.mean(h * h, axis=-1, keepdims=True)
    return h * lax.rsqrt(ms + EPS) * g_row


def _proj_kernel(h_ref, pos_ref, g1_ref, wt_ref, wn_ref, gcol_ref, bf_ref, invf_ref, tri_ref,
                 qt_ref, k_ref, vt_ref, gt_ref, qb_ref, kb_ref, vb_ref, gb_ref, stats_ref,
                 carry_ref):
    t = pl.program_id(1)
    tm = ROW_TILE
    d = D_BRANCH
    u = _rms_norm_rows(h_ref[...], g1_ref[...]).astype(BF16)

    def feat_major(group):
        return _dot_nt(wt_ref[group * d:(group + 1) * d, :], u)

    zn = _dot(u, wn_ref[...])
    f = _dot_nt(wt_ref[6 * d:6 * d + F_ROWS, :], u) + bf_ref[...]
    z_qa = feat_major(0)

    vb_ref[...] = zn[:, :d]
    gate_b = zn[:, d:]
    gb_ref[...] = (gate_b * jax.nn.sigmoid(gate_b)).astype(BF16)
    z_ka = feat_major(1)

    def head_norm(z, group):
        z3 = z.reshape(N_HEADS, HEAD_DIM, tm)
        ss = jnp.sum(z3 * z3, axis=1, keepdims=True)
        y3 = z3 * lax.rsqrt(ss * (1.0 / HEAD_DIM) + EPS)
        return y3.reshape(d, tm) * gcol_ref[group * d:(group + 1) * d, :]

    ang = invf_ref[...] * pos_ref[...].astype(F32)
    cos = jnp.cos(ang)
    sin = jnp.sin(ang)

    def rope(y):
        y3 = y.reshape(N_HEADS, HEAD_DIM, tm)
        x1 = y3[:, :ROPE_HALF, :]
        x2 = y3[:, ROPE_HALF:ROPE_DIM, :]
        rot = [x1 * cos - x2 * sin, x2 * cos + x1 * sin, y3[:, ROPE_DIM:, :]]
        return jnp.concatenate(rot, axis=1).reshape(d, tm)

    lf = jnp.minimum(f, 0.0) - jnp.log1p(jnp.exp(-jnp.abs(f)))
    stacked = jnp.concatenate([x.astype(BF16) for x in _split3(lf)], axis=0)
    cs = _dot(stacked, tri_ref[...])
    carry = jnp.where(t == 0, 0.0, carry_ref[:, 0:1])
    c = cs[:F_ROWS] + cs[F_ROWS:2 * F_ROWS] + cs[2 * F_ROWS:] + carry
    carry_ref[...] = jnp.broadcast_to(c[:, tm - 1:tm], (F_ROWS, LANES))
    c2 = c * LOG2E

    qa = head_norm(z_qa, 0)
    z_va = feat_major(2)
    ka = head_norm(z_ka, 1)
    z_ga = feat_major(3)
    row = lax.broadcasted_iota(jnp.int32, (SUBLANES, tm), 0)
    pad = jnp.zeros((HEAD_DIM - SUBLANES, tm), F32)
    for hd in range(N_HEADS):
        c_hi, c_mid, c_lo = _split3(c2[hd:hd + 1, :])
        q_aug = jnp.where(row == 0, c_hi, jnp.where(row == 1, c_mid, jnp.where(
            row == 2, c_lo, jnp.where(row < 6, 1.0, 0.0))))
        k_aug = jnp.where(row < 3, 1.0, jnp.where(row == 3, -c_hi, jnp.where(
            row == 4, -c_mid, jnp.where(row == 5, -c_lo, 0.0))))
        rows = slice(hd * HEAD_DIM, (hd + 1) * HEAD_DIM)
        qt_ref[0, hd] = jnp.concatenate([qa[rows], q_aug, pad], axis=0).astype(BF16)
        k_ref[0, hd] = jnp.concatenate([ka[rows], k_aug, pad], axis=0).T.astype(BF16)

    ones_row = jnp.where(lax.broadcasted_iota(jnp.int32, (N_HEADS, BF16_ROWS, tm), 1) == 0,
                         1.0, 0.0)
    z_qb = feat_major(4)
    va = z_va.reshape(N_HEADS, HEAD_DIM, tm)
    vt_ref[0, :, 0] = jnp.concatenate([va, ones_row], axis=1).astype(BF16)
    z_kb = feat_major(5)
    gt_ref[0] = (z_ga * jax.nn.sigmoid(z_ga)).reshape(N_HEADS, HEAD_DIM, tm).astype(BF16)

    qb_ref[...] = rope(head_norm(z_qb, 2)).T
    kb_ref[...] = rope(head_norm(z_kb, 3)).T

    def max_sq_norm(y):
        y3 = y.reshape(N_HEADS, HEAD_DIM, tm)
        return jnp.max(jnp.sum(y3 * y3, axis=1), axis=-1, keepdims=True)

    lane = lax.broadcasted_iota(jnp.int32, (N_HEADS, LANES), 1)
    stats = jnp.where(lane == 0, max_sq_norm(qa), jnp.where(lane == 1, max_sq_norm(ka), 0.0))
    for sub in range(tm // FOX_UQ):
        lo = sub * FOX_UQ
        hi = lo + FOX_UQ - 1
        stats = jnp.where(lane == 2 + 2 * sub, c2[:N_HEADS, lo:lo + 1],
                          jnp.where(lane == 3 + 2 * sub, c2[:N_HEADS, hi:hi + 1], stats))
    stats_ref[0] = stats


def _proj(h2, pos_row, batch, seq, g1, wt, wn, gcol, bf_col, invf, tri):
    nt = seq // ROW_TILE
    rows = batch * seq
    d = D_BRANCH
    row = lambda w: pl.BlockSpec((ROW_TILE, w), lambda b, t: (b * nt + t, 0))
    full = lambda a: pl.BlockSpec(a.shape, lambda b, t: (0,) * a.ndim)
    out_shape = (
        jax.ShapeDtypeStruct((batch, N_HEADS, 2 * HEAD_DIM, seq), BF16),
        jax.ShapeDtypeStruct((batch, N_HEADS, seq, 2 * HEAD_DIM), BF16),
        jax.ShapeDtypeStruct((batch, N_HEADS, nt, FOX_VROWS, ROW_TILE), BF16),
        jax.ShapeDtypeStruct((batch, N_HEADS, HEAD_DIM, seq), BF16),
        jax.ShapeDtypeStruct((rows, d), F32),
        jax.ShapeDtypeStruct((rows, d), F32),
        jax.ShapeDtypeStruct((rows, d), F32),
        jax.ShapeDtypeStruct((rows, d), BF16),
        jax.ShapeDtypeStruct((batch * nt, N_HEADS, LANES), F32),
    )
    out_specs = (
        pl.BlockSpec((1, N_HEADS, 2 * HEAD_DIM, ROW_TILE), lambda b, t: (b, 0, 0, t)),
        pl.BlockSpec((1, N_HEADS, ROW_TILE, 2 * HEAD_DIM), lambda b, t: (b, 0, t, 0)),
        pl.BlockSpec((1, N_HEADS, 1, FOX_VROWS, ROW_TILE), lambda b, t: (b, 0, t, 0, 0)),
        pl.BlockSpec((1, N_HEADS, HEAD_DIM, ROW_TILE), lambda b, t: (b, 0, 0, t)),
        row(d), row(d), row(d), row(d),
        pl.BlockSpec((1, N_HEADS, LANES), lambda b, t: (b * nt + t, 0, 0)),
    )
    return pl.pallas_call(
        _proj_kernel,
        out_shape=out_shape,
        grid=(batch, nt),
        in_specs=[row(D_MODEL), pl.BlockSpec((1, ROW_TILE), lambda b, t: (0, b * nt + t)),
                  full(g1), full(wt), full(wn), full(gcol), full(bf_col), full(invf), full(tri)],
        out_specs=out_specs,
        scratch_shapes=[pltpu.VMEM((F_ROWS, LANES), F32)],
        compiler_params=pltpu.CompilerParams(
            dimension_semantics=("arbitrary", "arbitrary"), vmem_limit_bytes=VMEM_LIMIT_BYTES),
        name="in_proj",
    )(h2, pos_row, g1, wt, wn, gcol, bf_col, invf, tri)


def _fox_kernel(plan_ref, q_ref, k_ref, v_ref, g_ref, o_ref):
    i = pl.program_id(2)
    plan = plan_ref[(pl.program_id(0) * N_HEADS + pl.program_id(1)) * pl.num_programs(2) + i]
    first = plan // len(FOX_EDGE_OFFSETS)
    edge = plan % len(FOX_EDGE_OFFSETS)
    uk, uq = FOX_UK, FOX_UQ
    n_q = FOX_T // uq
    q = [q_ref[0, 0, :, a * uq:(a + 1) * uq] for a in range(n_q)]
    rel = (lax.broadcasted_iota(jnp.int32, (uk, uq), 0)
           - lax.broadcasted_iota(jnp.int32, (uk, uq), 1))

    def scores(j, ks, a):
        start = pl.multiple_of(j * FOX_T, FOX_T)
        return _dot(k_ref[0, 0, pl.ds(start + ks * uk, uk), :], q[a])

    def consume(s, v_sub, state, max_rel):
        m, acc = state
        if max_rel is not None:
            s = jnp.where(rel <= max_rel, s, NEG)
        m_new = jnp.maximum(m, jnp.max(s, axis=0, keepdims=True))
        alpha = jnp.exp2(m - m_new)
        p = jnp.exp2((s - m_new).astype(BF16))
        acc = alpha * acc + _dot(v_sub, p)
        return m_new, acc

    def step(j, carry, diagonal=False, min_offset=None):
        s_first, states = carry
        states = list(states)
        units = []
        for ks in range(FOX_T // uk):
            for a in range(n_q):
                max_rel = None
                if min_offset is not None and ks - a < min_offset:
                    continue
                if diagonal:
                    if ks * uk >= (a + 1) * uq:
                        continue
                    if (ks + 1) * uk - 1 > a * uq:
                        max_rel = a * uq - ks * uk
                units.append((ks, a, max_rel))
        pending = {0: scores(j, units[0][0], units[0][1]) if s_first is None else s_first}
        queue = list(range(1, len(units))) + ([] if diagonal else ["next"])
        result = {}

        def issue(item):
            if item == "next":
                result["next"] = scores(j + 1, 0, 0)
            else:
                pending[item] = scores(j, units[item][0], units[item][1])

        for item in queue[:FOX_AHEAD - 1]:
            issue(item)
        for n, (ks, a, max_rel) in enumerate(units):
            if n + FOX_AHEAD - 1 < len(queue):
                issue(queue[n + FOX_AHEAD - 1])
            chunk, off = divmod(ks * uk, ROW_TILE)
            v_sub = v_ref[0, 0, j * (FOX_T // ROW_TILE) + chunk, :, off:off + uk]
            states[a] = consume(pending.pop(n), v_sub, states[a], max_rel)
        return result.get("next"), tuple(states)

    init = tuple((jnp.full((1, uq), NEG, F32), jnp.zeros((FOX_VROWS, uq), F32))
                 for _ in range(n_q))
    branches = [lambda: (scores(i, 0, 0), init)]
    for offset in FOX_EDGE_OFFSETS:
        branches.append(functools.partial(step, first, (None, init), min_offset=offset))
    carry = lax.switch(jnp.where(first >= i, 0, edge + 1), branches)
    carry = lax.fori_loop(first + 1, i, step, carry)
    _, states = step(i, carry, diagonal=True)
    out = jnp.concatenate([acc[:HEAD_DIM] / acc[HEAD_DIM:HEAD_DIM + 1] for _, acc in states],
                          axis=1)
    o_ref[0, 0] = (out * g_ref[0, 0].astype(F32)).astype(BF16)


def _fox_plan(stats, batch, seq):
    nt, nq = seq // ROW_TILE, seq // FOX_T
    sub_t, sub_q = ROW_TILE // FOX_UQ, FOX_T // FOX_UQ
    st = stats.reshape(batch, nt, N_HEADS, LANES)
    q_sq = st[..., 0].reshape(batch, nq, nt // nq, N_HEADS).max(axis=2)
    k_sq = st[..., 1].max(axis=1, keepdims=True)
    bound = jnp.sqrt(q_sq * k_sq) * 1.01 + 1.0
    edges = st[..., 2:2 + 2 * sub_t].reshape(batch, nt, N_HEADS, sub_t, 2)
    edges = edges.transpose(0, 1, 3, 2, 4).reshape(batch, nq, sub_q, N_HEADS, 2)
    c_first, c_last = edges[..., 0], edges[..., 1]
    gap = c_first[:, :, :, None, None, :] - c_last[:, None, None, :, :, :]
    skip = (2.0 * bound[:, :, None, None, None, :] + gap) < -FOX_SKIP_BITS
    steps = jnp.arange(nq, dtype=jnp.int32)
    step_skip = jnp.all(skip, axis=(2, 4))
    first = jnp.min(jnp.where(step_skip, nq, steps[None, None, :, None]), axis=2)
    first = jnp.minimum(first, steps[None, :, None])
    other_step = steps[None, None, None, :, None, None] != first[:, :, None, None, None, :]
    at_first = jnp.all(skip | other_step, axis=3)
    offset = (jnp.arange(sub_q)[None, :] - jnp.arange(sub_q)[:, None])
    variant = jnp.zeros_like(first)
    for v, min_offset in enumerate(FOX_EDGE_OFFSETS):
        if min_offset is not None:
            dropped = (offset < min_offset)[None, None, :, :, None]
            ok = jnp.all(at_first | ~dropped, axis=(2, 3))
            variant = jnp.where(ok, v, variant)
    plan = first * len(FOX_EDGE_OFFSETS) + variant
    return plan.transpose(0, 2, 1).reshape(-1).astype(jnp.int32)


def _fox(first, qt, k, vt, gt):
    b, _, _, s = qt.shape
    q_spec = pl.BlockSpec((1, 1, 2 * HEAD_DIM, FOX_T), lambda bi, h, i, f: (bi, h, 0, i))
    k_spec = pl.BlockSpec((1, 1, s, 2 * HEAD_DIM), lambda bi, h, i, f: (bi, h, 0, 0))
    v_spec = pl.BlockSpec((1, 1, s // ROW_TILE, FOX_VROWS, ROW_TILE),
                          lambda bi, h, i, f: (bi, h, 0, 0, 0))
    o_spec = pl.BlockSpec((1, 1, HEAD_DIM, FOX_T), lambda bi, h, i, f: (bi, h, 0, i))
    return pl.pallas_call(
        _fox_kernel,
        out_shape=jax.ShapeDtypeStruct((b, N_HEADS, HEAD_DIM, s), BF16),
        grid_spec=pltpu.PrefetchScalarGridSpec(
            num_scalar_prefetch=1,
            grid=(b, N_HEADS, s // FOX_T),
            in_specs=[q_spec, k_spec, v_spec, o_spec],
            out_specs=o_spec),
        compiler_params=pltpu.CompilerParams(
            dimension_semantics=("arbitrary",) * 3, vmem_limit_bytes=VMEM_LIMIT_BYTES),
        name="fox_attention",
    )(first, qt, k, vt, gt)


def _dil_kernel(q_ref, k_ref, v_ref, g_ref, o_ref, q4_s, k4_s, v4_s, acc_s, m_s, l_s, *, seq):
    tq, tw, sup = DIL_TQ, DIL_TW, DIL_SUPER
    part = seq // DIL_SPLIT

    def regroup(c, carry):
        for src, dst in ((q_ref, q4_s), (k_ref, k4_s), (v_ref, v4_s)):
            for r in range(DIL_SPLIT):
                rows = src[0, pl.ds(c * DIL_SPLIT * DIL_COPY + r, DIL_COPY, stride=DIL_SPLIT), :]
                dst[pl.ds(r * part + c * DIL_COPY, DIL_COPY), :] = rows
        return carry

    lax.fori_loop(0, part // DIL_COPY, regroup, 0)

    def stream_rows(nat_ref, split_ref, d, r, pos, n):
        if d % DIL_SPLIT:
            x = nat_ref[0, pl.ds(d * pos + r, n, stride=d), :]
        else:
            sd = d // DIL_SPLIT
            start = (r % DIL_SPLIT) * part + sd * pos + r // DIL_SPLIT
            x = split_ref[pl.ds(start, n, stride=sd), :]
        return x.astype(BF16)

    lane = lax.broadcasted_iota(jnp.int32, (1, LANES), 1)
    low = lane < HEAD_DIM
    rel = (lax.broadcasted_iota(jnp.int32, (tq, tw), 0)
           - lax.broadcasted_iota(jnp.int32, (tq, tw), 1))

    def band_bias(q_minus_w):
        dist = rel + q_minus_w
        return jnp.where(dist >= 0, jnp.where(dist <= N_BACK, 0.0, NEG), NEG)

    bias_inner = band_bias(N_BACK)
    bias_start = band_bias(0)

    def issue(d, base, idx, nblk):
        r = idx // nblk
        jb = idx % nblk
        qs = base // d + jb * tq
        ws = jnp.maximum(qs - N_BACK, 0)
        q = stream_rows(q_ref, q4_s, d, r, qs, tq)
        kw = stream_rows(k_ref, k4_s, d, r, ws, tw)
        zero = jnp.zeros_like(q)
        q2 = jnp.concatenate([jnp.where(low, q, zero), jnp.where(low, zero, q)], axis=0)
        return _dot_nt(q2, kw), r, qs, ws, d * (qs - base // d) + r

    def consume(g, d, s, r, qs, ws, out_row):
        vw = stream_rows(v_ref, v4_s, d, r, ws, tw)
        bias = jnp.where(qs == ws, bias_start, bias_inner)
        ps, ms, ls = [], [], []
        for hh in range(2):
            sm = s[hh * tq:(hh + 1) * tq] + bias
            m = jnp.max(sm, axis=-1, keepdims=True)
            p = jnp.exp2(sm - m)
            ps.append(p.astype(BF16))
            ms.append(jnp.broadcast_to(m, (tq, LANES)))
            ls.append(jnp.broadcast_to(jnp.sum(p, axis=-1, keepdims=True), (tq, LANES)))
        acc = _dot(jnp.concatenate(ps, axis=0), vw)
        accs = [acc[:tq], acc[tq:]]
        rows = pl.ds(out_row, tq, stride=d)
        acc_s[g, rows, :] = jnp.where(low, accs[0], accs[1])
        m_s[g, rows, :] = jnp.where(low, ms[0], ms[1])
        l_s[g, rows, :] = jnp.where(low, ls[0], ls[1])

    def superblock(sb, carry):
        base = pl.multiple_of(sb * sup, sup)
        for g, (_, d) in enumerate(DILATED_PATTERNS):
            nblk = sup // (d * tq)

            def body(it, c, g=g, d=d, nblk=nblk):
                first = it * DIL_GROUP
                pending = {n: issue(d, base, first + n, nblk) for n in range(DIL_AHEAD)}
                for n in range(DIL_GROUP):
                    if n + DIL_AHEAD < DIL_GROUP:
                        pending[n + DIL_AHEAD] = issue(d, base, first + n + DIL_AHEAD, nblk)
                    consume(g, d, *pending.pop(n))
                return c

            lax.fori_loop(0, d * nblk // DIL_GROUP, body, 0)

        rows = pl.ds(base, sup)
        m1, m2, m3 = m_s[0], m_s[1], m_s[2]
        mx = jnp.maximum(jnp.maximum(m1, m2), m3)
        e1, e2, e3 = jnp.exp2(m1 - mx), jnp.exp2(m2 - mx), jnp.exp2(m3 - mx)
        num = e1 * acc_s[0] + e2 * acc_s[1] + e3 * acc_s[2]
        den = e1 * l_s[0] + e2 * l_s[1] + e3 * l_s[2]
        o_ref[0, rows, :] = (num / den * g_ref[0, rows, :].astype(F32)).astype(BF16)
        return carry

    lax.fori_loop(0, seq // sup, superblock, 0)


def _dilated(qb, kb, vb, gb):
    b, s, _ = qb.shape
    spec = pl.BlockSpec((1, s, LANES), lambda bi, p: (bi, 0, p))
    regrouped = pltpu.VMEM((s, LANES), F32)
    scratch = pltpu.VMEM((len(DILATED_PATTERNS), DIL_SUPER, LANES), F32)
    return pl.pallas_call(
        functools.partial(_dil_kernel, seq=s),
        out_shape=jax.ShapeDtypeStruct((b, s, D_BRANCH), BF16),
        grid=(b, D_BRANCH // LANES),
        in_specs=[spec, spec, spec, spec],
        out_specs=spec,
        scratch_shapes=[regrouped, regrouped, regrouped, scratch, scratch, scratch],
        compiler_params=pltpu.CompilerParams(
            dimension_semantics=("arbitrary",) * 2, vmem_limit_bytes=VMEM_LIMIT_BYTES),
        name="dilated_attn",
    )(qb, kb, vb, gb)


def _post_kernel(h_ref, oat_ref, ob_ref, p_ref, woa_ref, wob_ref, g2_ref, wg_ref, wple_ref,
                 out_ref):
    h1 = h_ref[...] + _dot_tn(oat_ref[0], woa_ref[...]) + _dot(ob_ref[...], wob_ref[...])
    u2 = _rms_norm_rows(h1, g2_ref[...]).astype(BF16)
    gate = jax.nn.sigmoid(_dot(u2, wg_ref[...]))
    ple = _dot(p_ref[...].astype(BF16), wple_ref[...])
    out_ref[...] = h1 + ple * gate


def _post(h2, oat, ob, p2, batch, seq, woa, wob, g2, wg, wple):
    nt = seq // ROW_TILE
    row = lambda w: pl.BlockSpec((ROW_TILE, w), lambda b, t: (b * nt + t, 0))
    full = lambda a: pl.BlockSpec(a.shape, lambda b, t: (0,) * a.ndim)
    return pl.pallas_call(
        _post_kernel,
        out_shape=jax.ShapeDtypeStruct((batch * seq, D_MODEL), F32),
        grid=(batch, nt),
        in_specs=[row(D_MODEL), pl.BlockSpec((1, D_BRANCH, ROW_TILE), lambda b, t: (b, 0, t)),
                  row(D_BRANCH), row(PLE_DIM),
                  full(woa), full(wob), full(g2), full(wg), full(wple)],
        out_specs=row(D_MODEL),
        compiler_params=pltpu.CompilerParams(
            dimension_semantics=("arbitrary", "arbitrary"), vmem_limit_bytes=VMEM_LIMIT_BYTES),
        name="out_stage",
    )(h2, oat, ob, p2, woa, wob, g2, wg, wple)


def _layer(h2, p2, pos_row, invf, tri, batch, seq, norm_g, w_in, b_f, qk_g, w_out, w_ple,
           ple_norm_g, w_ple_gate):
    d = D_BRANCH
    cuts = np.cumsum([0, d, d, d, d, N_HEADS, d, d, d, d])
    qa_w, ka_w, va_w, ga_w, f_w, qb_w, kb_w, vb_w, gb_w = (
        w_in[:, cuts[i]:cuts[i + 1]] for i in range(9))
    f_w = jnp.pad(f_w, ((0, 0), (0, F_ROWS - N_HEADS)))
    wt = jnp.concatenate([qa_w, ka_w, va_w, ga_w, qb_w, kb_w, f_w], axis=1).T.astype(BF16)
    wn = jnp.concatenate([vb_w, gb_w], axis=1).astype(BF16)
    bf_col = jnp.pad(b_f.astype(F32), (0, F_ROWS - N_HEADS)).reshape(F_ROWS, 1)
    scale = HEAD_DIM ** -0.5
    gcol = jnp.concatenate([jnp.tile(qk_g[0] * (scale * LOG2E), N_HEADS), jnp.tile(qk_g[1], N_HEADS),
                            jnp.tile(qk_g[2] * (scale * LOG2E), N_HEADS), jnp.tile(qk_g[3], N_HEADS)]
                           ).astype(F32).reshape(4 * d, 1)

    qt, k, vt, gt, qb, kb, vb, gb, stats = _proj(
        h2, pos_row, batch, seq, norm_g.reshape(1, D_MODEL), wt, wn, gcol, bf_col, invf, tri)

    oat = _fox(_fox_plan(stats, batch, seq), qt, k, vt, gt).reshape(batch, d, seq)
    as3 = lambda a: a.reshape(batch, seq, d)
    ob = _dilated(as3(qb), as3(kb), as3(vb), as3(gb)).reshape(batch * seq, d)

    w_out_b = w_out.astype(BF16)
    return _post(h2, oat, ob, p2, batch, seq, w_out_b[:d], w_out_b[d:],
                 ple_norm_g.reshape(1, D_MODEL), w_ple_gate.astype(BF16), w_ple.astype(BF16))


def kernel(x, p, positions, norm_g, w_in, b_f, qk_norm_g, w_out, w_ple, ple_norm_g, w_ple_gate):
    batch, seq, _ = x.shape
    depth = p.shape[0]
    assert all(w // dil == N_BACK for w, dil in DILATED_PATTERNS)
    assert seq % DIL_SUPER == 0 and seq % ROW_TILE == 0 and FOX_T % FOX_UQ == 0
    pos_row = positions.reshape(1, batch * seq)
    inv_freq = ROPE_THETA ** (-jnp.arange(ROPE_HALF, dtype=F32) / ROPE_HALF)
    invf = inv_freq.reshape(ROPE_HALF, 1)
    idx = np.arange(ROW_TILE)
    tri = jnp.asarray(idx[:, None] <= idx[None, :], BF16)
    h2 = x.reshape(batch * seq, D_MODEL)
    for i in range(depth):
        h2 = _layer(h2, p[i].reshape(batch * seq, PLE_DIM), pos_row, invf, tri, batch, seq,
                    norm_g[i], w_in[i], b_f[i], qk_norm_g[i], w_out[i], w_ple[i],
                    ple_norm_g[i], w_ple_gate[i])
    return h2.reshape(batch, seq, D_MODEL)
```

```python
import functools

import numpy as np
import jax
import jax.numpy as jnp
from jax import lax
from jax.experimental import pallas as pl
from jax.experimental.pallas import tpu as pltpu

D_MODEL = 1024
HEAD_DIM = 64
N_HEADS = 8
D_BRANCH = N_HEADS * HEAD_DIM
PLE_DIM = 256
ROPE_THETA = 500000.0
ROPE_DIM = HEAD_DIM // 4
ROPE_HALF = ROPE_DIM // 2
DILATED_PATTERNS = ((128, 1), (512, 4), (2048, 16))
N_BACK = 128
EPS = 1e-6
NEG = -1e30

LANES = 128
SUBLANES = 8
BF16_ROWS = 16
VMEM_LIMIT_BYTES = 56 * 1024 * 1024

ROW_TILE = 512
FOX_T = 1024
FOX_UK = 256
FOX_UQ = 256
FOX_AHEAD = 8
FOX_VROWS = HEAD_DIM + BF16_ROWS
FOX_SKIP_BITS = 150.0
FOX_EDGE_OFFSETS = (None, 0, 1, 2)
LOG2E = 1.4426950408889634
DIL_TQ = N_BACK
DIL_TW = DIL_TQ + N_BACK
DIL_SUPER = max(d for _, d in DILATED_PATTERNS) * DIL_TQ
DIL_SPLIT = 4
DIL_COPY = 512
DIL_GROUP = 8
DIL_AHEAD = 4
F_ROWS = BF16_ROWS

F32 = jnp.float32
BF16 = jnp.bfloat16


def _dot(a, b):
    return jnp.dot(a, b, preferred_element_type=F32)


def _dot_nt(a, b):
    return lax.dot_general(a, b, (((1,), (1,)), ((), ())), preferred_element_type=F32)


def _dot_tn(a, b):
    return lax.dot_general(a, b, (((0,), (0,)), ((), ())), preferred_element_type=F32)


def _split3(x):
    hi = x.astype(BF16).astype(F32)
    r = x - hi
    mid = r.astype(BF16).astype(F32)
    lo = (r - mid).astype(BF16).astype(F32)
    return hi, mid, lo


def _rms_norm_rows(h, g_row):
    ms = jnp.mean(h * h, axis=-1, keepdims=True)
    return h * lax.rsqrt(ms + EPS) * g_row


def _proj_kernel(h_ref, pos_ref, g1_ref, wt_ref, wn_ref, gcol_ref, bf_ref, invf_ref, tri_ref,
                 qt_ref, k_ref, vt_ref, gt_ref, qb_ref, kb_ref, vb_ref, gb_ref, stats_ref,
                 carry_ref):
    t = pl.program_id(1)
    tm = ROW_TILE
    d = D_BRANCH
    u = _rms_norm_rows(h_ref[...], g1_ref[...]).astype(BF16)

    def feat_major(group):
        return _dot_nt(wt_ref[group * d:(group + 1) * d, :], u)

    zn = _dot(u, wn_ref[...])
    f = _dot_nt(wt_ref[6 * d:6 * d + F_ROWS, :], u) + bf_ref[...]
    z_qa = feat_major(0)

    vb_ref[...] = zn[:, :d]
    gate_b = zn[:, d:]
    gb_ref[...] = (gate_b * jax.nn.sigmoid(gate_b)).astype(BF16)
    z_ka = feat_major(1)

    def head_norm(z, group):
        z3 = z.reshape(N_HEADS, HEAD_DIM, tm)
        ss = jnp.sum(z3 * z3, axis=1, keepdims=True)
        y3 = z3 * lax.rsqrt(ss * (1.0 / HEAD_DIM) + EPS)
        return y3.reshape(d, tm) * gcol_ref[group * d:(group + 1) * d, :]

    ang = invf_ref[...] * pos_ref[...].astype(F32)
    cos = jnp.cos(ang)
    sin = jnp.sin(ang)

    def rope(y):
        y3 = y.reshape(N_HEADS, HEAD_DIM, tm)
        x1 = y3[:, :ROPE_HALF, :]
        x2 = y3[:, ROPE_HALF:ROPE_DIM, :]
        rot = [x1 * cos - x2 * sin, x2 * cos + x1 * sin, y3[:, ROPE_DIM:, :]]
        return jnp.concatenate(rot, axis=1).reshape(d, tm)

    lf = jnp.minimum(f, 0.0) - jnp.log1p(jnp.exp(-jnp.abs(f)))
    stacked = jnp.concatenate([x.astype(BF16) for x in _split3(lf)], axis=0)
    cs = _dot(stacked, tri_ref[...])
    carry = jnp.where(t == 0, 0.0, carry_ref[:, 0:1])
    c = cs[:F_ROWS] + cs[F_ROWS:2 * F_ROWS] + cs[2 * F_ROWS:] + carry
    carry_ref[...] = jnp.broadcast_to(c[:, tm - 1:tm], (F_ROWS, LANES))
    c2 = c * LOG2E

    qa = head_norm(z_qa, 0)
    z_va = feat_major(2)
    ka = head_norm(z_ka, 1)
    z_ga = feat_major(3)
    row = lax.broadcasted_iota(jnp.int32, (SUBLANES, tm), 0)
    pad = jnp.zeros((HEAD_DIM - SUBLANES, tm), F32)
    for hd in range(N_HEADS):
        c_hi, c_mid, c_lo = _split3(c2[hd:hd + 1, :])
        q_aug = jnp.where(row == 0, c_hi, jnp.where(row == 1, c_mid, jnp.where(
            row == 2, c_lo, jnp.where(row < 6, 1.0, 0.0))))
        k_aug = jnp.where(row < 3, 1.0, jnp.where(row == 3, -c_hi, jnp.where(
            row == 4, -c_mid, jnp.where(row == 5, -c_lo, 0.0))))
        rows = slice(hd * HEAD_DIM, (hd + 1) * HEAD_DIM)
        qt_ref[0, hd] = jnp.concatenate([qa[rows], q_aug, pad], axis=0).astype(BF16)
        k_ref[0, hd] = jnp.concatenate([ka[rows], k_aug, pad], axis=0).T.astype(BF16)

    ones_row = jnp.where(lax.broadcasted_iota(jnp.int32, (N_HEADS, BF16_ROWS, tm), 1) == 0,
                         1.0, 0.0)
    z_qb = feat_major(4)
    va = z_va.reshape(N_HEADS, HEAD_DIM, tm)
    vt_ref[0, :, 0] = jnp.concatenate([va, ones_row], axis=1).astype(BF16)
    z_kb = feat_major(5)
    gt_ref[0] = (z_ga * jax.nn.sigmoid(z_ga)).reshape(N_HEADS, HEAD_DIM, tm).astype(BF16)

    qb_ref[...] = rope(head_norm(z_qb, 2)).T
    kb_ref[...] = rope(head_norm(z_kb, 3)).T

    def max_sq_norm(y):
        y3 = y.reshape(N_HEADS, HEAD_DIM, tm)
        return jnp.max(jnp.sum(y3 * y3, axis=1), axis=-1, keepdims=True)

    lane = lax.broadcasted_iota(jnp.int32, (N_HEADS, LANES), 1)
    stats = jnp.where(lane == 0, max_sq_norm(qa), jnp.where(lane == 1, max_sq_norm(ka), 0.0))
    for sub in range(tm // FOX_UQ):
        lo = sub * FOX_UQ
        hi = lo + FOX_UQ - 1
        stats = jnp.where(lane == 2 + 2 * sub, c2[:N_HEADS, lo:lo + 1],
                          jnp.where(lane == 3 + 2 * sub, c2[:N_HEADS, hi:hi + 1], stats))
    stats_ref[0] = stats


def _proj(h2, pos_row, batch, seq, g1, wt, wn, gcol, bf_col, invf, tri):
    nt = seq // ROW_TILE
    rows = batch * seq
    d = D_BRANCH
    row = lambda w: pl.BlockSpec((ROW_TILE, w), lambda b, t: (b * nt + t, 0))
    full = lambda a: pl.BlockSpec(a.shape, lambda b, t: (0,) * a.ndim)
    out_shape = (
        jax.ShapeDtypeStruct((batch, N_HEADS, 2 * HEAD_DIM, seq), BF16),
        jax.ShapeDtypeStruct((batch, N_HEADS, seq, 2 * HEAD_DIM), BF16),
        jax.ShapeDtypeStruct((batch, N_HEADS, nt, FOX_VROWS, ROW_TILE), BF16),
        jax.ShapeDtypeStruct((batch, N_HEADS, HEAD_DIM, seq), BF16),
        jax.ShapeDtypeStruct((rows, d), F32),
        jax.ShapeDtypeStruct((rows, d), F32),
        jax.ShapeDtypeStruct((rows, d), F32),
        jax.ShapeDtypeStruct((rows, d), BF16),
        jax.ShapeDtypeStruct((batch * nt, N_HEADS, LANES), F32),
    )
    out_specs = (
        pl.BlockSpec((1, N_HEADS, 2 * HEAD_DIM, ROW_TILE), lambda b, t: (b, 0, 0, t)),
        pl.BlockSpec((1, N_HEADS, ROW_TILE, 2 * HEAD_DIM), lambda b, t: (b, 0, t, 0)),
        pl.BlockSpec((1, N_HEADS, 1, FOX_VROWS, ROW_TILE), lambda b, t: (b, 0, t, 0, 0)),
        pl.BlockSpec((1, N_HEADS, HEAD_DIM, ROW_TILE), lambda b, t: (b, 0, 0, t)),
        row(d), row(d), row(d), row(d),
        pl.BlockSpec((1, N_HEADS, LANES), lambda b, t: (b * nt + t, 0, 0)),
    )
    return pl.pallas_call(
        _proj_kernel,
        out_shape=out_shape,
        grid=(batch, nt),
        in_specs=[row(D_MODEL), pl.BlockSpec((1, ROW_TILE), lambda b, t: (0, b * nt + t)),
                  full(g1), full(wt), full(wn), full(gcol), full(bf_col), full(invf), full(tri)],
        out_specs=out_specs,
        scratch_shapes=[pltpu.VMEM((F_ROWS, LANES), F32)],
        compiler_params=pltpu.CompilerParams(
            dimension_semantics=("arbitrary", "arbitrary"), vmem_limit_bytes=VMEM_LIMIT_BYTES),
        name="in_proj",
    )(h2, pos_row, g1, wt, wn, gcol, bf_col, invf, tri)


def _fox_kernel(plan_ref, q_ref, k_ref, v_ref, g_ref, o_ref):
    i = pl.program_id(2)
    plan = plan_ref[(pl.program_id(0) * N_HEADS + pl.program_id(1)) * pl.num_programs(2) + i]
    first = plan // len(FOX_EDGE_OFFSETS)
    edge = plan % len(FOX_EDGE_OFFSETS)
    uk, uq = FOX_UK, FOX_UQ
    n_q = FOX_T // uq
    q = [q_ref[0, 0, :, a * uq:(a + 1) * uq] for a in range(n_q)]
    rel = (lax.broadcasted_iota(jnp.int32, (uk, uq), 0)
           - lax.broadcasted_iota(jnp.int32, (uk, uq), 1))

    def scores(j, ks, a):
        start = pl.multiple_of(j * FOX_T, FOX_T)
        return _dot(k_ref[0, 0, pl.ds(start + ks * uk, uk), :], q[a])

    def consume(s, v_sub, state, max_rel):
        m, acc = state
        if max_rel is not None:
            s = jnp.where(rel <= max_rel, s, NEG)
        m_new = jnp.maximum(m, jnp.max(s, axis=0, keepdims=True))
        alpha = jnp.exp2(m - m_new)
        p = jnp.exp2((s - m_new).astype(BF16))
        acc = alpha * acc + _dot(v_sub, p)
        return m_new, acc

    def step(j, carry, diagonal=False, min_offset=None):
        s_first, states = carry
        states = list(states)
        units = []
        for ks in range(FOX_T // uk):
            for a in range(n_q):
                max_rel = None
                if min_offset is not None and ks - a < min_offset:
                    continue
                if diagonal:
                    if ks * uk >= (a + 1) * uq:
                        continue
                    if (ks + 1) * uk - 1 > a * uq:
                        max_rel = a * uq - ks * uk
                units.append((ks, a, max_rel))
        pending = {0: scores(j, units[0][0], units[0][1]) if s_first is None else s_first}
        queue = list(range(1, len(units))) + ([] if diagonal else ["next"])
        result = {}

        def issue(item):
            if item == "next":
                result["next"] = scores(j + 1, 0, 0)
            else:
                pending[item] = scores(j, units[item][0], units[item][1])

        for item in queue[:FOX_AHEAD - 1]:
            issue(item)
        for n, (ks, a, max_rel) in enumerate(units):
            if n + FOX_AHEAD - 1 < len(queue):
                issue(queue[n + FOX_AHEAD - 1])
            chunk, off = divmod(ks * uk, ROW_TILE)
            v_sub = v_ref[0, 0, j * (FOX_T // ROW_TILE) + chunk, :, off:off + uk]
            states[a] = consume(pending.pop(n), v_sub, states[a], max_rel)
        return result.get("next"), tuple(states)

    init = tuple((jnp.full((1, uq), NEG, F32), jnp.zeros((FOX_VROWS, uq), F32))
                 for _ in range(n_q))
    branches = [lambda: (scores(i, 0, 0), init)]
    for offset in FOX_EDGE_OFFSETS:
        branches.append(functools.partial(step, first, (None, init), min_offset=offset))
    carry = lax.switch(jnp.where(first >= i, 0, edge + 1), branches)
    carry = lax.fori_loop(first + 1, i, step, carry)
    _, states = step(i, carry, diagonal=True)
    out = jnp.concatenate([acc[:HEAD_DIM] / acc[HEAD_DIM:HEAD_DIM + 1] for _, acc in states],
                          axis=1)
    o_ref[0, 0] = (out * g_ref[0, 0].astype(F32)).astype(BF16)


def _fox_plan(stats, batch, seq):
    nt, nq = seq // ROW_TILE, seq // FOX_T
    sub_t, sub_q = ROW_TILE // FOX_UQ, FOX_T // FOX_UQ
    st = stats.reshape(batch, nt, N_HEADS, LANES)
    q_sq = st[..., 0].reshape(batch, nq, nt // nq, N_HEADS).max(axis=2)
    k_sq = st[..., 1].max(axis=1, keepdims=True)
    bound = jnp.sqrt(q_sq * k_sq) * 1.01 + 1.0
    edges = st[..., 2:2 + 2 * sub_t].reshape(batch, nt, N_HEADS, sub_t, 2)
    edges = edges.transpose(0, 1, 3, 2, 4).reshape(batch, nq, sub_q, N_HEADS, 2)
    c_first, c_last = edges[..., 0], edges[..., 1]
    gap = c_first[:, :, :, None, None, :] - c_last[:, None, None, :, :, :]
    skip = (2.0 * bound[:, :, None, None, None, :] + gap) < -FOX_SKIP_BITS
    steps = jnp.arange(nq, dtype=jnp.int32)
    step_skip = jnp.all(skip, axis=(2, 4))
    first = jnp.min(jnp.where(step_skip, nq, steps[None, None, :, None]), axis=2)
    first = jnp.minimum(first, steps[None, :, None])
    other_step = steps[None, None, None, :, None, None] != first[:, :, None, None, None, :]
    at_first = jnp.all(skip | other_step, axis=3)
    offset = (jnp.arange(sub_q)[None, :] - jnp.arange(sub_q)[:, None])
    variant = jnp.zeros_like(first)
    for v, min_offset in enumerate(FOX_EDGE_OFFSETS):
        if min_offset is not None:
            dropped = (offset < min_offset)[None, None, :, :, None]
            ok = jnp.all(at_first | ~dropped, axis=(2, 3))
            variant = jnp.where(ok, v, variant)
    plan = first * len(FOX_EDGE_OFFSETS) + variant
    return plan.transpose(0, 2, 1).reshape(-1).astype(jnp.int32)


def _fox(first, qt, k, vt, gt):
    b, _, _, s = qt.shape
    q_spec = pl.BlockSpec((1, 1, 2 * HEAD_DIM, FOX_T), lambda bi, h, i, f: (bi, h, 0, i))
    k_spec = pl.BlockSpec((1, 1, s, 2 * HEAD_DIM), lambda bi, h, i, f: (bi, h, 0, 0))
    v_spec = pl.BlockSpec((1, 1, s // ROW_TILE, FOX_VROWS, ROW_TILE),
                          lambda bi, h, i, f: (bi, h, 0, 0, 0))
    o_spec = pl.BlockSpec((1, 1, HEAD_DIM, FOX_T), lambda bi, h, i, f: (bi, h, 0, i))
    return pl.pallas_call(
        _fox_kernel,
        out_shape=jax.ShapeDtypeStruct((b, N_HEADS, HEAD_DIM, s), BF16),
        grid_spec=pltpu.PrefetchScalarGridSpec(
            num_scalar_prefetch=1,
            grid=(b, N_HEADS, s // FOX_T),
            in_specs=[q_spec, k_spec, v_spec, o_spec],
            out_specs=o_spec),
        compiler_params=pltpu.CompilerParams(
            dimension_semantics=("arbitrary",) * 3, vmem_limit_bytes=VMEM_LIMIT_BYTES),
        name="fox_attention",
    )(first, qt, k, vt, gt)


def _dil_kernel(q_ref, k_ref, v_ref, g_ref, o_ref, q4_s, k4_s, v4_s, acc_s, m_s, l_s, *, seq):
    tq, tw, sup = DIL_TQ, DIL_TW, DIL_SUPER
    part = seq // DIL_SPLIT

    def regroup(c, carry):
        for src, dst in ((q_ref, q4_s), (k_ref, k4_s), (v_ref, v4_s)):
            for r in range(DIL_SPLIT):
                rows = src[0, pl.ds(c * DIL_SPLIT * DIL_COPY + r, DIL_COPY, stride=DIL_SPLIT), :]
                dst[pl.ds(r * part + c * DIL_COPY, DIL_COPY), :] = rows
        return carry

    lax.fori_loop(0, part // DIL_COPY, regroup, 0)

    def stream_rows(nat_ref, split_ref, d, r, pos, n):
        if d % DIL_SPLIT:
            x = nat_ref[0, pl.ds(d * pos + r, n, stride=d), :]
        else:
            sd = d // DIL_SPLIT
            start = (r % DIL_SPLIT) * part + sd * pos + r // DIL_SPLIT
            x = split_ref[pl.ds(start, n, stride=sd), :]
        return x.astype(BF16)

    lane = lax.broadcasted_iota(jnp.int32, (1, LANES), 1)
    low = lane < HEAD_DIM
    rel = (lax.broadcasted_iota(jnp.int32, (tq, tw), 0)
           - lax.broadcasted_iota(jnp.int32, (tq, tw), 1))

    def band_bias(q_minus_w):
        dist = rel + q_minus_w
        return jnp.where(dist >= 0, jnp.where(dist <= N_BACK, 0.0, NEG), NEG)

    bias_inner = band_bias(N_BACK)
    bias_start = band_bias(0)

    def issue(d, base, idx, nblk):
        r = idx // nblk
        jb = idx % nblk
        qs = base // d + jb * tq
        ws = jnp.maximum(qs - N_BACK, 0)
        q = stream_rows(q_ref, q4_s, d, r, qs, tq)
        kw = stream_rows(k_ref, k4_s, d, r, ws, tw)
        zero = jnp.zeros_like(q)
        q2 = jnp.concatenate([jnp.where(low, q, zero), jnp.where(low, zero, q)], axis=0)
        return _dot_nt(q2, kw), r, qs, ws, d * (qs - base // d) + r

    def consume(g, d, s, r, qs, ws, out_row):
        vw = stream_rows(v_ref, v4_s, d, r, ws, tw)
        bias = jnp.where(qs == ws, bias_start, bias_inner)
        ps, ms = [], []
        for hh in range(2):
            sm = s[hh * tq:(hh + 1) * tq] + bias
            m = jnp.max(sm, axis=-1, keepdims=True)
            ps.append(jnp.exp2((sm - m).astype(BF16)))
            ms.append(jnp.broadcast_to(m, (tq, LANES)))
        vw1 = jnp.concatenate([vw, jnp.ones_like(vw)], axis=1)
        acc = _dot(jnp.concatenate(ps, axis=0), vw1)
        rows = pl.ds(out_row, tq, stride=d)
        acc_s[g, rows, :] = jnp.where(low, acc[:tq, :LANES], acc[tq:, :LANES])
        m_s[g, rows, :] = jnp.where(low, ms[0], ms[1])
        l_s[g, rows, :] = jnp.where(low, acc[:tq, LANES:], acc[tq:, LANES:])

    def superblock(sb, carry):
        base = pl.multiple_of(sb * sup, sup)
        for g, (_, d) in enumerate(DILATED_PATTERNS):
            nblk = sup // (d * tq)

            def body(it, c, g=g, d=d, nblk=nblk):
                first = it * DIL_GROUP
                pending = {n: issue(d, base, first + n, nblk) for n in range(DIL_AHEAD)}
                for n in range(DIL_GROUP):
                    if n + DIL_AHEAD < DIL_GROUP:
                        pending[n + DIL_AHEAD] = issue(d, base, first + n + DIL_AHEAD, nblk)
                    consume(g, d, *pending.pop(n))
                return c

            lax.fori_loop(0, d * nblk // DIL_GROUP, body, 0)

        rows = pl.ds(base, sup)
        m1, m2, m3 = m_s[0], m_s[1], m_s[2]
        mx = jnp.maximum(jnp.maximum(m1, m2), m3)
        e1, e2, e3 = jnp.exp2(m1 - mx), jnp.exp2(m2 - mx), jnp.exp2(m3 - mx)
        num = e1 * acc_s[0] + e2 * acc_s[1] + e3 * acc_s[2]
        den = e1 * l_s[0] + e2 * l_s[1] + e3 * l_s[2]
        o_ref[0, rows, :] = (num / den * g_ref[0, rows, :].astype(F32)).astype(BF16)
        return carry

    lax.fori_loop(0, seq // sup, superblock, 0)


def _dilated(qb, kb, vb, gb):
    b, s, _ = qb.shape
    spec = pl.BlockSpec((1, s, LANES), lambda bi, p: (bi, 0, p))
    regrouped = pltpu.VMEM((s, LANES), F32)
    scratch = pltpu.VMEM((len(DILATED_PATTERNS), DIL_SUPER, LANES), F32)
    return pl.pallas_call(
        functools.partial(_dil_kernel, seq=s),
        out_shape=jax.ShapeDtypeStruct((b, s, D_BRANCH), BF16),
        grid=(b, D_BRANCH // LANES),
        in_specs=[spec, spec, spec, spec],
        out_specs=spec,
        scratch_shapes=[regrouped, regrouped, regrouped, scratch, scratch, scratch],
        compiler_params=pltpu.CompilerParams(
            dimension_semantics=("arbitrary",) * 2, vmem_limit_bytes=VMEM_LIMIT_BYTES),
        name="dilated_attn",
    )(qb, kb, vb, gb)


def _post_kernel(h_ref, oat_ref, ob_ref, p_ref, woa_ref, wob_ref, g2_ref, wg_ref, wple_ref,
                 out_ref):
    h1 = h_ref[...] + _dot_tn(oat_ref[0], woa_ref[...]) + _dot(ob_ref[...], wob_ref[...])
    u2 = _rms_norm_rows(h1, g2_ref[...]).astype(BF16)
    gate = jax.nn.sigmoid(_dot(u2, wg_ref[...]))
    ple = _dot(p_ref[...].astype(BF16), wple_ref[...])
    out_ref[...] = h1 + ple * gate


def _post(h2, oat, ob, p2, batch, seq, woa, wob, g2, wg, wple):
    nt = seq // ROW_TILE
    row = lambda w: pl.BlockSpec((ROW_TILE, w), lambda b, t: (b * nt + t, 0))
    full = lambda a: pl.BlockSpec(a.shape, lambda b, t: (0,) * a.ndim)
    return pl.pallas_call(
        _post_kernel,
        out_shape=jax.ShapeDtypeStruct((batch * seq, D_MODEL), F32),
        grid=(batch, nt),
        in_specs=[row(D_MODEL), pl.BlockSpec((1, D_BRANCH, ROW_TILE), lambda b, t: (b, 0, t)),
                  row(D_BRANCH), row(PLE_DIM),
                  full(woa), full(wob), full(g2), full(wg), full(wple)],
        out_specs=row(D_MODEL),
        compiler_params=pltpu.CompilerParams(
            dimension_semantics=("arbitrary", "arbitrary"), vmem_limit_bytes=VMEM_LIMIT_BYTES),
        name="out_stage",
    )(h2, oat, ob, p2, woa, wob, g2, wg, wple)


def _layer(h2, p2, pos_row, invf, tri, batch, seq, norm_g, w_in, b_f, qk_g, w_out, w_ple,
           ple_norm_g, w_ple_gate):
    d = D_BRANCH
    cuts = np.cumsum([0, d, d, d, d, N_HEADS, d, d, d, d])
    qa_w, ka_w, va_w, ga_w, f_w, qb_w, kb_w, vb_w, gb_w = (
        w_in[:, cuts[i]:cuts[i + 1]] for i in range(9))
    f_w = jnp.pad(f_w, ((0, 0), (0, F_ROWS - N_HEADS)))
    wt = jnp.concatenate([qa_w, ka_w, va_w, ga_w, qb_w, kb_w, f_w], axis=1).T.astype(BF16)
    wn = jnp.concatenate([vb_w, gb_w], axis=1).astype(BF16)
    bf_col = jnp.pad(b_f.astype(F32), (0, F_ROWS - N_HEADS)).reshape(F_ROWS, 1)
    scale = HEAD_DIM ** -0.5
    gcol = jnp.concatenate([jnp.tile(qk_g[0] * (scale * LOG2E), N_HEADS), jnp.tile(qk_g[1], N_HEADS),
                            jnp.tile(qk_g[2] * (scale * LOG2E), N_HEADS), jnp.tile(qk_g[3], N_HEADS)]
                           ).astype(F32).reshape(4 * d, 1)

    qt, k, vt, gt, qb, kb, vb, gb, stats = _proj(
        h2, pos_row, batch, seq, norm_g.reshape(1, D_MODEL), wt, wn, gcol, bf_col, invf, tri)

    oat = _fox(_fox_plan(stats, batch, seq), qt, k, vt, gt).reshape(batch, d, seq)
    as3 = lambda a: a.reshape(batch, seq, d)
    ob = _dilated(as3(qb), as3(kb), as3(vb), as3(gb)).reshape(batch * seq, d)

    w_out_b = w_out.astype(BF16)
    return _post(h2, oat, ob, p2, batch, seq, w_out_b[:d], w_out_b[d:],
                 ple_norm_g.reshape(1, D_MODEL), w_ple_gate.astype(BF16), w_ple.astype(BF16))


def kernel(x, p, positions, norm_g, w_in, b_f, qk_norm_g, w_out, w_ple, ple_norm_g, w_ple_gate):
    batch, seq, _ = x.shape
    depth = p.shape[0]
    assert all(w // dil == N_BACK for w, dil in DILATED_PATTERNS)
    assert seq % DIL_SUPER == 0 and seq % ROW_TILE == 0 and FOX_T % FOX_UQ == 0
    pos_row = positions.reshape(1, batch * seq)
    inv_freq = ROPE_THETA ** (-jnp.arange(ROPE_HALF, dtype=F32) / ROPE_HALF)
    invf = inv_freq.reshape(ROPE_HALF, 1)
    idx = np.arange(ROW_TILE)
    tri = jnp.asarray(idx[:, None] <= idx[None, :], BF16)
    h2 = x.reshape(batch * seq, D_MODEL)
    for i in range(depth):
        h2 = _layer(h2, p[i].reshape(batch * seq, PLE_DIM), pos_row, invf, tri, batch, seq,
                    norm_g[i], w_in[i], b_f[i], qk_norm_g[i], w_out[i], w_ple[i],
                    ple_norm_g[i], w_ple_gate[i])
    return h2.reshape(batch, seq, D_MODEL)
```

```python
import functools

import numpy as np
import jax
import jax.numpy as jnp
from jax import lax
from jax.experimental import pallas as pl
from jax.experimental.pallas import tpu as pltpu

D_MODEL = 1024
HEAD_DIM = 64
N_HEADS = 8
D_BRANCH = N_HEADS * HEAD_DIM
PLE_DIM = 256
ROPE_THETA = 500000.0
ROPE_DIM = HEAD_DIM // 4
ROPE_HALF = ROPE_DIM // 2
DILATED_PATTERNS = ((128, 1), (512, 4), (2048, 16))
N_BACK = 128
EPS = 1e-6
NEG = -1e30

LANES = 128
SUBLANES = 8
BF16_ROWS = 16
VMEM_LIMIT_BYTES = 56 * 1024 * 1024

ROW_TILE = 512
POST_TILE = 1024
FOX_T = 1024
FOX_UK = 256
FOX_UQ = 256
FOX_AHEAD = 8
FOX_VROWS = HEAD_DIM + BF16_ROWS
FOX_SKIP_BITS = 150.0
FOX_EDGE_OFFSETS = (None, 0, 1, 2)
LOG2E = 1.4426950408889634
DIL_TQ = N_BACK
DIL_TW = DIL_TQ + N_BACK
DIL_SUPER = max(d for _, d in DILATED_PATTERNS) * DIL_TQ
DIL_SPLIT = 4
DIL_COPY = 512
DIL_GROUP = 8
DIL_AHEAD = 4
F_ROWS = BF16_ROWS

F32 = jnp.float32
BF16 = jnp.bfloat16


def _dot(a, b):
    return jnp.dot(a, b, preferred_element_type=F32)


def _dot_nt(a, b):
    return lax.dot_general(a, b, (((1,), (1,)), ((), ())), preferred_element_type=F32)


def _dot_tn(a, b):
    return lax.dot_general(a, b, (((0,), (0,)), ((), ())), preferred_element_type=F32)


def _split3(x):
    hi = x.astype(BF16).astype(F32)
    r = x - hi
    mid = r.astype(BF16).astype(F32)
    lo = (r - mid).astype(BF16).astype(F32)
    return hi, mid, lo


def _rms_norm_rows(h, g_row):
    ms = jnp.mean(h * h, axis=-1, keepdims=True)
    return h * lax.rsqrt(ms + EPS) * g_row


def _proj_kernel(h_ref, pos_ref, g1_ref, wt_ref, wn_ref, gcol_ref, bf_ref, invf_ref, tri_ref,
                 qt_ref, k_ref, vt_ref, gt_ref, qb_ref, kb_ref, vb_ref, gb_ref, stats_ref,
                 carry_ref):
    t = pl.program_id(1)
    tm = ROW_TILE
    d = D_BRANCH
    u = _rms_norm_rows(h_ref[...], g1_ref[...]).astype(BF16)

    def feat_major(group):
        return _dot_nt(wt_ref[group * d:(group + 1) * d, :], u)

    zn = _dot(u, wn_ref[...])
    f = _dot_nt(wt_ref[6 * d:6 * d + F_ROWS, :], u) + bf_ref[...]
    z_qa = feat_major(0)

    vb_ref[...] = zn[:, :d]
    gate_b = zn[:, d:]
    gb_ref[...] = (gate_b * jax.nn.sigmoid(gate_b)).astype(BF16)
    z_ka = feat_major(1)

    def head_norm(z, group):
        z3 = z.reshape(N_HEADS, HEAD_DIM, tm)
        ss = jnp.sum(z3 * z3, axis=1, keepdims=True)
        y3 = z3 * lax.rsqrt(ss * (1.0 / HEAD_DIM) + EPS)
        return y3.reshape(d, tm) * gcol_ref[group * d:(group + 1) * d, :]

    ang = invf_ref[...] * pos_ref[...].astype(F32)
    cos = jnp.cos(ang)
    sin = jnp.sin(ang)

    def rope(y):
        y3 = y.reshape(N_HEADS, HEAD_DIM, tm)
        x1 = y3[:, :ROPE_HALF, :]
        x2 = y3[:, ROPE_HALF:ROPE_DIM, :]
        rot = [x1 * cos - x2 * sin, x2 * cos + x1 * sin, y3[:, ROPE_DIM:, :]]
        return jnp.concatenate(rot, axis=1).reshape(d, tm)

    lf = jnp.minimum(f, 0.0) - jnp.log1p(jnp.exp(-jnp.abs(f)))
    stacked = jnp.concatenate([x.astype(BF16) for x in _split3(lf)], axis=0)
    cs = _dot(stacked, tri_ref[...])
    carry = jnp.where(t == 0, 0.0, carry_ref[:, 0:1])
    c = cs[:F_ROWS] + cs[F_ROWS:2 * F_ROWS] + cs[2 * F_ROWS:] + carry
    carry_ref[...] = jnp.broadcast_to(c[:, tm - 1:tm], (F_ROWS, LANES))
    c2 = c * LOG2E

    qa = head_norm(z_qa, 0)
    z_va = feat_major(2)
    ka = head_norm(z_ka, 1)
    z_ga = feat_major(3)
    row = lax.broadcasted_iota(jnp.int32, (SUBLANES, tm), 0)
    pad = jnp.zeros((HEAD_DIM - SUBLANES, tm), F32)
    for hd in range(N_HEADS):
        c_hi, c_mid, c_lo = _split3(c2[hd:hd + 1, :])
        q_aug = jnp.where(row == 0, c_hi, jnp.where(row == 1, c_mid, jnp.where(
            row == 2, c_lo, jnp.where(row < 6, 1.0, 0.0))))
        k_aug = jnp.where(row < 3, 1.0, jnp.where(row == 3, -c_hi, jnp.where(
            row == 4, -c_mid, jnp.where(row == 5, -c_lo, 0.0))))
        rows = slice(hd * HEAD_DIM, (hd + 1) * HEAD_DIM)
        qt_ref[0, hd] = jnp.concatenate([qa[rows], q_aug, pad], axis=0).astype(BF16)
        k_ref[0, hd] = jnp.concatenate([ka[rows], k_aug, pad], axis=0).T.astype(BF16)

    ones_row = jnp.where(lax.broadcasted_iota(jnp.int32, (N_HEADS, BF16_ROWS, tm), 1) == 0,
                         1.0, 0.0)
    z_qb = feat_major(4)
    va = z_va.reshape(N_HEADS, HEAD_DIM, tm)
    vt_ref[0, :, 0] = jnp.concatenate([va, ones_row], axis=1).astype(BF16)
    z_kb = feat_major(5)
    gt_ref[0] = (z_ga * jax.nn.sigmoid(z_ga)).reshape(N_HEADS, HEAD_DIM, tm).astype(BF16)

    qb_ref[...] = rope(head_norm(z_qb, 2)).T
    kb_ref[...] = rope(head_norm(z_kb, 3)).T

    def max_sq_norm(y):
        y3 = y.reshape(N_HEADS, HEAD_DIM, tm)
        return jnp.max(jnp.sum(y3 * y3, axis=1), axis=-1, keepdims=True)

    lane = lax.broadcasted_iota(jnp.int32, (N_HEADS, LANES), 1)
    stats = jnp.where(lane == 0, max_sq_norm(qa), jnp.where(lane == 1, max_sq_norm(ka), 0.0))
    for sub in range(tm // FOX_UQ):
        lo = sub * FOX_UQ
        hi = lo + FOX_UQ - 1
        stats = jnp.where(lane == 2 + 2 * sub, c2[:N_HEADS, lo:lo + 1],
                          jnp.where(lane == 3 + 2 * sub, c2[:N_HEADS, hi:hi + 1], stats))
    stats_ref[0] = stats


def _proj(h2, pos_row, batch, seq, g1, wt, wn, gcol, bf_col, invf, tri):
    nt = seq // ROW_TILE
    rows = batch * seq
    d = D_BRANCH
    row = lambda w: pl.BlockSpec((ROW_TILE, w), lambda b, t: (b * nt + t, 0))
    full = lambda a: pl.BlockSpec(a.shape, lambda b, t: (0,) * a.ndim)
    out_shape = (
        jax.ShapeDtypeStruct((batch, N_HEADS, 2 * HEAD_DIM, seq), BF16),
        jax.ShapeDtypeStruct((batch, N_HEADS, seq, 2 * HEAD_DIM), BF16),
        jax.ShapeDtypeStruct((batch, N_HEADS, nt, FOX_VROWS, ROW_TILE), BF16),
        jax.ShapeDtypeStruct((batch, N_HEADS, HEAD_DIM, seq), BF16),
        jax.ShapeDtypeStruct((rows, d), F32),
        jax.ShapeDtypeStruct((rows, d), F32),
        jax.ShapeDtypeStruct((rows, d), F32),
        jax.ShapeDtypeStruct((rows, d), BF16),
        jax.ShapeDtypeStruct((batch * nt, N_HEADS, LANES), F32),
    )
    out_specs = (
        pl.BlockSpec((1, N_HEADS, 2 * HEAD_DIM, ROW_TILE), lambda b, t: (b, 0, 0, t)),
        pl.BlockSpec((1, N_HEADS, ROW_TILE, 2 * HEAD_DIM), lambda b, t: (b, 0, t, 0)),
        pl.BlockSpec((1, N_HEADS, 1, FOX_VROWS, ROW_TILE), lambda b, t: (b, 0, t, 0, 0)),
        pl.BlockSpec((1, N_HEADS, HEAD_DIM, ROW_TILE), lambda b, t: (b, 0, 0, t)),
        row(d), row(d), row(d), row(d),
        pl.BlockSpec((1, N_HEADS, LANES), lambda b, t: (b * nt + t, 0, 0)),
    )
    return pl.pallas_call(
        _proj_kernel,
        out_shape=out_shape,
        grid=(batch, nt),
        in_specs=[row(D_MODEL), pl.BlockSpec((1, ROW_TILE), lambda b, t: (0, b * nt + t)),
                  full(g1), full(wt), full(wn), full(gcol), full(bf_col), full(invf), full(tri)],
        out_specs=out_specs,
        scratch_shapes=[pltpu.VMEM((F_ROWS, LANES), F32)],
        compiler_params=pltpu.CompilerParams(
            dimension_semantics=("arbitrary", "arbitrary"), vmem_limit_bytes=VMEM_LIMIT_BYTES),
        name="in_proj",
    )(h2, pos_row, g1, wt, wn, gcol, bf_col, invf, tri)


def _fox_kernel(plan_ref, q_ref, k_ref, v_ref, g_ref, o_ref):
    i = pl.program_id(2)
    plan = plan_ref[(pl.program_id(0) * N_HEADS + pl.program_id(1)) * pl.num_programs(2) + i]
    first = plan // len(FOX_EDGE_OFFSETS)
    edge = plan % len(FOX_EDGE_OFFSETS)
    uk, uq = FOX_UK, FOX_UQ
    n_q = FOX_T // uq
    q = [q_ref[0, 0, :, a * uq:(a + 1) * uq] for a in range(n_q)]
    rel = (lax.broadcasted_iota(jnp.int32, (uk, uq), 0)
           - lax.broadcasted_iota(jnp.int32, (uk, uq), 1))

    def scores(j, ks, a):
        start = pl.multiple_of(j * FOX_T, FOX_T)
        return _dot(k_ref[0, 0, pl.ds(start + ks * uk, uk), :], q[a])

    def consume(s, v_sub, state, max_rel):
        m, acc = state
        if max_rel is not None:
            s = jnp.where(rel <= max_rel, s, NEG)
        m_new = jnp.maximum(m, jnp.max(s, axis=0, keepdims=True))
        alpha = jnp.exp2(m - m_new)
        p = jnp.exp2((s - m_new).astype(BF16))
        acc = alpha * acc + _dot(v_sub, p)
        return m_new, acc

    def step(j, carry, diagonal=False, min_offset=None):
        s_first, states = carry
        states = list(states)
        units = []
        for ks in range(FOX_T // uk):
            for a in range(n_q):
                max_rel = None
                if min_offset is not None and ks - a < min_offset:
                    continue
                if diagonal:
                    if ks * uk >= (a + 1) * uq:
                        continue
                    if (ks + 1) * uk - 1 > a * uq:
                        max_rel = a * uq - ks * uk
                units.append((ks, a, max_rel))
        pending = {0: scores(j, units[0][0], units[0][1]) if s_first is None else s_first}
        queue = list(range(1, len(units))) + ([] if diagonal else ["next"])
        result = {}

        def issue(item):
            if item == "next":
                result["next"] = scores(j + 1, 0, 0)
            else:
                pending[item] = scores(j, units[item][0], units[item][1])

        for item in queue[:FOX_AHEAD - 1]:
            issue(item)
        for n, (ks, a, max_rel) in enumerate(units):
            if n + FOX_AHEAD - 1 < len(queue):
                issue(queue[n + FOX_AHEAD - 1])
            chunk, off = divmod(ks * uk, ROW_TILE)
            v_sub = v_ref[0, 0, j * (FOX_T // ROW_TILE) + chunk, :, off:off + uk]
            states[a] = consume(pending.pop(n), v_sub, states[a], max_rel)
        return result.get("next"), tuple(states)

    init = tuple((jnp.full((1, uq), NEG, F32), jnp.zeros((FOX_VROWS, uq), F32))
                 for _ in range(n_q))
    branches = [lambda: (scores(i, 0, 0), init)]
    for offset in FOX_EDGE_OFFSETS:
        branches.append(functools.partial(step, first, (None, init), min_offset=offset))
    carry = lax.switch(jnp.where(first >= i, 0, edge + 1), branches)
    carry = lax.fori_loop(first + 1, i, step, carry)
    _, states = step(i, carry, diagonal=True)
    out = jnp.concatenate([acc[:HEAD_DIM] / acc[HEAD_DIM:HEAD_DIM + 1] for _, acc in states],
                          axis=1)
    o_ref[0, 0] = (out * g_ref[0, 0].astype(F32)).astype(BF16)


def _fox_plan(stats, batch, seq):
    nt, nq = seq // ROW_TILE, seq // FOX_T
    sub_t, sub_q = ROW_TILE // FOX_UQ, FOX_T // FOX_UQ
    st = stats.reshape(batch, nt, N_HEADS, LANES)
    q_sq = st[..., 0].reshape(batch, nq, nt // nq, N_HEADS).max(axis=2)
    k_sq = st[..., 1].max(axis=1, keepdims=True)
    bound = jnp.sqrt(q_sq * k_sq) * 1.01 + 1.0
    edges = st[..., 2:2 + 2 * sub_t].reshape(batch, nt, N_HEADS, sub_t, 2)
    edges = edges.transpose(0, 1, 3, 2, 4).reshape(batch, nq, sub_q, N_HEADS, 2)
    c_first, c_last = edges[..., 0], edges[..., 1]
    gap = c_first[:, :, :, None, None, :] - c_last[:, None, None, :, :, :]
    skip = (2.0 * bound[:, :, None, None, None, :] + gap) < -FOX_SKIP_BITS
    steps = jnp.arange(nq, dtype=jnp.int32)
    step_skip = jnp.all(skip, axis=(2, 4))
    first = jnp.min(jnp.where(step_skip, nq, steps[None, None, :, None]), axis=2)
    first = jnp.minimum(first, steps[None, :, None])
    other_step = steps[None, None, None, :, None, None] != first[:, :, None, None, None, :]
    at_first = jnp.all(skip | other_step, axis=3)
    offset = (jnp.arange(sub_q)[None, :] - jnp.arange(sub_q)[:, None])
    variant = jnp.zeros_like(first)
    for v, min_offset in enumerate(FOX_EDGE_OFFSETS):
        if min_offset is not None:
            dropped = (offset < min_offset)[None, None, :, :, None]
            ok = jnp.all(at_first | ~dropped, axis=(2, 3))
            variant = jnp.where(ok, v, variant)
    plan = first * len(FOX_EDGE_OFFSETS) + variant
    return plan.transpose(0, 2, 1).reshape(-1).astype(jnp.int32)


def _fox(first, qt, k, vt, gt):
    b, _, _, s = qt.shape
    q_spec = pl.BlockSpec((1, 1, 2 * HEAD_DIM, FOX_T), lambda bi, h, i, f: (bi, h, 0, i))
    k_spec = pl.BlockSpec((1, 1, s, 2 * HEAD_DIM), lambda bi, h, i, f: (bi, h, 0, 0))
    v_spec = pl.BlockSpec((1, 1, s // ROW_TILE, FOX_VROWS, ROW_TILE),
                          lambda bi, h, i, f: (bi, h, 0, 0, 0))
    o_spec = pl.BlockSpec((1, 1, HEAD_DIM, FOX_T), lambda bi, h, i, f: (bi, h, 0, i))
    return pl.pallas_call(
        _fox_kernel,
        out_shape=jax.ShapeDtypeStruct((b, N_HEADS, HEAD_DIM, s), BF16),
        grid_spec=pltpu.PrefetchScalarGridSpec(
            num_scalar_prefetch=1,
            grid=(b, N_HEADS, s // FOX_T),
            in_specs=[q_spec, k_spec, v_spec, o_spec],
            out_specs=o_spec),
        compiler_params=pltpu.CompilerParams(
            dimension_semantics=("arbitrary",) * 3, vmem_limit_bytes=VMEM_LIMIT_BYTES),
        name="fox_attention",
    )(first, qt, k, vt, gt)


def _dil_kernel(q_ref, k_ref, v_ref, g_ref, o_ref, q4_s, k4_s, v4_s, acc_s, m_s, l_s, *, seq):
    tq, tw, sup = DIL_TQ, DIL_TW, DIL_SUPER
    part = seq // DIL_SPLIT

    def regroup(c, carry):
        for src, dst in ((q_ref, q4_s), (k_ref, k4_s), (v_ref, v4_s)):
            for r in range(DIL_SPLIT):
                rows = src[0, pl.ds(c * DIL_SPLIT * DIL_COPY + r, DIL_COPY, stride=DIL_SPLIT), :]
                dst[pl.ds(r * part + c * DIL_COPY, DIL_COPY), :] = rows
        return carry

    lax.fori_loop(0, part // DIL_COPY, regroup, 0)

    def stream_rows(nat_ref, split_ref, d, r, pos, n):
        if d % DIL_SPLIT:
            x = nat_ref[0, pl.ds(d * pos + r, n, stride=d), :]
        else:
            sd = d // DIL_SPLIT
            start = (r % DIL_SPLIT) * part + sd * pos + r // DIL_SPLIT
            x = split_ref[pl.ds(start, n, stride=sd), :]
        return x.astype(BF16)

    lane = lax.broadcasted_iota(jnp.int32, (1, LANES), 1)
    low = lane < HEAD_DIM
    rel = (lax.broadcasted_iota(jnp.int32, (tq, tw), 0)
           - lax.broadcasted_iota(jnp.int32, (tq, tw), 1))

    def band_bias(q_minus_w):
        dist = rel + q_minus_w
        return jnp.where(dist >= 0, jnp.where(dist <= N_BACK, 0.0, NEG), NEG)

    bias_inner = band_bias(N_BACK)
    bias_start = band_bias(0)

    def issue(d, base, idx, nblk):
        r = idx // nblk
        jb = idx % nblk
        qs = base // d + jb * tq
        ws = jnp.maximum(qs - N_BACK, 0)
        q = stream_rows(q_ref, q4_s, d, r, qs, tq)
        kw = stream_rows(k_ref, k4_s, d, r, ws, tw)
        zero = jnp.zeros_like(q)
        q2 = jnp.concatenate([jnp.where(low, q, zero), jnp.where(low, zero, q)], axis=0)
        return _dot_nt(q2, kw), r, qs, ws, d * (qs - base // d) + r

    def consume(g, d, s, r, qs, ws, out_row):
        vw = stream_rows(v_ref, v4_s, d, r, ws, tw)
        bias = jnp.where(qs == ws, bias_start, bias_inner)
        ps, ms = [], []
        for hh in range(2):
            sm = s[hh * tq:(hh + 1) * tq] + bias
            m = jnp.max(sm, axis=-1, keepdims=True)
            ps.append(jnp.exp2((sm - m).astype(BF16)))
            ms.append(jnp.broadcast_to(m, (tq, LANES)))
        vw1 = jnp.concatenate([vw, jnp.ones_like(vw)], axis=1)
        acc = _dot(jnp.concatenate(ps, axis=0), vw1)
        rows = pl.ds(out_row, tq, stride=d)
        acc_s[g, rows, :] = jnp.where(low, acc[:tq, :LANES], acc[tq:, :LANES])
        m_s[g, rows, :] = jnp.where(low, ms[0], ms[1])
        l_s[g, rows, :] = jnp.where(low, acc[:tq, LANES:], acc[tq:, LANES:])

    def superblock(sb, carry):
        base = pl.multiple_of(sb * sup, sup)
        for g, (_, d) in enumerate(DILATED_PATTERNS):
            nblk = sup // (d * tq)

            def body(it, c, g=g, d=d, nblk=nblk):
                first = it * DIL_GROUP
                pending = {n: issue(d, base, first + n, nblk) for n in range(DIL_AHEAD)}
                for n in range(DIL_GROUP):
                    if n + DIL_AHEAD < DIL_GROUP:
                        pending[n + DIL_AHEAD] = issue(d, base, first + n + DIL_AHEAD, nblk)
                    consume(g, d, *pending.pop(n))
                return c

            lax.fori_loop(0, d * nblk // DIL_GROUP, body, 0)

        rows = pl.ds(base, sup)
        m1, m2, m3 = m_s[0], m_s[1], m_s[2]
        mx = jnp.maximum(jnp.maximum(m1, m2), m3)
        e1, e2, e3 = jnp.exp2(m1 - mx), jnp.exp2(m2 - mx), jnp.exp2(m3 - mx)
        num = e1 * acc_s[0] + e2 * acc_s[1] + e3 * acc_s[2]
        den = e1 * l_s[0] + e2 * l_s[1] + e3 * l_s[2]
        o_ref[0, rows, :] = (num / den * g_ref[0, rows, :].astype(F32)).astype(BF16)
        return carry

    lax.fori_loop(0, seq // sup, superblock, 0)


def _dilated(qb, kb, vb, gb):
    b, s, _ = qb.shape
    spec = pl.BlockSpec((1, s, LANES), lambda bi, p: (bi, 0, p))
    regrouped = pltpu.VMEM((s, LANES), F32)
    scratch = pltpu.VMEM((len(DILATED_PATTERNS), DIL_SUPER, LANES), F32)
    return pl.pallas_call(
        functools.partial(_dil_kernel, seq=s),
        out_shape=jax.ShapeDtypeStruct((b, s, D_BRANCH), BF16),
        grid=(b, D_BRANCH // LANES),
        in_specs=[spec, spec, spec, spec],
        out_specs=spec,
        scratch_shapes=[regrouped, regrouped, regrouped, scratch, scratch, scratch],
        compiler_params=pltpu.CompilerParams(
            dimension_semantics=("arbitrary",) * 2, vmem_limit_bytes=VMEM_LIMIT_BYTES),
        name="dilated_attn",
    )(qb, kb, vb, gb)


def _post_kernel(h_ref, oat_ref, ob_ref, p_ref, woa_ref, wob_ref, g2_ref, wg_ref, wple_ref,
                 out_ref):
    h1 = h_ref[...] + _dot_tn(oat_ref[0], woa_ref[...]) + _dot(ob_ref[...], wob_ref[...])
    u2 = _rms_norm_rows(h1, g2_ref[...]).astype(BF16)
    gate = jax.nn.sigmoid(_dot(u2, wg_ref[...]))
    ple = _dot(p_ref[...].astype(BF16), wple_ref[...])
    out_ref[...] = h1 + ple * gate


def _post(h2, oat, ob, p2, batch, seq, woa, wob, g2, wg, wple):
    nt = seq // POST_TILE
    row = lambda w: pl.BlockSpec((POST_TILE, w), lambda b, t: (b * nt + t, 0))
    full = lambda a: pl.BlockSpec(a.shape, lambda b, t: (0,) * a.ndim)
    return pl.pallas_call(
        _post_kernel,
        out_shape=jax.ShapeDtypeStruct((batch * seq, D_MODEL), F32),
        grid=(batch, nt),
        in_specs=[row(D_MODEL), pl.BlockSpec((1, D_BRANCH, POST_TILE), lambda b, t: (b, 0, t)),
                  row(D_BRANCH), row(PLE_DIM),
                  full(woa), full(wob), full(g2), full(wg), full(wple)],
        out_specs=row(D_MODEL),
        compiler_params=pltpu.CompilerParams(
            dimension_semantics=("arbitrary", "arbitrary"), vmem_limit_bytes=VMEM_LIMIT_BYTES),
        name="out_stage",
    )(h2, oat, ob, p2, woa, wob, g2, wg, wple)


def _layer(h2, p2, pos_row, invf, tri, batch, seq, norm_g, w_in, b_f, qk_g, w_out, w_ple,
           ple_norm_g, w_ple_gate):
    d = D_BRANCH
    cuts = np.cumsum([0, d, d, d, d, N_HEADS, d, d, d, d])
    qa_w, ka_w, va_w, ga_w, f_w, qb_w, kb_w, vb_w, gb_w = (
        w_in[:, cuts[i]:cuts[i + 1]] for i in range(9))
    f_w = jnp.pad(f_w, ((0, 0), (0, F_ROWS - N_HEADS)))
    wt = jnp.concatenate([qa_w, ka_w, va_w, ga_w, qb_w, kb_w, f_w], axis=1).T.astype(BF16)
    wn = jnp.concatenate([vb_w, gb_w], axis=1).astype(BF16)
    bf_col = jnp.pad(b_f.astype(F32), (0, F_ROWS - N_HEADS)).reshape(F_ROWS, 1)
    scale = HEAD_DIM ** -0.5
    gcol = jnp.concatenate([jnp.tile(qk_g[0] * (scale * LOG2E), N_HEADS), jnp.tile(qk_g[1], N_HEADS),
                            jnp.tile(qk_g[2] * (scale * LOG2E), N_HEADS), jnp.tile(qk_g[3], N_HEADS)]
                           ).astype(F32).reshape(4 * d, 1)

    qt, k, vt, gt, qb, kb, vb, gb, stats = _proj(
        h2, pos_row, batch, seq, norm_g.reshape(1, D_MODEL), wt, wn, gcol, bf_col, invf, tri)

    oat = _fox(_fox_plan(stats, batch, seq), qt, k, vt, gt).reshape(batch, d, seq)
    as3 = lambda a: a.reshape(batch, seq, d)
    ob = _dilated(as3(qb), as3(kb), as3(vb), as3(gb)).reshape(batch * seq, d)

    w_out_b = w_out.astype(BF16)
    return _post(h2, oat, ob, p2, batch, seq, w_out_b[:d], w_out_b[d:],
                 ple_norm_g.reshape(1, D_MODEL), w_ple_gate.astype(BF16), w_ple.astype(BF16))


def kernel(x, p, positions, norm_g, w_in, b_f, qk_norm_g, w_out, w_ple, ple_norm_g, w_ple_gate):
    batch, seq, _ = x.shape
    depth = p.shape[0]
    assert all(w // dil == N_BACK for w, dil in DILATED_PATTERNS)
    assert seq % DIL_SUPER == 0 and seq % ROW_TILE == 0 and FOX_T % FOX_UQ == 0
    pos_row = positions.reshape(1, batch * seq)
    inv_freq = ROPE_THETA ** (-jnp.arange(ROPE_HALF, dtype=F32) / ROPE_HALF)
    invf = inv_freq.reshape(ROPE_HALF, 1)
    idx = np.arange(ROW_TILE)
    tri = jnp.asarray(idx[:, None] <= idx[None, :], BF16)
    h2 = x.reshape(batch * seq, D_MODEL)
    for i in range(depth):
        h2 = _layer(h2, p[i].reshape(batch * seq, PLE_DIM), pos_row, invf, tri, batch, seq,
                    norm_g[i], w_in[i], b_f[i], qk_norm_g[i], w_out[i], w_ple[i],
                    ple_norm_g[i], w_ple_gate[i])
    return h2.reshape(batch, seq, D_MODEL)
```

```python
import functools

import numpy as np
import jax
import jax.numpy as jnp
from jax import lax
from jax.experimental import pallas as pl
from jax.experimental.pallas import tpu as pltpu

D_MODEL = 1024
HEAD_DIM = 64
N_HEADS = 8
D_BRANCH = N_HEADS * HEAD_DIM
PLE_DIM = 256
ROPE_THETA = 500000.0
ROPE_DIM = HEAD_DIM // 4
ROPE_HALF = ROPE_DIM // 2
DILATED_PATTERNS = ((128, 1), (512, 4), (2048, 16))
N_BACK = 128
EPS = 1e-6
NEG = -1e30

LANES = 128
SUBLANES = 8
BF16_ROWS = 16
VMEM_LIMIT_BYTES = 56 * 1024 * 1024

ROW_TILE = 512
POST_TILE = 1024
FOX_T = 1024
FOX_UK = 256
FOX_UQ = 256
FOX_AHEAD = 8
FOX_VROWS = HEAD_DIM + BF16_ROWS
FOX_SKIP_BITS = 150.0
FOX_EDGE_OFFSETS = (None, 0, 1, 2)
LOG2E = 1.4426950408889634
DIL_TQ = N_BACK
DIL_TW = DIL_TQ + N_BACK
DIL_SUPER = max(d for _, d in DILATED_PATTERNS) * DIL_TQ
DIL_SPLIT = 4
DIL_COPY = 512
DIL_GROUP = 8
DIL_AHEAD = 4
F_ROWS = BF16_ROWS

F32 = jnp.float32
BF16 = jnp.bfloat16


def _dot(a, b):
    return jnp.dot(a, b, preferred_element_type=F32)


def _dot_nt(a, b):
    return lax.dot_general(a, b, (((1,), (1,)), ((), ())), preferred_element_type=F32)


def _dot_tn(a, b):
    return lax.dot_general(a, b, (((0,), (0,)), ((), ())), preferred_element_type=F32)


def _split3(x):
    hi = x.astype(BF16).astype(F32)
    r = x - hi
    mid = r.astype(BF16).astype(F32)
    lo = (r - mid).astype(BF16).astype(F32)
    return hi, mid, lo


def _rms_norm_rows(h, g_row):
    ms = jnp.mean(h * h, axis=-1, keepdims=True)
    return h * lax.rsqrt(ms + EPS) * g_row


def _proj_kernel(h_ref, pos_ref, g1_ref, wt_ref, wn_ref, gcol_ref, bf_ref, invf_ref, tri_ref,
                 qt_ref, k_ref, vt_ref, gt_ref, qb_ref, kb_ref, vb_ref, gb_ref, stats_ref,
                 carry_ref):
    t = pl.program_id(1)
    tm = ROW_TILE
    d = D_BRANCH
    u = _rms_norm_rows(h_ref[...], g1_ref[...]).astype(BF16)

    def feat_major(group):
        return _dot_nt(wt_ref[group * d:(group + 1) * d, :], u)

    zn = _dot(u, wn_ref[...])
    f = _dot_nt(wt_ref[6 * d:6 * d + F_ROWS, :], u) + bf_ref[...]
    z_qa = feat_major(0)

    vb_ref[...] = zn[:, :d]
    gate_b = zn[:, d:]
    gb_ref[...] = (gate_b * jax.nn.sigmoid(gate_b)).astype(BF16)
    z_ka = feat_major(1)

    def head_norm(z, group):
        z3 = z.reshape(N_HEADS, HEAD_DIM, tm)
        ss = jnp.sum(z3 * z3, axis=1, keepdims=True)
        y3 = z3 * lax.rsqrt(ss * (1.0 / HEAD_DIM) + EPS)
        return y3.reshape(d, tm) * gcol_ref[group * d:(group + 1) * d, :]

    ang = invf_ref[...] * pos_ref[...].astype(F32)
    cos = jnp.cos(ang)
    sin = jnp.sin(ang)

    def rope(y):
        y3 = y.reshape(N_HEADS, HEAD_DIM, tm)
        x1 = y3[:, :ROPE_HALF, :]
        x2 = y3[:, ROPE_HALF:ROPE_DIM, :]
        rot = [x1 * cos - x2 * sin, x2 * cos + x1 * sin, y3[:, ROPE_DIM:, :]]
        return jnp.concatenate(rot, axis=1).reshape(d, tm)

    lf = jnp.minimum(f, 0.0) - jnp.log1p(jnp.exp(-jnp.abs(f)))
    stacked = jnp.concatenate([x.astype(BF16) for x in _split3(lf)], axis=0)
    cs = _dot(stacked, tri_ref[...])
    carry = jnp.where(t == 0, 0.0, carry_ref[:, 0:1])
    c = cs[:F_ROWS] + cs[F_ROWS:2 * F_ROWS] + cs[2 * F_ROWS:] + carry
    carry_ref[...] = jnp.broadcast_to(c[:, tm - 1:tm], (F_ROWS, LANES))
    c2 = c * LOG2E

    qa = head_norm(z_qa, 0)
    z_va = feat_major(2)
    ka = head_norm(z_ka, 1)
    z_ga = feat_major(3)
    row = lax.broadcasted_iota(jnp.int32, (SUBLANES, tm), 0)
    pad = jnp.zeros((HEAD_DIM - SUBLANES, tm), F32)
    for hd in range(N_HEADS):
        c_hi, c_mid, c_lo = _split3(c2[hd:hd + 1, :])
        q_aug = jnp.where(row == 0, c_hi, jnp.where(row == 1, c_mid, jnp.where(
            row == 2, c_lo, jnp.where(row < 6, 1.0, 0.0))))
        k_aug = jnp.where(row < 3, 1.0, jnp.where(row == 3, -c_hi, jnp.where(
            row == 4, -c_mid, jnp.where(row == 5, -c_lo, 0.0))))
        rows = slice(hd * HEAD_DIM, (hd + 1) * HEAD_DIM)
        qt_ref[0, hd] = jnp.concatenate([qa[rows], q_aug, pad], axis=0).astype(BF16)
        k_ref[0, hd] = jnp.concatenate([ka[rows], k_aug, pad], axis=0).T.astype(BF16)

    ones_row = jnp.where(lax.broadcasted_iota(jnp.int32, (N_HEADS, BF16_ROWS, tm), 1) == 0,
                         1.0, 0.0)
    z_qb = feat_major(4)
    va = z_va.reshape(N_HEADS, HEAD_DIM, tm)
    vt_ref[0, :, 0] = jnp.concatenate([va, ones_row], axis=1).astype(BF16)
    z_kb = feat_major(5)
    gt_ref[0] = (z_ga * jax.nn.sigmoid(z_ga)).reshape(N_HEADS, HEAD_DIM, tm).astype(BF16)

    qb_ref[...] = rope(head_norm(z_qb, 2)).T
    kb_ref[...] = rope(head_norm(z_kb, 3)).T

    def max_sq_norm(y):
        y3 = y.reshape(N_HEADS, HEAD_DIM, tm)
        return jnp.max(jnp.sum(y3 * y3, axis=1), axis=-1, keepdims=True)

    lane = lax.broadcasted_iota(jnp.int32, (N_HEADS, LANES), 1)
    stats = jnp.where(lane == 0, max_sq_norm(qa), jnp.where(lane == 1, max_sq_norm(ka), 0.0))
    for sub in range(tm // FOX_UQ):
        lo = sub * FOX_UQ
        hi = lo + FOX_UQ - 1
        stats = jnp.where(lane == 2 + 2 * sub, c2[:N_HEADS, lo:lo + 1],
                          jnp.where(lane == 3 + 2 * sub, c2[:N_HEADS, hi:hi + 1], stats))
    stats_ref[0] = stats


def _proj(h2, pos_row, batch, seq, g1, wt, wn, gcol, bf_col, invf, tri):
    nt = seq // ROW_TILE
    rows = batch * seq
    d = D_BRANCH
    row = lambda w: pl.BlockSpec((ROW_TILE, w), lambda b, t: (b * nt + t, 0))
    full = lambda a: pl.BlockSpec(a.shape, lambda b, t: (0,) * a.ndim)
    out_shape = (
        jax.ShapeDtypeStruct((batch, N_HEADS, 2 * HEAD_DIM, seq), BF16),
        jax.ShapeDtypeStruct((batch, N_HEADS, seq, 2 * HEAD_DIM), BF16),
        jax.ShapeDtypeStruct((batch, N_HEADS, nt, FOX_VROWS, ROW_TILE), BF16),
        jax.ShapeDtypeStruct((batch, N_HEADS, HEAD_DIM, seq), BF16),
        jax.ShapeDtypeStruct((rows, d), F32),
        jax.ShapeDtypeStruct((rows, d), F32),
        jax.ShapeDtypeStruct((rows, d), F32),
        jax.ShapeDtypeStruct((rows, d), BF16),
        jax.ShapeDtypeStruct((batch * nt, N_HEADS, LANES), F32),
    )
    out_specs = (
        pl.BlockSpec((1, N_HEADS, 2 * HEAD_DIM, ROW_TILE), lambda b, t: (b, 0, 0, t)),
        pl.BlockSpec((1, N_HEADS, ROW_TILE, 2 * HEAD_DIM), lambda b, t: (b, 0, t, 0)),
        pl.BlockSpec((1, N_HEADS, 1, FOX_VROWS, ROW_TILE), lambda b, t: (b, 0, t, 0, 0)),
        pl.BlockSpec((1, N_HEADS, HEAD_DIM, ROW_TILE), lambda b, t: (b, 0, 0, t)),
        row(d), row(d), row(d), row(d),
        pl.BlockSpec((1, N_HEADS, LANES), lambda b, t: (b * nt + t, 0, 0)),
    )
    return pl.pallas_call(
        _proj_kernel,
        out_shape=out_shape,
        grid=(batch, nt),
        in_specs=[row(D_MODEL), pl.BlockSpec((1, ROW_TILE), lambda b, t: (0, b * nt + t)),
                  full(g1), full(wt), full(wn), full(gcol), full(bf_col), full(invf), full(tri)],
        out_specs=out_specs,
        scratch_shapes=[pltpu.VMEM((F_ROWS, LANES), F32)],
        compiler_params=pltpu.CompilerParams(
            dimension_semantics=("arbitrary", "arbitrary"), vmem_limit_bytes=VMEM_LIMIT_BYTES),
        name="in_proj",
    )(h2, pos_row, g1, wt, wn, gcol, bf_col, invf, tri)


def _fox_kernel(plan_ref, q_ref, k_ref, v_ref, g_ref, o_ref):
    i = pl.program_id(2)
    plan = plan_ref[(pl.program_id(0) * N_HEADS + pl.program_id(1)) * pl.num_programs(2) + i]
    first = plan // len(FOX_EDGE_OFFSETS)
    edge = plan % len(FOX_EDGE_OFFSETS)
    uk, uq = FOX_UK, FOX_UQ
    n_q = FOX_T // uq
    q = [q_ref[0, 0, :, a * uq:(a + 1) * uq] for a in range(n_q)]
    rel = (lax.broadcasted_iota(jnp.int32, (uk, uq), 0)
           - lax.broadcasted_iota(jnp.int32, (uk, uq), 1))

    def scores(j, ks, a):
        start = pl.multiple_of(j * FOX_T, FOX_T)
        return _dot(k_ref[0, 0, pl.ds(start + ks * uk, uk), :], q[a])

    def consume(s, v_sub, state, max_rel):
        m, acc = state
        if max_rel is not None:
            s = jnp.where(rel <= max_rel, s, NEG)
        m_new = jnp.maximum(m, jnp.max(s, axis=0, keepdims=True))
        alpha = jnp.exp2(m - m_new)
        p = jnp.exp2((s - m_new).astype(BF16))
        acc = alpha * acc + _dot(v_sub, p)
        return m_new, acc

    def step(j, carry, diagonal=False, min_offset=None):
        s_first, states = carry
        states = list(states)
        units = []
        for ks in range(FOX_T // uk):
            for a in range(n_q):
                max_rel = None
                if min_offset is not None and ks - a < min_offset:
                    continue
                if diagonal:
                    if ks * uk >= (a + 1) * uq:
                        continue
                    if (ks + 1) * uk - 1 > a * uq:
                        max_rel = a * uq - ks * uk
                units.append((ks, a, max_rel))
        pending = {0: scores(j, units[0][0], units[0][1]) if s_first is None else s_first}
        queue = list(range(1, len(units))) + ([] if diagonal else ["next"])
        result = {}

        def issue(item):
            if item == "next":
                result["next"] = scores(j + 1, 0, 0)
            else:
                pending[item] = scores(j, units[item][0], units[item][1])

        for item in queue[:FOX_AHEAD - 1]:
            issue(item)
        for n, (ks, a, max_rel) in enumerate(units):
            if n + FOX_AHEAD - 1 < len(queue):
                issue(queue[n + FOX_AHEAD - 1])
            chunk, off = divmod(ks * uk, ROW_TILE)
            v_sub = v_ref[0, 0, j * (FOX_T // ROW_TILE) + chunk, :, off:off + uk]
            states[a] = consume(pending.pop(n), v_sub, states[a], max_rel)
        return result.get("next"), tuple(states)

    init = tuple((jnp.full((1, uq), NEG, F32), jnp.zeros((FOX_VROWS, uq), F32))
                 for _ in range(n_q))
    branches = [lambda: (scores(i, 0, 0), init)]
    for offset in FOX_EDGE_OFFSETS:
        branches.append(functools.partial(step, first, (None, init), min_offset=offset))
    carry = lax.switch(jnp.where(first >= i, 0, edge + 1), branches)
    carry = lax.fori_loop(first + 1, i, step, carry)
    _, states = step(i, carry, diagonal=True)
    out = jnp.concatenate([acc[:HEAD_DIM] / acc[HEAD_DIM:HEAD_DIM + 1] for _, acc in states],
                          axis=1)
    o_ref[0, 0] = (out * g_ref[0, 0].astype(F32)).astype(BF16)


def _fox_plan(stats, batch, seq):
    nt, nq = seq // ROW_TILE, seq // FOX_T
    sub_t, sub_q = ROW_TILE // FOX_UQ, FOX_T // FOX_UQ
    st = stats.reshape(batch, nt, N_HEADS, LANES)
    q_sq = st[..., 0].reshape(batch, nq, nt // nq, N_HEADS).max(axis=2)
    k_sq = st[..., 1].max(axis=1, keepdims=True)
    bound = jnp.sqrt(q_sq * k_sq) * 1.01 + 1.0
    edges = st[..., 2:2 + 2 * sub_t].reshape(batch, nt, N_HEADS, sub_t, 2)
    edges = edges.transpose(0, 1, 3, 2, 4).reshape(batch, nq, sub_q, N_HEADS, 2)
    c_first, c_last = edges[..., 0], edges[..., 1]
    gap = c_first[:, :, :, None, None, :] - c_last[:, None, None, :, :, :]
    skip = (2.0 * bound[:, :, None, None, None, :] + gap) < -FOX_SKIP_BITS
    steps = jnp.arange(nq, dtype=jnp.int32)
    step_skip = jnp.all(skip, axis=(2, 4))
    first = jnp.min(jnp.where(step_skip, nq, steps[None, None, :, None]), axis=2)
    first = jnp.minimum(first, steps[None, :, None])
    other_step = steps[None, None, None, :, None, None] != first[:, :, None, None, None, :]
    at_first = jnp.all(skip | other_step, axis=3)
    offset = (jnp.arange(sub_q)[None, :] - jnp.arange(sub_q)[:, None])
    variant = jnp.zeros_like(first)
    for v, min_offset in enumerate(FOX_EDGE_OFFSETS):
        if min_offset is not None:
            dropped = (offset < min_offset)[None, None, :, :, None]
            ok = jnp.all(at_first | ~dropped, axis=(2, 3))
            variant = jnp.where(ok, v, variant)
    plan = first * len(FOX_EDGE_OFFSETS) + variant
    return plan.transpose(0, 2, 1).reshape(-1).astype(jnp.int32)


def _fox(first, qt, k, vt, gt):
    b, _, _, s = qt.shape
    q_spec = pl.BlockSpec((1, 1, 2 * HEAD_DIM, FOX_T), lambda bi, h, i, f: (bi, h, 0, i))
    k_spec = pl.BlockSpec((1, 1, s, 2 * HEAD_DIM), lambda bi, h, i, f: (bi, h, 0, 0))
    v_spec = pl.BlockSpec((1, 1, s // ROW_TILE, FOX_VROWS, ROW_TILE),
                          lambda bi, h, i, f: (bi, h, 0, 0, 0))
    o_spec = pl.BlockSpec((1, 1, HEAD_DIM, FOX_T), lambda bi, h, i, f: (bi, h, 0, i))
    return pl.pallas_call(
        _fox_kernel,
        out_shape=jax.ShapeDtypeStruct((b, N_HEADS, HEAD_DIM, s), BF16),
        grid_spec=pltpu.PrefetchScalarGridSpec(
            num_scalar_prefetch=1,
            grid=(b, N_HEADS, s // FOX_T),
            in_specs=[q_spec, k_spec, v_spec, o_spec],
            out_specs=o_spec),
        compiler_params=pltpu.CompilerParams(
            dimension_semantics=("arbitrary",) * 3, vmem_limit_bytes=VMEM_LIMIT_BYTES),
        name="fox_attention",
    )(first, qt, k, vt, gt)


def _dil_kernel(q_ref, k_ref, v_ref, g_ref, o_ref, q4_s, k4_s, v4_s, acc_s, m_s, l_s, *, seq):
    tq, tw, sup = DIL_TQ, DIL_TW, DIL_SUPER
    part = seq // DIL_SPLIT

    def regroup(c, carry):
        for src, dst in ((q_ref, q4_s), (k_ref, k4_s), (v_ref, v4_s)):
            for r in range(DIL_SPLIT):
                rows = src[0, pl.ds(c * DIL_SPLIT * DIL_COPY + r, DIL_COPY, stride=DIL_SPLIT), :]
                dst[pl.ds(r * part + c * DIL_COPY, DIL_COPY), :] = rows
        return carry

    lax.fori_loop(0, part // DIL_COPY, regroup, 0)

    def stream_rows(nat_ref, split_ref, d, r, pos, n):
        if d % DIL_SPLIT:
            x = nat_ref[0, pl.ds(d * pos + r, n, stride=d), :]
        else:
            sd = d // DIL_SPLIT
            start = (r % DIL_SPLIT) * part + sd * pos + r // DIL_SPLIT
            x = split_ref[pl.ds(start, n, stride=sd), :]
        return x.astype(BF16)

    lane = lax.broadcasted_iota(jnp.int32, (1, LANES), 1)
    low = lane < HEAD_DIM
    rel = (lax.broadcasted_iota(jnp.int32, (tq, tw), 0)
           - lax.broadcasted_iota(jnp.int32, (tq, tw), 1))

    def band_bias(q_minus_w):
        dist = rel + q_minus_w
        return jnp.where(dist >= 0, jnp.where(dist <= N_BACK, 0.0, NEG), NEG)

    bias_inner = band_bias(N_BACK)
    bias_start = band_bias(0)

    def issue(d, base, idx, nblk):
        r = idx // nblk
        jb = idx % nblk
        qs = base // d + jb * tq
        ws = jnp.maximum(qs - N_BACK, 0)
        q = stream_rows(q_ref, q4_s, d, r, qs, tq)
        kw = stream_rows(k_ref, k4_s, d, r, ws, tw)
        zero = jnp.zeros_like(q)
        q2 = jnp.concatenate([jnp.where(low, q, zero), jnp.where(low, zero, q)], axis=0)
        return _dot_nt(q2, kw), r, qs, ws, d * (qs - base // d) + r

    def consume(g, d, s, r, qs, ws, out_row):
        vw = stream_rows(v_ref, v4_s, d, r, ws, tw)
        bias = jnp.where(qs == ws, bias_start, bias_inner)
        ps, ms = [], []
        for hh in range(2):
            sm = s[hh * tq:(hh + 1) * tq] + bias
            m = jnp.max(sm, axis=-1, keepdims=True)
            ps.append(jnp.exp2((sm - m).astype(BF16)))
            ms.append(jnp.broadcast_to(m, (tq, LANES)))
        vw1 = jnp.concatenate([vw, jnp.ones_like(vw)], axis=1)
        acc = _dot(jnp.concatenate(ps, axis=0), vw1)
        rows = pl.ds(out_row, tq, stride=d)
        acc_s[g, rows, :] = jnp.where(low, acc[:tq, :LANES], acc[tq:, :LANES])
        m_s[g, rows, :] = jnp.where(low, ms[0], ms[1])
        l_s[g, rows, :] = jnp.where(low, acc[:tq, LANES:], acc[tq:, LANES:])

    def superblock(sb, carry):
        base = pl.multiple_of(sb * sup, sup)
        for g, (_, d) in enumerate(DILATED_PATTERNS):
            nblk = sup // (d * tq)

            def body(it, c, g=g, d=d, nblk=nblk):
                first = it * DIL_GROUP
                pending = {n: issue(d, base, first + n, nblk) for n in range(DIL_AHEAD)}
                for n in range(DIL_GROUP):
                    if n + DIL_AHEAD < DIL_GROUP:
                        pending[n + DIL_AHEAD] = issue(d, base, first + n + DIL_AHEAD, nblk)
                    consume(g, d, *pending.pop(n))
                return c

            lax.fori_loop(0, d * nblk // DIL_GROUP, body, 0)

        rows = pl.ds(base, sup)
        m1, m2, m3 = m_s[0], m_s[1], m_s[2]
        mx = jnp.maximum(jnp.maximum(m1, m2), m3)
        e1, e2, e3 = jnp.exp2(m1 - mx), jnp.exp2(m2 - mx), jnp.exp2(m3 - mx)
        num = e1 * acc_s[0] + e2 * acc_s[1] + e3 * acc_s[2]
        den = e1 * l_s[0] + e2 * l_s[1] + e3 * l_s[2]
        o_ref[0, rows, :] = (num / den * g_ref[0, rows, :].astype(F32)).astype(BF16)
        return carry

    lax.fori_loop(0, seq // sup, superblock, 0)


def _dilated(qb, kb, vb, gb):
    b, s, _ = qb.shape
    spec = pl.BlockSpec((1, s, LANES), lambda bi, p: (bi, 0, p))
    regrouped = pltpu.VMEM((s, LANES), F32)
    scratch = pltpu.VMEM((len(DILATED_PATTERNS), DIL_SUPER, LANES), F32)
    return pl.pallas_call(
        functools.partial(_dil_kernel, seq=s),
        out_shape=jax.ShapeDtypeStruct((b, s, D_BRANCH), BF16),
        grid=(b, D_BRANCH // LANES),
        in_specs=[spec, spec, spec, spec],
        out_specs=spec,
        scratch_shapes=[regrouped, regrouped, regrouped, scratch, scratch, scratch],
        compiler_params=pltpu.CompilerParams(
            dimension_semantics=("arbitrary",) * 2, vmem_limit_bytes=VMEM_LIMIT_BYTES),
        name="dilated_attn",
    )(qb, kb, vb, gb)


def _post_kernel(h_ref, oat_ref, ob_ref, p_ref, woa_ref, wob_ref, g2_ref, wg_ref, wple_ref,
                 out_ref):
    h1 = h_ref[...] + _dot_tn(oat_ref[0], woa_ref[...]) + _dot(ob_ref[...], wob_ref[...])
    u2 = _rms_norm_rows(h1, g2_ref[...]).astype(BF16)
    gate = jax.nn.sigmoid(_dot(u2, wg_ref[...]))
    ple = _dot(p_ref[0].astype(BF16), wple_ref[...])
    out_ref[...] = h1 + ple * gate


def _post(h2, oat, ob, p3, layer, batch, seq, woa, wob, g2, wg, wple):
    nt = seq // POST_TILE
    row = lambda w: pl.BlockSpec((POST_TILE, w), lambda b, t: (b * nt + t, 0))
    full = lambda a: pl.BlockSpec(a.shape, lambda b, t: (0,) * a.ndim)
    return pl.pallas_call(
        _post_kernel,
        out_shape=jax.ShapeDtypeStruct((batch * seq, D_MODEL), F32),
        grid=(batch, nt),
        in_specs=[row(D_MODEL), pl.BlockSpec((1, D_BRANCH, POST_TILE), lambda b, t: (b, 0, t)),
                  row(D_BRANCH),
                  pl.BlockSpec((1, POST_TILE, PLE_DIM), lambda b, t: (layer, b * nt + t, 0)),
                  full(woa), full(wob), full(g2), full(wg), full(wple)],
        out_specs=row(D_MODEL),
        compiler_params=pltpu.CompilerParams(
            dimension_semantics=("arbitrary", "arbitrary"), vmem_limit_bytes=VMEM_LIMIT_BYTES),
        name="out_stage",
    )(h2, oat, ob, p3, woa, wob, g2, wg, wple)


def _layer(h2, p3, layer, pos_row, invf, tri, batch, seq, norm_g, w_in, b_f, qk_g, w_out, w_ple,
           ple_norm_g, w_ple_gate):
    d = D_BRANCH
    cuts = np.cumsum([0, d, d, d, d, N_HEADS, d, d, d, d])
    qa_w, ka_w, va_w, ga_w, f_w, qb_w, kb_w, vb_w, gb_w = (
        w_in[:, cuts[i]:cuts[i + 1]] for i in range(9))
    f_w = jnp.pad(f_w, ((0, 0), (0, F_ROWS - N_HEADS)))
    wt = jnp.concatenate([qa_w, ka_w, va_w, ga_w, qb_w, kb_w, f_w], axis=1).T.astype(BF16)
    wn = jnp.concatenate([vb_w, gb_w], axis=1).astype(BF16)
    bf_col = jnp.pad(b_f.astype(F32), (0, F_ROWS - N_HEADS)).reshape(F_ROWS, 1)
    scale = HEAD_DIM ** -0.5
    gcol = jnp.concatenate([jnp.tile(qk_g[0] * (scale * LOG2E), N_HEADS), jnp.tile(qk_g[1], N_HEADS),
                            jnp.tile(qk_g[2] * (scale * LOG2E), N_HEADS), jnp.tile(qk_g[3], N_HEADS)]
                           ).astype(F32).reshape(4 * d, 1)

    qt, k, vt, gt, qb, kb, vb, gb, stats = _proj(
        h2, pos_row, batch, seq, norm_g.reshape(1, D_MODEL), wt, wn, gcol, bf_col, invf, tri)

    oat = _fox(_fox_plan(stats, batch, seq), qt, k, vt, gt).reshape(batch, d, seq)
    as3 = lambda a: a.reshape(batch, seq, d)
    ob = _dilated(as3(qb), as3(kb), as3(vb), as3(gb)).reshape(batch * seq, d)

    w_out_b = w_out.astype(BF16)
    return _post(h2, oat, ob, p3, layer, batch, seq, w_out_b[:d], w_out_b[d:],
                 ple_norm_g.reshape(1, D_MODEL), w_ple_gate.astype(BF16), w_ple.astype(BF16))


def kernel(x, p, positions, norm_g, w_in, b_f, qk_norm_g, w_out, w_ple, ple_norm_g, w_ple_gate):
    batch, seq, _ = x.shape
    depth = p.shape[0]
    assert x.shape[2] == D_MODEL and p.shape[1:] == (batch, seq, PLE_DIM)
    assert all(w // dil == N_BACK for w, dil in DILATED_PATTERNS)
    assert all(seq % tile == 0 for tile in (ROW_TILE, POST_TILE, FOX_T, DIL_SUPER,
                                            DIL_SPLIT * DIL_COPY))
    assert FOX_UK == FOX_UQ and FOX_T % FOX_UQ == 0 and ROW_TILE % FOX_UQ == 0
    assert FOX_T % ROW_TILE == 0 and DIL_SUPER % (DIL_GROUP * DIL_TQ) == 0
    pos_row = positions.reshape(1, batch * seq)
    inv_freq = ROPE_THETA ** (-jnp.arange(ROPE_HALF, dtype=F32) / ROPE_HALF)
    invf = inv_freq.reshape(ROPE_HALF, 1)
    idx = np.arange(ROW_TILE)
    tri = jnp.asarray(idx[:, None] <= idx[None, :], BF16)
    h2 = x.reshape(batch * seq, D_MODEL)
    p3 = p.reshape(depth, batch * seq, PLE_DIM)
    for i in range(depth):
        h2 = _layer(h2, p3, i, pos_row, invf, tri, batch, seq,
                    norm_g[i], w_in[i], b_f[i], qk_norm_g[i], w_out[i], w_ple[i],
                    ple_norm_g[i], w_ple_gate[i])
    return h2.reshape(batch, seq, D_MODEL)
```

```python
import functools

import numpy as np
import jax
import jax.numpy as jnp
from jax import lax
from jax.experimental import pallas as pl
from jax.experimental.pallas import tpu as pltpu

D_MODEL = 1024
HEAD_DIM = 64
N_HEADS = 8
D_BRANCH = N_HEADS * HEAD_DIM
PLE_DIM = 256
ROPE_THETA = 500000.0
ROPE_DIM = HEAD_DIM // 4
ROPE_HALF = ROPE_DIM // 2
DILATED_PATTERNS = ((128, 1), (512, 4), (2048, 16))
N_BACK = 128
EPS = 1e-6
NEG = -1e30

LANES = 128
SUBLANES = 8
BF16_ROWS = 16
VMEM_LIMIT_BYTES = 56 * 1024 * 1024

ROW_TILE = 512
POST_TILE = 1024
FOX_T = 1024
FOX_UK = 256
FOX_UQ = 256
FOX_AHEAD = 8
FOX_VROWS = HEAD_DIM + BF16_ROWS
FOX_SKIP_BITS = 150.0
FOX_EDGE_OFFSETS = (None, 0, 1, 2)
LOG2E = 1.4426950408889634
DIL_TQ = N_BACK
DIL_TW = DIL_TQ + N_BACK
DIL_SUPER = max(d for _, d in DILATED_PATTERNS) * DIL_TQ
DIL_SPLIT = 4
DIL_COPY = 512
DIL_GROUP = 8
DIL_AHEAD = 4
F_ROWS = BF16_ROWS

F32 = jnp.float32
BF16 = jnp.bfloat16


def _dot(a, b):
    return jnp.dot(a, b, preferred_element_type=F32)


def _dot_nt(a, b):
    return lax.dot_general(a, b, (((1,), (1,)), ((), ())), preferred_element_type=F32)


def _dot_tn(a, b):
    return lax.dot_general(a, b, (((0,), (0,)), ((), ())), preferred_element_type=F32)


def _split3(x):
    hi = x.astype(BF16).astype(F32)
    r = x - hi
    mid = r.astype(BF16).astype(F32)
    lo = (r - mid).astype(BF16).astype(F32)
    return hi, mid, lo


def _rms_norm_rows(h, g_row):
    ms = jnp.mean(h * h, axis=-1, keepdims=True)
    return h * lax.rsqrt(ms + EPS) * g_row


def _proj_kernel(h_ref, pos_ref, g1_ref, wc_ref, wn_ref, gcol_ref, bf_ref, invf_ref, tri_ref,
                 qt_ref, k_ref, vt_ref, gt_ref, qb_ref, kb_ref, vb_ref, gb_ref, stats_ref,
                 carry_ref, wt_ref):
    t = pl.program_id(1)
    tm = ROW_TILE
    d = D_BRANCH

    @pl.when((pl.program_id(0) == 0) & (t == 0))
    def _():
        for g in range(6):
            wt_ref[g * d:(g + 1) * d, :] = wc_ref[:, g * d:(g + 1) * d].astype(F32).T.astype(BF16)
        f_cols = wc_ref[:, 6 * d:6 * d + LANES].astype(F32).T
        wt_ref[6 * d:6 * d + F_ROWS, :] = f_cols[:F_ROWS].astype(BF16)

    u = _rms_norm_rows(h_ref[...], g1_ref[...]).astype(BF16)

    def feat_major(group):
        return _dot_nt(wt_ref[group * d:(group + 1) * d, :], u)

    zn = _dot(u, wn_ref[...])
    f = _dot_nt(wt_ref[6 * d:6 * d + F_ROWS, :], u) + bf_ref[...]
    z_qa = feat_major(0)

    vb_ref[...] = zn[:, :d]
    gate_b = zn[:, d:]
    gb_ref[...] = (gate_b * jax.nn.sigmoid(gate_b)).astype(BF16)
    z_ka = feat_major(1)

    def head_norm(z, group):
        z3 = z.reshape(N_HEADS, HEAD_DIM, tm)
        ss = jnp.sum(z3 * z3, axis=1, keepdims=True)
        y3 = z3 * lax.rsqrt(ss * (1.0 / HEAD_DIM) + EPS)
        return y3.reshape(d, tm) * gcol_ref[group * d:(group + 1) * d, :]

    ang = invf_ref[...] * pos_ref[...].astype(F32)
    cos = jnp.cos(ang)
    sin = jnp.sin(ang)

    def rope(y):
        y3 = y.reshape(N_HEADS, HEAD_DIM, tm)
        x1 = y3[:, :ROPE_HALF, :]
        x2 = y3[:, ROPE_HALF:ROPE_DIM, :]
        rot = [x1 * cos - x2 * sin, x2 * cos + x1 * sin, y3[:, ROPE_DIM:, :]]
        return jnp.concatenate(rot, axis=1).reshape(d, tm)

    lf = jnp.minimum(f, 0.0) - jnp.log1p(jnp.exp(-jnp.abs(f)))
    stacked = jnp.concatenate([x.astype(BF16) for x in _split3(lf)], axis=0)
    cs = _dot(stacked, tri_ref[...])
    carry = jnp.where(t == 0, 0.0, carry_ref[:, 0:1])
    c = cs[:F_ROWS] + cs[F_ROWS:2 * F_ROWS] + cs[2 * F_ROWS:] + carry
    carry_ref[...] = jnp.broadcast_to(c[:, tm - 1:tm], (F_ROWS, LANES))
    c2 = c * LOG2E

    qa = head_norm(z_qa, 0)
    z_va = feat_major(2)
    ka = head_norm(z_ka, 1)
    z_ga = feat_major(3)
    row = lax.broadcasted_iota(jnp.int32, (SUBLANES, tm), 0)
    pad = jnp.zeros((HEAD_DIM - SUBLANES, tm), F32)
    for hd in range(N_HEADS):
        c_hi, c_mid, c_lo = _split3(c2[hd:hd + 1, :])
        q_aug = jnp.where(row == 0, c_hi, jnp.where(row == 1, c_mid, jnp.where(
            row == 2, c_lo, jnp.where(row < 6, 1.0, 0.0))))
        k_aug = jnp.where(row < 3, 1.0, jnp.where(row == 3, -c_hi, jnp.where(
            row == 4, -c_mid, jnp.where(row == 5, -c_lo, 0.0))))
        rows = slice(hd * HEAD_DIM, (hd + 1) * HEAD_DIM)
        qt_ref[0, hd] = jnp.concatenate([qa[rows], q_aug, pad], axis=0).astype(BF16)
        k_ref[0, hd] = jnp.concatenate([ka[rows], k_aug, pad], axis=0).T.astype(BF16)

    ones_row = jnp.where(lax.broadcasted_iota(jnp.int32, (N_HEADS, BF16_ROWS, tm), 1) == 0,
                         1.0, 0.0)
    z_qb = feat_major(4)
    va = z_va.reshape(N_HEADS, HEAD_DIM, tm)
    vt_ref[0, :, 0] = jnp.concatenate([va, ones_row], axis=1).astype(BF16)
    z_kb = feat_major(5)
    gt_ref[0] = (z_ga * jax.nn.sigmoid(z_ga)).reshape(N_HEADS, HEAD_DIM, tm).astype(BF16)

    qb_ref[...] = rope(head_norm(z_qb, 2)).T
    kb_ref[...] = rope(head_norm(z_kb, 3)).T

    def max_sq_norm(y):
        y3 = y.reshape(N_HEADS, HEAD_DIM, tm)
        return jnp.max(jnp.sum(y3 * y3, axis=1), axis=-1, keepdims=True)

    lane = lax.broadcasted_iota(jnp.int32, (N_HEADS, LANES), 1)
    stats = jnp.where(lane == 0, max_sq_norm(qa), jnp.where(lane == 1, max_sq_norm(ka), 0.0))
    for sub in range(tm // FOX_UQ):
        lo = sub * FOX_UQ
        hi = lo + FOX_UQ - 1
        stats = jnp.where(lane == 2 + 2 * sub, c2[:N_HEADS, lo:lo + 1],
                          jnp.where(lane == 3 + 2 * sub, c2[:N_HEADS, hi:hi + 1], stats))
    stats_ref[0] = stats


def _proj(h2, pos_row, batch, seq, g1, wc, wn, gcol, bf_col, invf, tri):
    nt = seq // ROW_TILE
    rows = batch * seq
    d = D_BRANCH
    row = lambda w: pl.BlockSpec((ROW_TILE, w), lambda b, t: (b * nt + t, 0))
    full = lambda a: pl.BlockSpec(a.shape, lambda b, t: (0,) * a.ndim)
    once = lambda a: pl.BlockSpec(a.shape, lambda b, t: (0,) * a.ndim,
                                  pipeline_mode=pl.Buffered(1))
    out_shape = (
        jax.ShapeDtypeStruct((batch, N_HEADS, 2 * HEAD_DIM, seq), BF16),
        jax.ShapeDtypeStruct((batch, N_HEADS, seq, 2 * HEAD_DIM), BF16),
        jax.ShapeDtypeStruct((batch, N_HEADS, nt, FOX_VROWS, ROW_TILE), BF16),
        jax.ShapeDtypeStruct((batch, N_HEADS, HEAD_DIM, seq), BF16),
        jax.ShapeDtypeStruct((rows, d), F32),
        jax.ShapeDtypeStruct((rows, d), F32),
        jax.ShapeDtypeStruct((rows, d), F32),
        jax.ShapeDtypeStruct((rows, d), BF16),
        jax.ShapeDtypeStruct((batch * nt, N_HEADS, LANES), F32),
    )
    out_specs = (
        pl.BlockSpec((1, N_HEADS, 2 * HEAD_DIM, ROW_TILE), lambda b, t: (b, 0, 0, t)),
        pl.BlockSpec((1, N_HEADS, ROW_TILE, 2 * HEAD_DIM), lambda b, t: (b, 0, t, 0)),
        pl.BlockSpec((1, N_HEADS, 1, FOX_VROWS, ROW_TILE), lambda b, t: (b, 0, t, 0, 0)),
        pl.BlockSpec((1, N_HEADS, HEAD_DIM, ROW_TILE), lambda b, t: (b, 0, 0, t)),
        row(d), row(d), row(d), row(d),
        pl.BlockSpec((1, N_HEADS, LANES), lambda b, t: (b * nt + t, 0, 0)),
    )
    return pl.pallas_call(
        _proj_kernel,
        out_shape=out_shape,
        grid=(batch, nt),
        in_specs=[row(D_MODEL), pl.BlockSpec((1, ROW_TILE), lambda b, t: (0, b * nt + t)),
                  full(g1), once(wc), full(wn), full(gcol), full(bf_col), full(invf), full(tri)],
        out_specs=out_specs,
        scratch_shapes=[pltpu.VMEM((F_ROWS, LANES), F32),
                        pltpu.VMEM((6 * d + F_ROWS, D_MODEL), BF16)],
        compiler_params=pltpu.CompilerParams(
            dimension_semantics=("arbitrary", "arbitrary"), vmem_limit_bytes=VMEM_LIMIT_BYTES),
        name="in_proj",
    )(h2, pos_row, g1, wc, wn, gcol, bf_col, invf, tri)


def _fox_kernel(plan_ref, q_ref, k_ref, v_ref, g_ref, o_ref):
    i = pl.program_id(2)
    plan = plan_ref[(pl.program_id(0) * N_HEADS + pl.program_id(1)) * pl.num_programs(2) + i]
    first = plan // len(FOX_EDGE_OFFSETS)
    edge = plan % len(FOX_EDGE_OFFSETS)
    uk, uq = FOX_UK, FOX_UQ
    n_q = FOX_T // uq
    q = [q_ref[0, 0, :, a * uq:(a + 1) * uq] for a in range(n_q)]
    rel = (lax.broadcasted_iota(jnp.int32, (uk, uq), 0)
           - lax.broadcasted_iota(jnp.int32, (uk, uq), 1))

    def scores(j, ks, a):
        start = pl.multiple_of(j * FOX_T, FOX_T)
        return _dot(k_ref[0, 0, pl.ds(start + ks * uk, uk), :], q[a])

    def consume(s, v_sub, state, max_rel):
        m, acc = state
        if max_rel is not None:
            s = jnp.where(rel <= max_rel, s, NEG)
        m_new = jnp.maximum(m, jnp.max(s, axis=0, keepdims=True))
        alpha = jnp.exp2(m - m_new)
        p = jnp.exp2((s - m_new).astype(BF16))
        acc = alpha * acc + _dot(v_sub, p)
        return m_new, acc

    def step(j, carry, diagonal=False, min_offset=None):
        s_first, states = carry
        states = list(states)
        units = []
        for ks in range(FOX_T // uk):
            for a in range(n_q):
                max_rel = None
                if min_offset is not None and ks - a < min_offset:
                    continue
                if diagonal:
                    if ks * uk >= (a + 1) * uq:
                        continue
                    if (ks + 1) * uk - 1 > a * uq:
                        max_rel = a * uq - ks * uk
                units.append((ks, a, max_rel))
        pending = {0: scores(j, units[0][0], units[0][1]) if s_first is None else s_first}
        queue = list(range(1, len(units))) + ([] if diagonal else ["next"])
        result = {}

        def issue(item):
            if item == "next":
                result["next"] = scores(j + 1, 0, 0)
            else:
                pending[item] = scores(j, units[item][0], units[item][1])

        for item in queue[:FOX_AHEAD - 1]:
            issue(item)
        for n, (ks, a, max_rel) in enumerate(units):
            if n + FOX_AHEAD - 1 < len(queue):
                issue(queue[n + FOX_AHEAD - 1])
            chunk, off = divmod(ks * uk, ROW_TILE)
            v_sub = v_ref[0, 0, j * (FOX_T // ROW_TILE) + chunk, :, off:off + uk]
            states[a] = consume(pending.pop(n), v_sub, states[a], max_rel)
        return result.get("next"), tuple(states)

    init = tuple((jnp.full((1, uq), NEG, F32), jnp.zeros((FOX_VROWS, uq), F32))
                 for _ in range(n_q))
    branches = [lambda: (scores(i, 0, 0), init)]
    for offset in FOX_EDGE_OFFSETS:
        branches.append(functools.partial(step, first, (None, init), min_offset=offset))
    carry = lax.switch(jnp.where(first >= i, 0, edge + 1), branches)
    carry = lax.fori_loop(first + 1, i, step, carry)
    _, states = step(i, carry, diagonal=True)
    out = jnp.concatenate([acc[:HEAD_DIM] / acc[HEAD_DIM:HEAD_DIM + 1] for _, acc in states],
                          axis=1)
    o_ref[0, 0] = (out * g_ref[0, 0].astype(F32)).astype(BF16)


def _fox_plan(stats, batch, seq):
    nt, nq = seq // ROW_TILE, seq // FOX_T
    sub_t, sub_q = ROW_TILE // FOX_UQ, FOX_T // FOX_UQ
    st = stats.reshape(batch, nt, N_HEADS, LANES)
    q_sq = st[..., 0].reshape(batch, nq, nt // nq, N_HEADS).max(axis=2)
    k_sq = st[..., 1].max(axis=1, keepdims=True)
    bound = jnp.sqrt(q_sq * k_sq) * 1.01 + 1.0
    edges = st[..., 2:2 + 2 * sub_t].reshape(batch, nt, N_HEADS, sub_t, 2)
    edges = edges.transpose(0, 1, 3, 2, 4).reshape(batch, nq, sub_q, N_HEADS, 2)
    c_first, c_last = edges[..., 0], edges[..., 1]
    gap = c_first[:, :, :, None, None, :] - c_last[:, None, None, :, :, :]
    skip = (2.0 * bound[:, :, None, None, None, :] + gap) < -FOX_SKIP_BITS
    steps = jnp.arange(nq, dtype=jnp.int32)
    step_skip = jnp.all(skip, axis=(2, 4))
    first = jnp.min(jnp.where(step_skip, nq, steps[None, None, :, None]), axis=2)
    first = jnp.minimum(first, steps[None, :, None])
    other_step = steps[None, None, None, :, None, None] != first[:, :, None, None, None, :]
    at_first = jnp.all(skip | other_step, axis=3)
    offset = (jnp.arange(sub_q)[None, :] - jnp.arange(sub_q)[:, None])
    variant = jnp.zeros_like(first)
    for v, min_offset in enumerate(FOX_EDGE_OFFSETS):
        if min_offset is not None:
            dropped = (offset < min_offset)[None, None, :, :, None]
            ok = jnp.all(at_first | ~dropped, axis=(2, 3))
            variant = jnp.where(ok, v, variant)
    plan = first * len(FOX_EDGE_OFFSETS) + variant
    return plan.transpose(0, 2, 1).reshape(-1).astype(jnp.int32)


def _fox(first, qt, k, vt, gt):
    b, _, _, s = qt.shape
    q_spec = pl.BlockSpec((1, 1, 2 * HEAD_DIM, FOX_T), lambda bi, h, i, f: (bi, h, 0, i))
    k_spec = pl.BlockSpec((1, 1, s, 2 * HEAD_DIM), lambda bi, h, i, f: (bi, h, 0, 0))
    v_spec = pl.BlockSpec((1, 1, s // ROW_TILE, FOX_VROWS, ROW_TILE),
                          lambda bi, h, i, f: (bi, h, 0, 0, 0))
    o_spec = pl.BlockSpec((1, 1, HEAD_DIM, FOX_T), lambda bi, h, i, f: (bi, h, 0, i))
    return pl.pallas_call(
        _fox_kernel,
        out_shape=jax.ShapeDtypeStruct((b, N_HEADS, HEAD_DIM, s), BF16),
        grid_spec=pltpu.PrefetchScalarGridSpec(
            num_scalar_prefetch=1,
            grid=(b, N_HEADS, s // FOX_T),
            in_specs=[q_spec, k_spec, v_spec, o_spec],
            out_specs=o_spec),
        compiler_params=pltpu.CompilerParams(
            dimension_semantics=("arbitrary",) * 3, vmem_limit_bytes=VMEM_LIMIT_BYTES),
        name="fox_attention",
    )(first, qt, k, vt, gt)


def _dil_kernel(q_ref, k_ref, v_ref, g_ref, o_ref, q4_s, k4_s, v4_s, acc_s, m_s, l_s, *, seq):
    tq, tw, sup = DIL_TQ, DIL_TW, DIL_SUPER
    part = seq // DIL_SPLIT

    def regroup(c, carry):
        for src, dst in ((q_ref, q4_s), (k_ref, k4_s), (v_ref, v4_s)):
            for r in range(DIL_SPLIT):
                rows = src[0, pl.ds(c * DIL_SPLIT * DIL_COPY + r, DIL_COPY, stride=DIL_SPLIT), :]
                dst[pl.ds(r * part + c * DIL_COPY, DIL_COPY), :] = rows
        return carry

    lax.fori_loop(0, part // DIL_COPY, regroup, 0)

    def stream_rows(nat_ref, split_ref, d, r, pos, n):
        if d % DIL_SPLIT:
            x = nat_ref[0, pl.ds(d * pos + r, n, stride=d), :]
        else:
            sd = d // DIL_SPLIT
            start = (r % DIL_SPLIT) * part + sd * pos + r // DIL_SPLIT
            x = split_ref[pl.ds(start, n, stride=sd), :]
        return x.astype(BF16)

    lane = lax.broadcasted_iota(jnp.int32, (1, LANES), 1)
    low = lane < HEAD_DIM
    rel = (lax.broadcasted_iota(jnp.int32, (tq, tw), 0)
           - lax.broadcasted_iota(jnp.int32, (tq, tw), 1))

    def band_bias(q_minus_w):
        dist = rel + q_minus_w
        return jnp.where(dist >= 0, jnp.where(dist <= N_BACK, 0.0, NEG), NEG)

    bias_inner = band_bias(N_BACK)
    bias_start = band_bias(0)

    def issue(d, base, idx, nblk):
        r = idx // nblk
        jb = idx % nblk
        qs = base // d + jb * tq
        ws = jnp.maximum(qs - N_BACK, 0)
        q = stream_rows(q_ref, q4_s, d, r, qs, tq)
        kw = stream_rows(k_ref, k4_s, d, r, ws, tw)
        zero = jnp.zeros_like(q)
        q2 = jnp.concatenate([jnp.where(low, q, zero), jnp.where(low, zero, q)], axis=0)
        return _dot_nt(q2, kw), r, qs, ws, d * (qs - base // d) + r

    def consume(g, d, s, r, qs, ws, out_row):
        vw = stream_rows(v_ref, v4_s, d, r, ws, tw)
        bias = jnp.where(qs == ws, bias_start, bias_inner)
        ps, ms = [], []
        for hh in range(2):
            sm = s[hh * tq:(hh + 1) * tq] + bias
            m = jnp.max(sm, axis=-1, keepdims=True)
            ps.append(jnp.exp2((sm - m).astype(BF16)))
            ms.append(jnp.broadcast_to(m, (tq, LANES)))
        vw1 = jnp.concatenate([vw, jnp.ones_like(vw)], axis=1)
        acc = _dot(jnp.concatenate(ps, axis=0), vw1)
        rows = pl.ds(out_row, tq, stride=d)
        acc_s[g, rows, :] = jnp.where(low, acc[:tq, :LANES], acc[tq:, :LANES])
        m_s[g, rows, :] = jnp.where(low, ms[0], ms[1])
        l_s[g, rows, :] = jnp.where(low, acc[:tq, LANES:], acc[tq:, LANES:])

    def superblock(sb, carry):
        base = pl.multiple_of(sb * sup, sup)
        for g, (_, d) in enumerate(DILATED_PATTERNS):
            nblk = sup // (d * tq)

            def body(it, c, g=g, d=d, nblk=nblk):
                first = it * DIL_GROUP
                pending = {n: issue(d, base, first + n, nblk) for n in range(DIL_AHEAD)}
                for n in range(DIL_GROUP):
                    if n + DIL_AHEAD < DIL_GROUP:
                        pending[n + DIL_AHEAD] = issue(d, base, first + n + DIL_AHEAD, nblk)
                    consume(g, d, *pending.pop(n))
                return c

            lax.fori_loop(0, d * nblk // DIL_GROUP, body, 0)

        rows = pl.ds(base, sup)
        m1, m2, m3 = m_s[0], m_s[1], m_s[2]
        mx = jnp.maximum(jnp.maximum(m1, m2), m3)
        e1, e2, e3 = jnp.exp2(m1 - mx), jnp.exp2(m2 - mx), jnp.exp2(m3 - mx)
        num = e1 * acc_s[0] + e2 * acc_s[1] + e3 * acc_s[2]
        den = e1 * l_s[0] + e2 * l_s[1] + e3 * l_s[2]
        o_ref[0, rows, :] = (num / den * g_ref[0, rows, :].astype(F32)).astype(BF16)
        return carry

    lax.fori_loop(0, seq // sup, superblock, 0)


def _dilated(qb, kb, vb, gb):
    b, s, _ = qb.shape
    spec = pl.BlockSpec((1, s, LANES), lambda bi, p: (bi, 0, p))
    regrouped = pltpu.VMEM((s, LANES), F32)
    scratch = pltpu.VMEM((len(DILATED_PATTERNS), DIL_SUPER, LANES), F32)
    return pl.pallas_call(
        functools.partial(_dil_kernel, seq=s),
        out_shape=jax.ShapeDtypeStruct((b, s, D_BRANCH), BF16),
        grid=(b, D_BRANCH // LANES),
        in_specs=[spec, spec, spec, spec],
        out_specs=spec,
        scratch_shapes=[regrouped, regrouped, regrouped, scratch, scratch, scratch],
        compiler_params=pltpu.CompilerParams(
            dimension_semantics=("arbitrary",) * 2, vmem_limit_bytes=VMEM_LIMIT_BYTES),
        name="dilated_attn",
    )(qb, kb, vb, gb)


def _post_kernel(h_ref, oat_ref, ob_ref, p_ref, woa_ref, wob_ref, g2_ref, wg_ref, wple_ref,
                 out_ref):
    h1 = h_ref[...] + _dot_tn(oat_ref[0], woa_ref[...]) + _dot(ob_ref[...], wob_ref[...])
    u2 = _rms_norm_rows(h1, g2_ref[...]).astype(BF16)
    gate = jax.nn.sigmoid(_dot(u2, wg_ref[...]))
    ple = _dot(p_ref[0].astype(BF16), wple_ref[...])
    out_ref[...] = h1 + ple * gate


def _post(h2, oat, ob, p3, layer, batch, seq, woa, wob, g2, wg, wple):
    nt = seq // POST_TILE
    row = lambda w: pl.BlockSpec((POST_TILE, w), lambda b, t: (b * nt + t, 0))
    full = lambda a: pl.BlockSpec(a.shape, lambda b, t: (0,) * a.ndim)
    return pl.pallas_call(
        _post_kernel,
        out_shape=jax.ShapeDtypeStruct((batch * seq, D_MODEL), F32),
        grid=(batch, nt),
        in_specs=[row(D_MODEL), pl.BlockSpec((1, D_BRANCH, POST_TILE), lambda b, t: (b, 0, t)),
                  row(D_BRANCH),
                  pl.BlockSpec((1, POST_TILE, PLE_DIM), lambda b, t: (layer, b * nt + t, 0)),
                  full(woa), full(wob), full(g2), full(wg), full(wple)],
        out_specs=row(D_MODEL),
        compiler_params=pltpu.CompilerParams(
            dimension_semantics=("arbitrary", "arbitrary"), vmem_limit_bytes=VMEM_LIMIT_BYTES),
        name="out_stage",
    )(h2, oat, ob, p3, woa, wob, g2, wg, wple)


def _layer(h2, p3, layer, pos_row, invf, tri, batch, seq, norm_g, w_in, b_f, qk_g, w_out, w_ple,
           ple_norm_g, w_ple_gate):
    d = D_BRANCH
    cuts = np.cumsum([0, d, d, d, d, N_HEADS, d, d, d, d])
    qa_w, ka_w, va_w, ga_w, f_w, qb_w, kb_w, vb_w, gb_w = (
        w_in[:, cuts[i]:cuts[i + 1]] for i in range(9))
    f_w = jnp.pad(f_w, ((0, 0), (0, LANES - N_HEADS)))
    wc = jnp.concatenate([qa_w, ka_w, va_w, ga_w, qb_w, kb_w, f_w], axis=1).astype(BF16)
    wn = jnp.concatenate([vb_w, gb_w], axis=1).astype(BF16)
    bf_col = jnp.pad(b_f.astype(F32), (0, F_ROWS - N_HEADS)).reshape(F_ROWS, 1)
    scale = HEAD_DIM ** -0.5
    gcol = jnp.concatenate([jnp.tile(qk_g[0] * (scale * LOG2E), N_HEADS), jnp.tile(qk_g[1], N_HEADS),
                            jnp.tile(qk_g[2] * (scale * LOG2E), N_HEADS), jnp.tile(qk_g[3], N_HEADS)]
                           ).astype(F32).reshape(4 * d, 1)

    qt, k, vt, gt, qb, kb, vb, gb, stats = _proj(
        h2, pos_row, batch, seq, norm_g.reshape(1, D_MODEL), wc, wn, gcol, bf_col, invf, tri)

    oat = _fox(_fox_plan(stats, batch, seq), qt, k, vt, gt).reshape(batch, d, seq)
    as3 = lambda a: a.reshape(batch, seq, d)
    ob = _dilated(as3(qb), as3(kb), as3(vb), as3(gb)).reshape(batch * seq, d)

    w_out_b = w_out.astype(BF16)
    return _post(h2, oat, ob, p3, layer, batch, seq, w_out_b[:d], w_out_b[d:],
                 ple_norm_g.reshape(1, D_MODEL), w_ple_gate.astype(BF16), w_ple.astype(BF16))


def kernel(x, p, positions, norm_g, w_in, b_f, qk_norm_g, w_out, w_ple, ple_norm_g, w_ple_gate):
    batch, seq, _ = x.shape
    depth = p.shape[0]
    assert x.shape[2] == D_MODEL and p.shape[1:] == (batch, seq, PLE_DIM)
    assert all(w // dil == N_BACK for w, dil in DILATED_PATTERNS)
    assert all(seq % tile == 0 for tile in (ROW_TILE, POST_TILE, FOX_T, DIL_SUPER,
                                            DIL_SPLIT * DIL_COPY))
    assert FOX_UK == FOX_UQ and FOX_T % FOX_UQ == 0 and ROW_TILE % FOX_UQ == 0
    assert FOX_T % ROW_TILE == 0 and DIL_SUPER % (DIL_GROUP * DIL_TQ) == 0
    pos_row = positions.reshape(1, batch * seq)
    inv_freq = ROPE_THETA ** (-jnp.arange(ROPE_HALF, dtype=F32) / ROPE_HALF)
    invf = inv_freq.reshape(ROPE_HALF, 1)
    idx = np.arange(ROW_TILE)
    tri = jnp.asarray(idx[:, None] <= idx[None, :], BF16)
    h2 = x.reshape(batch * seq, D_MODEL)
    p3 = p.reshape(depth, batch * seq, PLE_DIM)
    for i in range(depth):
        h2 = _layer(h2, p3, i, pos_row, invf, tri, batch, seq,
                    norm_g[i], w_in[i], b_f[i], qk_norm_g[i], w_out[i], w_ple[i],
                    ple_norm_g[i], w_ple_gate[i])
    return h2.reshape(batch, seq, D_MODEL)
```

```python
import functools

import numpy as np
import jax
import jax.numpy as jnp
from jax import lax
from jax.experimental import pallas as pl
from jax.experimental.pallas import tpu as pltpu

D_MODEL = 1024
HEAD_DIM = 64
N_HEADS = 8
D_BRANCH = N_HEADS * HEAD_DIM
PLE_DIM = 256
ROPE_THETA = 500000.0
ROPE_DIM = HEAD_DIM // 4
ROPE_HALF = ROPE_DIM // 2
DILATED_PATTERNS = ((128, 1), (512, 4), (2048, 16))
N_BACK = 128
EPS = 1e-6
NEG = -1e30

LANES = 128
SUBLANES = 8
BF16_ROWS = 16
VMEM_LIMIT_BYTES = 56 * 1024 * 1024

ROW_TILE = 512
POST_TILE = 1024
FOX_T = 1024
FOX_UK = 256
FOX_UQ = 256
FOX_AHEAD = 8
FOX_VROWS = HEAD_DIM + BF16_ROWS
FOX_SKIP_BITS = 136.0
FOX_EDGE_OFFSETS = (None, 0, 1, 2)
LOG2E = 1.4426950408889634
DIL_TQ = N_BACK
DIL_TW = DIL_TQ + N_BACK
DIL_SUPER = max(d for _, d in DILATED_PATTERNS) * DIL_TQ
DIL_SPLIT = 4
DIL_COPY = 512
DIL_GROUP = 8
DIL_AHEAD = 4
F_ROWS = BF16_ROWS

F32 = jnp.float32
BF16 = jnp.bfloat16


def _dot(a, b):
    return jnp.dot(a, b, preferred_element_type=F32)


def _dot_nt(a, b):
    return lax.dot_general(a, b, (((1,), (1,)), ((), ())), preferred_element_type=F32)


def _dot_tn(a, b):
    return lax.dot_general(a, b, (((0,), (0,)), ((), ())), preferred_element_type=F32)


def _split3(x):
    hi = x.astype(BF16).astype(F32)
    r = x - hi
    mid = r.astype(BF16).astype(F32)
    lo = (r - mid).astype(BF16).astype(F32)
    return hi, mid, lo


def _rms_norm_rows(h, g_row):
    ms = jnp.mean(h * h, axis=-1, keepdims=True)
    return h * lax.rsqrt(ms + EPS) * g_row


def _proj_kernel(h_ref, pos_ref, g1_ref, wc_ref, wn_ref, gcol_ref, bf_ref, invf_ref, tri_ref,
                 qt_ref, k_ref, vt_ref, gt_ref, qb_ref, kb_ref, vb_ref, gb_ref, stats_ref,
                 carry_ref, wt_ref):
    t = pl.program_id(1)
    tm = ROW_TILE
    d = D_BRANCH

    @pl.when((pl.program_id(0) == 0) & (t == 0))
    def _():
        for g in range(6):
            wt_ref[g * d:(g + 1) * d, :] = wc_ref[:, g * d:(g + 1) * d].astype(F32).T.astype(BF16)
        f_cols = wc_ref[:, 6 * d:6 * d + LANES].astype(F32).T
        wt_ref[6 * d:6 * d + F_ROWS, :] = f_cols[:F_ROWS].astype(BF16)

    u = _rms_norm_rows(h_ref[...], g1_ref[...]).astype(BF16)

    def feat_major(group):
        return _dot_nt(wt_ref[group * d:(group + 1) * d, :], u)

    zn = _dot(u, wn_ref[...])
    f = _dot_nt(wt_ref[6 * d:6 * d + F_ROWS, :], u) + bf_ref[...]
    z_qa = feat_major(0)

    vb_ref[...] = zn[:, :d]
    gate_b = zn[:, d:]
    gb_ref[...] = (gate_b * jax.nn.sigmoid(gate_b)).astype(BF16)
    z_ka = feat_major(1)

    def head_norm(z, group):
        z3 = z.reshape(N_HEADS, HEAD_DIM, tm)
        ss = jnp.sum(z3 * z3, axis=1, keepdims=True)
        y3 = z3 * lax.rsqrt(ss * (1.0 / HEAD_DIM) + EPS)
        return y3.reshape(d, tm) * gcol_ref[group * d:(group + 1) * d, :]

    ang = invf_ref[...] * pos_ref[...].astype(F32)
    cos = jnp.cos(ang)
    sin = jnp.sin(ang)

    def rope(y):
        y3 = y.reshape(N_HEADS, HEAD_DIM, tm)
        x1 = y3[:, :ROPE_HALF, :]
        x2 = y3[:, ROPE_HALF:ROPE_DIM, :]
        rot = [x1 * cos - x2 * sin, x2 * cos + x1 * sin, y3[:, ROPE_DIM:, :]]
        return jnp.concatenate(rot, axis=1).reshape(d, tm)

    lf = jnp.minimum(f, 0.0) - jnp.log1p(jnp.exp(-jnp.abs(f)))
    stacked = jnp.concatenate([x.astype(BF16) for x in _split3(lf)], axis=0)
    cs = _dot(stacked, tri_ref[...])
    carry = jnp.where(t == 0, 0.0, carry_ref[:, 0:1])
    c = cs[:F_ROWS] + cs[F_ROWS:2 * F_ROWS] + cs[2 * F_ROWS:] + carry
    carry_ref[...] = jnp.broadcast_to(c[:, tm - 1:tm], (F_ROWS, LANES))
    c2 = c * LOG2E

    qa = head_norm(z_qa, 0)
    z_va = feat_major(2)
    ka = head_norm(z_ka, 1)
    z_ga = feat_major(3)
    row = lax.broadcasted_iota(jnp.int32, (SUBLANES, tm), 0)
    pad = jnp.zeros((HEAD_DIM - SUBLANES, tm), F32)
    for hd in range(N_HEADS):
        c_hi, c_mid, c_lo = _split3(c2[hd:hd + 1, :])
        q_aug = jnp.where(row == 0, c_hi, jnp.where(row == 1, c_mid, jnp.where(
            row == 2, c_lo, jnp.where(row < 6, 1.0, 0.0))))
        k_aug = jnp.where(row < 3, 1.0, jnp.where(row == 3, -c_hi, jnp.where(
            row == 4, -c_mid, jnp.where(row == 5, -c_lo, 0.0))))
        rows = slice(hd * HEAD_DIM, (hd + 1) * HEAD_DIM)
        qt_ref[0, hd] = jnp.concatenate([qa[rows], q_aug, pad], axis=0).astype(BF16)
        k_ref[0, hd] = jnp.concatenate([ka[rows], k_aug, pad], axis=0).T.astype(BF16)

    ones_row = jnp.where(lax.broadcasted_iota(jnp.int32, (N_HEADS, BF16_ROWS, tm), 1) == 0,
                         1.0, 0.0)
    z_qb = feat_major(4)
    va = z_va.reshape(N_HEADS, HEAD_DIM, tm)
    vt_ref[0, :, 0] = jnp.concatenate([va, ones_row], axis=1).astype(BF16)
    z_kb = feat_major(5)
    gt_ref[0] = (z_ga * jax.nn.sigmoid(z_ga)).reshape(N_HEADS, HEAD_DIM, tm).astype(BF16)

    qb_ref[...] = rope(head_norm(z_qb, 2)).T
    kb_ref[...] = rope(head_norm(z_kb, 3)).T

    def max_sq_norm(y):
        y3 = y.reshape(N_HEADS, HEAD_DIM, tm)
        return jnp.max(jnp.sum(y3 * y3, axis=1), axis=-1, keepdims=True)

    lane = lax.broadcasted_iota(jnp.int32, (N_HEADS, LANES), 1)
    stats = jnp.where(lane == 0, max_sq_norm(qa), jnp.where(lane == 1, max_sq_norm(ka), 0.0))
    for sub in range(tm // FOX_UQ):
        lo = sub * FOX_UQ
        hi = lo + FOX_UQ - 1
        stats = jnp.where(lane == 2 + 2 * sub, c2[:N_HEADS, lo:lo + 1],
                          jnp.where(lane == 3 + 2 * sub, c2[:N_HEADS, hi:hi + 1], stats))
    stats_ref[0] = stats


def _proj(h2, pos_row, batch, seq, g1, wc, wn, gcol, bf_col, invf, tri):
    nt = seq // ROW_TILE
    rows = batch * seq
    d = D_BRANCH
    row = lambda w: pl.BlockSpec((ROW_TILE, w), lambda b, t: (b * nt + t, 0))
    full = lambda a: pl.BlockSpec(a.shape, lambda b, t: (0,) * a.ndim)
    once = lambda a: pl.BlockSpec(a.shape, lambda b, t: (0,) * a.ndim,
                                  pipeline_mode=pl.Buffered(1))
    out_shape = (
        jax.ShapeDtypeStruct((batch, N_HEADS, 2 * HEAD_DIM, seq), BF16),
        jax.ShapeDtypeStruct((batch, N_HEADS, seq, 2 * HEAD_DIM), BF16),
        jax.ShapeDtypeStruct((batch, N_HEADS, nt, FOX_VROWS, ROW_TILE), BF16),
        jax.ShapeDtypeStruct((batch, N_HEADS, HEAD_DIM, seq), BF16),
        jax.ShapeDtypeStruct((rows, d), F32),
        jax.ShapeDtypeStruct((rows, d), F32),
        jax.ShapeDtypeStruct((rows, d), F32),
        jax.ShapeDtypeStruct((rows, d), BF16),
        jax.ShapeDtypeStruct((batch * nt, N_HEADS, LANES), F32),
    )
    out_specs = (
        pl.BlockSpec((1, N_HEADS, 2 * HEAD_DIM, ROW_TILE), lambda b, t: (b, 0, 0, t)),
        pl.BlockSpec((1, N_HEADS, ROW_TILE, 2 * HEAD_DIM), lambda b, t: (b, 0, t, 0)),
        pl.BlockSpec((1, N_HEADS, 1, FOX_VROWS, ROW_TILE), lambda b, t: (b, 0, t, 0, 0)),
        pl.BlockSpec((1, N_HEADS, HEAD_DIM, ROW_TILE), lambda b, t: (b, 0, 0, t)),
        row(d), row(d), row(d), row(d),
        pl.BlockSpec((1, N_HEADS, LANES), lambda b, t: (b * nt + t, 0, 0)),
    )
    return pl.pallas_call(
        _proj_kernel,
        out_shape=out_shape,
        grid=(batch, nt),
        in_specs=[row(D_MODEL), pl.BlockSpec((1, ROW_TILE), lambda b, t: (0, b * nt + t)),
                  full(g1), once(wc), full(wn), full(gcol), full(bf_col), full(invf), full(tri)],
        out_specs=out_specs,
        scratch_shapes=[pltpu.VMEM((F_ROWS, LANES), F32),
                        pltpu.VMEM((6 * d + F_ROWS, D_MODEL), BF16)],
        compiler_params=pltpu.CompilerParams(
            dimension_semantics=("arbitrary", "arbitrary"), vmem_limit_bytes=VMEM_LIMIT_BYTES),
        name="in_proj",
    )(h2, pos_row, g1, wc, wn, gcol, bf_col, invf, tri)


def _fox_kernel(plan_ref, q_ref, k_ref, v_ref, g_ref, o_ref):
    i = pl.program_id(2)
    plan = plan_ref[(pl.program_id(0) * N_HEADS + pl.program_id(1)) * pl.num_programs(2) + i]
    first = plan // len(FOX_EDGE_OFFSETS)
    edge = plan % len(FOX_EDGE_OFFSETS)
    uk, uq = FOX_UK, FOX_UQ
    n_q = FOX_T // uq
    q = [q_ref[0, 0, :, a * uq:(a + 1) * uq] for a in range(n_q)]
    rel = (lax.broadcasted_iota(jnp.int32, (uk, uq), 0)
           - lax.broadcasted_iota(jnp.int32, (uk, uq), 1))

    def scores(j, ks, a):
        start = pl.multiple_of(j * FOX_T, FOX_T)
        return _dot(k_ref[0, 0, pl.ds(start + ks * uk, uk), :], q[a])

    def consume(s, v_sub, state, max_rel):
        m, acc = state
        if max_rel is not None:
            s = jnp.where(rel <= max_rel, s, NEG)
        m_new = jnp.maximum(m, jnp.max(s, axis=0, keepdims=True))
        alpha = jnp.exp2(m - m_new)
        p = jnp.exp2((s - m_new).astype(BF16))
        acc = alpha * acc + _dot(v_sub, p)
        return m_new, acc

    def step(j, carry, diagonal=False, min_offset=None):
        s_first, states = carry
        states = list(states)
        units = []
        for ks in range(FOX_T // uk):
            for a in range(n_q):
                max_rel = None
                if min_offset is not None and ks - a < min_offset:
                    continue
                if diagonal:
                    if ks * uk >= (a + 1) * uq:
                        continue
                    if (ks + 1) * uk - 1 > a * uq:
                        max_rel = a * uq - ks * uk
                units.append((ks, a, max_rel))
        pending = {0: scores(j, units[0][0], units[0][1]) if s_first is None else s_first}
        queue = list(range(1, len(units))) + ([] if diagonal else ["next"])
        result = {}

        def issue(item):
            if item == "next":
                result["next"] = scores(j + 1, 0, 0)
            else:
                pending[item] = scores(j, units[item][0], units[item][1])

        for item in queue[:FOX_AHEAD - 1]:
            issue(item)
        for n, (ks, a, max_rel) in enumerate(units):
            if n + FOX_AHEAD - 1 < len(queue):
                issue(queue[n + FOX_AHEAD - 1])
            chunk, off = divmod(ks * uk, ROW_TILE)
            v_sub = v_ref[0, 0, j * (FOX_T // ROW_TILE) + chunk, :, off:off + uk]
            states[a] = consume(pending.pop(n), v_sub, states[a], max_rel)
        return result.get("next"), tuple(states)

    init = tuple((jnp.full((1, uq), NEG, F32), jnp.zeros((FOX_VROWS, uq), F32))
                 for _ in range(n_q))
    branches = [lambda: (scores(i, 0, 0), init)]
    for offset in FOX_EDGE_OFFSETS:
        branches.append(functools.partial(step, first, (None, init), min_offset=offset))
    carry = lax.switch(jnp.where(first >= i, 0, edge + 1), branches)
    carry = lax.fori_loop(first + 1, i, step, carry)
    _, states = step(i, carry, diagonal=True)
    out = jnp.concatenate([acc[:HEAD_DIM] / acc[HEAD_DIM:HEAD_DIM + 1] for _, acc in states],
                          axis=1)
    o_ref[0, 0] = (out * g_ref[0, 0].astype(F32)).astype(BF16)


def _fox_plan(stats, batch, seq):
    nt, nq = seq // ROW_TILE, seq // FOX_T
    sub_t, sub_q = ROW_TILE // FOX_UQ, FOX_T // FOX_UQ
    st = stats.reshape(batch, nt, N_HEADS, LANES)
    q_sq = st[..., 0].reshape(batch, nq, nt // nq, N_HEADS).max(axis=2)
    k_sq = st[..., 1].max(axis=1, keepdims=True)
    bound = jnp.sqrt(q_sq * k_sq) * 1.01 + 1.0
    edges = st[..., 2:2 + 2 * sub_t].reshape(batch, nt, N_HEADS, sub_t, 2)
    edges = edges.transpose(0, 1, 3, 2, 4).reshape(batch, nq, sub_q, N_HEADS, 2)
    c_first, c_last = edges[..., 0], edges[..., 1]
    gap = c_first[:, :, :, None, None, :] - c_last[:, None, None, :, :, :]
    skip = (2.0 * bound[:, :, None, None, None, :] + gap) < -FOX_SKIP_BITS
    steps = jnp.arange(nq, dtype=jnp.int32)
    step_skip = jnp.all(skip, axis=(2, 4))
    first = jnp.min(jnp.where(step_skip, nq, steps[None, None, :, None]), axis=2)
    first = jnp.minimum(first, steps[None, :, None])
    other_step = steps[None, None, None, :, None, None] != first[:, :, None, None, None, :]
    at_first = jnp.all(skip | other_step, axis=3)
    offset = (jnp.arange(sub_q)[None, :] - jnp.arange(sub_q)[:, None])
    variant = jnp.zeros_like(first)
    for v, min_offset in enumerate(FOX_EDGE_OFFSETS):
        if min_offset is not None:
            dropped = (offset < min_offset)[None, None, :, :, None]
            ok = jnp.all(at_first | ~dropped, axis=(2, 3))
            variant = jnp.where(ok, v, variant)
    plan = first * len(FOX_EDGE_OFFSETS) + variant
    return plan.transpose(0, 2, 1).reshape(-1).astype(jnp.int32)


def _fox(first, qt, k, vt, gt):
    b, _, _, s = qt.shape
    q_spec = pl.BlockSpec((1, 1, 2 * HEAD_DIM, FOX_T), lambda bi, h, i, f: (bi, h, 0, i))
    k_spec = pl.BlockSpec((1, 1, s, 2 * HEAD_DIM), lambda bi, h, i, f: (bi, h, 0, 0))
    v_spec = pl.BlockSpec((1, 1, s // ROW_TILE, FOX_VROWS, ROW_TILE),
                          lambda bi, h, i, f: (bi, h, 0, 0, 0))
    o_spec = pl.BlockSpec((1, 1, HEAD_DIM, FOX_T), lambda bi, h, i, f: (bi, h, 0, i))
    return pl.pallas_call(
        _fox_kernel,
        out_shape=jax.ShapeDtypeStruct((b, N_HEADS, HEAD_DIM, s), BF16),
        grid_spec=pltpu.PrefetchScalarGridSpec(
            num_scalar_prefetch=1,
            grid=(b, N_HEADS, s // FOX_T),
            in_specs=[q_spec, k_spec, v_spec, o_spec],
            out_specs=o_spec),
        compiler_params=pltpu.CompilerParams(
            dimension_semantics=("arbitrary",) * 3, vmem_limit_bytes=VMEM_LIMIT_BYTES),
        name="fox_attention",
    )(first, qt, k, vt, gt)


def _dil_kernel(q_ref, k_ref, v_ref, g_ref, o_ref, q4_s, k4_s, v4_s, acc_s, m_s, l_s, *, seq):
    tq, tw, sup = DIL_TQ, DIL_TW, DIL_SUPER
    part = seq // DIL_SPLIT

    def regroup(c, carry):
        for src, dst in ((q_ref, q4_s), (k_ref, k4_s), (v_ref, v4_s)):
            for r in range(DIL_SPLIT):
                rows = src[0, pl.ds(c * DIL_SPLIT * DIL_COPY + r, DIL_COPY, stride=DIL_SPLIT), :]
                dst[pl.ds(r * part + c * DIL_COPY, DIL_COPY), :] = rows
        return carry

    lax.fori_loop(0, part // DIL_COPY, regroup, 0)

    def stream_rows(nat_ref, split_ref, d, r, pos, n):
        if d % DIL_SPLIT:
            x = nat_ref[0, pl.ds(d * pos + r, n, stride=d), :]
        else:
            sd = d // DIL_SPLIT
            start = (r % DIL_SPLIT) * part + sd * pos + r // DIL_SPLIT
            x = split_ref[pl.ds(start, n, stride=sd), :]
        return x.astype(BF16)

    lane = lax.broadcasted_iota(jnp.int32, (1, LANES), 1)
    low = lane < HEAD_DIM
    rel = (lax.broadcasted_iota(jnp.int32, (tq, tw), 0)
           - lax.broadcasted_iota(jnp.int32, (tq, tw), 1))

    def band_bias(q_minus_w):
        dist = rel + q_minus_w
        return jnp.where(dist >= 0, jnp.where(dist <= N_BACK, 0.0, NEG), NEG)

    bias_inner = band_bias(N_BACK)
    bias_start = band_bias(0)

    def issue(d, base, idx, nblk):
        r = idx // nblk
        jb = idx % nblk
        qs = base // d + jb * tq
        ws = jnp.maximum(qs - N_BACK, 0)
        q = stream_rows(q_ref, q4_s, d, r, qs, tq)
        kw = stream_rows(k_ref, k4_s, d, r, ws, tw)
        zero = jnp.zeros_like(q)
        q2 = jnp.concatenate([jnp.where(low, q, zero), jnp.where(low, zero, q)], axis=0)
        return _dot_nt(q2, kw), r, qs, ws, d * (qs - base // d) + r

    def consume(g, d, s, r, qs, ws, out_row):
        vw = stream_rows(v_ref, v4_s, d, r, ws, tw)
        bias = jnp.where(qs == ws, bias_start, bias_inner)
        ps, ms = [], []
        for hh in range(2):
            sm = s[hh * tq:(hh + 1) * tq] + bias
            m = jnp.max(sm, axis=-1, keepdims=True)
            ps.append(jnp.exp2((sm - m).astype(BF16)))
            ms.append(jnp.broadcast_to(m, (tq, LANES)))
        vw1 = jnp.concatenate([vw, jnp.ones_like(vw)], axis=1)
        acc = _dot(jnp.concatenate(ps, axis=0), vw1)
        rows = pl.ds(out_row, tq, stride=d)
        acc_s[g, rows, :] = jnp.where(low, acc[:tq, :LANES], acc[tq:, :LANES])
        m_s[g, rows, :] = jnp.where(low, ms[0], ms[1])
        l_s[g, rows, :] = jnp.where(low, acc[:tq, LANES:], acc[tq:, LANES:])

    def superblock(sb, carry):
        base = pl.multiple_of(sb * sup, sup)
        for g, (_, d) in enumerate(DILATED_PATTERNS):
            nblk = sup // (d * tq)

            def body(it, c, g=g, d=d, nblk=nblk):
                first = it * DIL_GROUP
                pending = {n: issue(d, base, first + n, nblk) for n in range(DIL_AHEAD)}
                for n in range(DIL_GROUP):
                    if n + DIL_AHEAD < DIL_GROUP:
                        pending[n + DIL_AHEAD] = issue(d, base, first + n + DIL_AHEAD, nblk)
                    consume(g, d, *pending.pop(n))
                return c

            lax.fori_loop(0, d * nblk // DIL_GROUP, body, 0)

        rows = pl.ds(base, sup)
        m1, m2, m3 = m_s[0], m_s[1], m_s[2]
        mx = jnp.maximum(jnp.maximum(m1, m2), m3)
        e1, e2, e3 = jnp.exp2(m1 - mx), jnp.exp2(m2 - mx), jnp.exp2(m3 - mx)
        num = e1 * acc_s[0] + e2 * acc_s[1] + e3 * acc_s[2]
        den = e1 * l_s[0] + e2 * l_s[1] + e3 * l_s[2]
        o_ref[0, rows, :] = (num / den * g_ref[0, rows, :].astype(F32)).astype(BF16)
        return carry

    lax.fori_loop(0, seq // sup, superblock, 0)


def _dilated(qb, kb, vb, gb):
    b, s, _ = qb.shape
    spec = pl.BlockSpec((1, s, LANES), lambda bi, p: (bi, 0, p))
    regrouped = pltpu.VMEM((s, LANES), F32)
    scratch = pltpu.VMEM((len(DILATED_PATTERNS), DIL_SUPER, LANES), F32)
    return pl.pallas_call(
        functools.partial(_dil_kernel, seq=s),
        out_shape=jax.ShapeDtypeStruct((b, s, D_BRANCH), BF16),
        grid=(b, D_BRANCH // LANES),
        in_specs=[spec, spec, spec, spec],
        out_specs=spec,
        scratch_shapes=[regrouped, regrouped, regrouped, scratch, scratch, scratch],
        compiler_params=pltpu.CompilerParams(
            dimension_semantics=("arbitrary",) * 2, vmem_limit_bytes=VMEM_LIMIT_BYTES),
        name="dilated_attn",
    )(qb, kb, vb, gb)


def _post_kernel(h_ref, oat_ref, ob_ref, p_ref, wo_ref, g2_ref, wg_ref, wple_ref,
                 out_ref, wo_s, wg_s, wple_s):
    @pl.when((pl.program_id(0) == 0) & (pl.program_id(1) == 0))
    def _():
        wo_s[...] = wo_ref[0].astype(BF16)
        wg_s[...] = wg_ref[0].astype(BF16)
        wple_s[...] = wple_ref[0].astype(BF16)

    d = D_BRANCH
    h1 = h_ref[...] + _dot_tn(oat_ref[0], wo_s[:d, :]) + _dot(ob_ref[...], wo_s[d:, :])
    u2 = _rms_norm_rows(h1, g2_ref[...]).astype(BF16)
    gate = jax.nn.sigmoid(_dot(u2, wg_s[...]))
    ple = _dot(p_ref[0].astype(BF16), wple_s[...])
    out_ref[...] = h1 + ple * gate


def _post(h2, oat, ob, p3, layer, batch, seq, w_out, g2, w_gate, w_ple):
    nt = seq // POST_TILE
    row = lambda w: pl.BlockSpec((POST_TILE, w), lambda b, t: (b * nt + t, 0))
    full = lambda a: pl.BlockSpec(a.shape, lambda b, t: (0,) * a.ndim)
    once = lambda a: pl.BlockSpec((1,) + a.shape[1:], lambda b, t: (layer, 0, 0),
                                  pipeline_mode=pl.Buffered(1))
    return pl.pallas_call(
        _post_kernel,
        out_shape=jax.ShapeDtypeStruct((batch * seq, D_MODEL), F32),
        grid=(batch, nt),
        in_specs=[row(D_MODEL), pl.BlockSpec((1, D_BRANCH, POST_TILE), lambda b, t: (b, 0, t)),
                  row(D_BRANCH),
                  pl.BlockSpec((1, POST_TILE, PLE_DIM), lambda b, t: (layer, b * nt + t, 0)),
                  once(w_out), full(g2), once(w_gate), once(w_ple)],
        out_specs=row(D_MODEL),
        scratch_shapes=[pltpu.VMEM(w_out.shape[1:], BF16), pltpu.VMEM(w_gate.shape[1:], BF16),
                        pltpu.VMEM(w_ple.shape[1:], BF16)],
        compiler_params=pltpu.CompilerParams(
            dimension_semantics=("arbitrary", "arbitrary"), vmem_limit_bytes=VMEM_LIMIT_BYTES),
        name="out_stage",
    )(h2, oat, ob, p3, w_out, g2, w_gate, w_ple)


def _layer(h2, p3, layer, pos_row, invf, tri, batch, seq, norm_g, w_in, b_f, qk_g, w_out, w_ple,
           ple_norm_g, w_ple_gate):
    d = D_BRANCH
    cuts = np.cumsum([0, d, d, d, d, N_HEADS, d, d, d, d])
    qa_w, ka_w, va_w, ga_w, f_w, qb_w, kb_w, vb_w, gb_w = (
        w_in[:, cuts[i]:cuts[i + 1]] for i in range(9))
    f_w = jnp.pad(f_w, ((0, 0), (0, LANES - N_HEADS)))
    wc = jnp.concatenate([qa_w, ka_w, va_w, ga_w, qb_w, kb_w, f_w], axis=1).astype(BF16)
    wn = jnp.concatenate([vb_w, gb_w], axis=1).astype(BF16)
    bf_col = jnp.pad(b_f.astype(F32), (0, F_ROWS - N_HEADS)).reshape(F_ROWS, 1)
    scale = HEAD_DIM ** -0.5
    gcol = jnp.concatenate([jnp.tile(qk_g[0] * (scale * LOG2E), N_HEADS), jnp.tile(qk_g[1], N_HEADS),
                            jnp.tile(qk_g[2] * (scale * LOG2E), N_HEADS), jnp.tile(qk_g[3], N_HEADS)]
                           ).astype(F32).reshape(4 * d, 1)

    qt, k, vt, gt, qb, kb, vb, gb, stats = _proj(
        h2, pos_row, batch, seq, norm_g.reshape(1, D_MODEL), wc, wn, gcol, bf_col, invf, tri)

    oat = _fox(_fox_plan(stats, batch, seq), qt, k, vt, gt).reshape(batch, d, seq)
    as3 = lambda a: a.reshape(batch, seq, d)
    ob = _dilated(as3(qb), as3(kb), as3(vb), as3(gb)).reshape(batch * seq, d)

    return _post(h2, oat, ob, p3, layer, batch, seq, w_out.astype(F32),
                 ple_norm_g[layer].reshape(1, D_MODEL), w_ple_gate.astype(F32),
                 w_ple.astype(F32))


def kernel(x, p, positions, norm_g, w_in, b_f, qk_norm_g, w_out, w_ple, ple_norm_g, w_ple_gate):
    batch, seq, _ = x.shape
    depth = p.shape[0]
    assert x.shape[2] == D_MODEL and p.shape[1:] == (batch, seq, PLE_DIM)
    assert all(w // dil == N_BACK for w, dil in DILATED_PATTERNS)
    assert all(seq % tile == 0 for tile in (ROW_TILE, POST_TILE, FOX_T, DIL_SUPER,
                                            DIL_SPLIT * DIL_COPY))
    assert FOX_UK == FOX_UQ and FOX_T % FOX_UQ == 0 and ROW_TILE % FOX_UQ == 0
    assert FOX_T % ROW_TILE == 0 and DIL_SUPER % (DIL_GROUP * DIL_TQ) == 0
    pos_row = positions.reshape(1, batch * seq)
    inv_freq = ROPE_THETA ** (-jnp.arange(ROPE_HALF, dtype=F32) / ROPE_HALF)
    invf = inv_freq.reshape(ROPE_HALF, 1)
    idx = np.arange(ROW_TILE)
    tri = jnp.asarray(idx[:, None] <= idx[None, :], BF16)
    h2 = x.reshape(batch * seq, D_MODEL)
    p3 = p.reshape(depth, batch * seq, PLE_DIM)
    for i in range(depth):
        h2 = _layer(h2, p3, i, pos_row, invf, tri, batch, seq,
                    norm_g[i], w_in[i], b_f[i], qk_norm_g[i], w_out, w_ple,
                    ple_norm_g, w_ple_gate)
    return h2.reshape(batch, seq, D_MODEL)
```

```python
import functools

import numpy as np
import jax
import jax.numpy as jnp
from jax import lax
from jax.experimental import pallas as pl
from jax.experimental.pallas import tpu as pltpu

D_MODEL = 1024
HEAD_DIM = 64
N_HEADS = 8
D_BRANCH = N_HEADS * HEAD_DIM
PLE_DIM = 256
ROPE_THETA = 500000.0
ROPE_DIM = HEAD_DIM // 4
ROPE_HALF = ROPE_DIM // 2
DILATED_PATTERNS = ((128, 1), (512, 4), (2048, 16))
N_BACK = 128
EPS = 1e-6
NEG = -1e30

LANES = 128
SUBLANES = 8
BF16_ROWS = 16
VMEM_LIMIT_BYTES = 56 * 1024 * 1024

ROW_TILE = 512
POST_TILE = 1024
FOX_T = 1024
FOX_UK = 256
FOX_UQ = 256
FOX_AHEAD = 8
FOX_VROWS = HEAD_DIM + BF16_ROWS
FOX_SKIP_BITS = 136.0
FOX_EDGE_OFFSETS = (None, 0, 1, 2)
LOG2E = 1.4426950408889634
DIL_TQ = N_BACK
DIL_TW = DIL_TQ + N_BACK
DIL_SUPER = max(d for _, d in DILATED_PATTERNS) * DIL_TQ
DIL_SPLIT = 4
DIL_COPY = 512
DIL_GROUP = 8
DIL_AHEAD = 4
F_ROWS = BF16_ROWS
W_IN_COLUMNS = dict(zip(
    ("qa", "ka", "va", "ga", "f", "qb", "kb", "vb", "gb"),
    np.cumsum([0] + [D_BRANCH] * 4 + [N_HEADS] + [D_BRANCH] * 3).tolist()))
N_IN = 8 * D_BRANCH + N_HEADS

F32 = jnp.float32
BF16 = jnp.bfloat16


def _dot(a, b):
    return jnp.dot(a, b, preferred_element_type=F32)


def _dot_nt(a, b):
    return lax.dot_general(a, b, (((1,), (1,)), ((), ())), preferred_element_type=F32)


def _dot_tn(a, b):
    return lax.dot_general(a, b, (((0,), (0,)), ((), ())), preferred_element_type=F32)


def _split3(x):
    hi = x.astype(BF16).astype(F32)
    r = x - hi
    mid = r.astype(BF16).astype(F32)
    lo = (r - mid).astype(BF16).astype(F32)
    return hi, mid, lo


def _rms_norm_rows(h, g_row):
    ms = jnp.mean(h * h, axis=-1, keepdims=True)
    return h * lax.rsqrt(ms + EPS) * g_row


def _proj_kernel(h_ref, pos_ref, g1_ref, win_ref, gcol_ref, bf_ref, invf_ref, tri_ref,
                 qt_ref, k_ref, vt_ref, gt_ref, qb_ref, kb_ref, vb_ref, gb_ref, stats_ref,
                 carry_ref, wt_ref, wn_ref):
    t = pl.program_id(1)
    tm = ROW_TILE
    d = D_BRANCH

    @pl.when((pl.program_id(0) == 0) & (t == 0))
    def _():
        def cols(name, width=d):
            lo = W_IN_COLUMNS[name]
            return win_ref[0, :, lo:lo + width]

        for g, name in enumerate(("qa", "ka", "va", "ga", "qb", "kb")):
            wt_ref[g * d:(g + 1) * d, :] = cols(name).T.astype(BF16)
        f_rows = cols("f", LANES).T[:F_ROWS]
        keep = lax.broadcasted_iota(jnp.int32, f_rows.shape, 0) < N_HEADS
        wt_ref[6 * d:6 * d + F_ROWS, :] = jnp.where(keep, f_rows, 0.0).astype(BF16)
        wn_ref[:, :d] = cols("vb").astype(BF16)
        wn_ref[:, d:] = cols("gb").astype(BF16)

    u = _rms_norm_rows(h_ref[...], g1_ref[...]).astype(BF16)

    def feat_major(group):
        return _dot_nt(wt_ref[group * d:(group + 1) * d, :], u)

    zn = _dot(u, wn_ref[...])
    f = _dot_nt(wt_ref[6 * d:6 * d + F_ROWS, :], u) + bf_ref[...]
    z_qa = feat_major(0)

    vb_ref[...] = zn[:, :d]
    gate_b = zn[:, d:]
    gb_ref[...] = (gate_b * jax.nn.sigmoid(gate_b)).astype(BF16)
    z_ka = feat_major(1)

    def head_norm(z, group):
        z3 = z.reshape(N_HEADS, HEAD_DIM, tm)
        ss = jnp.sum(z3 * z3, axis=1, keepdims=True)
        y3 = z3 * lax.rsqrt(ss * (1.0 / HEAD_DIM) + EPS)
        return y3.reshape(d, tm) * gcol_ref[group * d:(group + 1) * d, :]

    ang = invf_ref[...] * pos_ref[...].astype(F32)
    cos = jnp.cos(ang)
    sin = jnp.sin(ang)

    def rope(y):
        y3 = y.reshape(N_HEADS, HEAD_DIM, tm)
        x1 = y3[:, :ROPE_HALF, :]
        x2 = y3[:, ROPE_HALF:ROPE_DIM, :]
        rot = [x1 * cos - x2 * sin, x2 * cos + x1 * sin, y3[:, ROPE_DIM:, :]]
        return jnp.concatenate(rot, axis=1).reshape(d, tm)

    lf = jnp.minimum(f, 0.0) - jnp.log1p(jnp.exp(-jnp.abs(f)))
    stacked = jnp.concatenate([x.astype(BF16) for x in _split3(lf)], axis=0)
    cs = _dot(stacked, tri_ref[...])
    carry = jnp.where(t == 0, 0.0, carry_ref[:, 0:1])
    c = cs[:F_ROWS] + cs[F_ROWS:2 * F_ROWS] + cs[2 * F_ROWS:] + carry
    carry_ref[...] = jnp.broadcast_to(c[:, tm - 1:tm], (F_ROWS, LANES))
    c2 = c * LOG2E

    qa = head_norm(z_qa, 0)
    z_va = feat_major(2)
    ka = head_norm(z_ka, 1)
    z_ga = feat_major(3)
    row = lax.broadcasted_iota(jnp.int32, (SUBLANES, tm), 0)
    pad = jnp.zeros((HEAD_DIM - SUBLANES, tm), F32)
    for hd in range(N_HEADS):
        c_hi, c_mid, c_lo = _split3(c2[hd:hd + 1, :])
        q_aug = jnp.where(row == 0, c_hi, jnp.where(row == 1, c_mid, jnp.where(
            row == 2, c_lo, jnp.where(row < 6, 1.0, 0.0))))
        k_aug = jnp.where(row < 3, 1.0, jnp.where(row == 3, -c_hi, jnp.where(
            row == 4, -c_mid, jnp.where(row == 5, -c_lo, 0.0))))
        rows = slice(hd * HEAD_DIM, (hd + 1) * HEAD_DIM)
        qt_ref[0, hd] = jnp.concatenate([qa[rows], q_aug, pad], axis=0).astype(BF16)
        k_ref[0, hd] = jnp.concatenate([ka[rows], k_aug, pad], axis=0).T.astype(BF16)

    ones_row = jnp.where(lax.broadcasted_iota(jnp.int32, (N_HEADS, BF16_ROWS, tm), 1) == 0,
                         1.0, 0.0)
    z_qb = feat_major(4)
    va = z_va.reshape(N_HEADS, HEAD_DIM, tm)
    vt_ref[0, :, 0] = jnp.concatenate([va, ones_row], axis=1).astype(BF16)
    z_kb = feat_major(5)
    gt_ref[0] = (z_ga * jax.nn.sigmoid(z_ga)).reshape(N_HEADS, HEAD_DIM, tm).astype(BF16)

    qb_ref[...] = rope(head_norm(z_qb, 2)).T
    kb_ref[...] = rope(head_norm(z_kb, 3)).T

    def max_sq_norm(y):
        y3 = y.reshape(N_HEADS, HEAD_DIM, tm)
        return jnp.max(jnp.sum(y3 * y3, axis=1), axis=-1, keepdims=True)

    lane = lax.broadcasted_iota(jnp.int32, (N_HEADS, LANES), 1)
    stats = jnp.where(lane == 0, max_sq_norm(qa), jnp.where(lane == 1, max_sq_norm(ka), 0.0))
    for sub in range(tm // FOX_UQ):
        lo = sub * FOX_UQ
        hi = lo + FOX_UQ - 1
        stats = jnp.where(lane == 2 + 2 * sub, c2[:N_HEADS, lo:lo + 1],
                          jnp.where(lane == 3 + 2 * sub, c2[:N_HEADS, hi:hi + 1], stats))
    stats_ref[0] = stats


def _proj(h2, pos_row, layer, batch, seq, g1, w_in, gcol, bf_col, invf, tri):
    nt = seq // ROW_TILE
    rows = batch * seq
    d = D_BRANCH
    row = lambda w: pl.BlockSpec((ROW_TILE, w), lambda b, t: (b * nt + t, 0))
    full = lambda a: pl.BlockSpec(a.shape, lambda b, t: (0,) * a.ndim)
    w_spec = pl.BlockSpec((1,) + w_in.shape[1:], lambda b, t: (layer, 0, 0),
                          pipeline_mode=pl.Buffered(1))
    out_shape = (
        jax.ShapeDtypeStruct((batch, N_HEADS, 2 * HEAD_DIM, seq), BF16),
        jax.ShapeDtypeStruct((batch, N_HEADS, seq, 2 * HEAD_DIM), BF16),
        jax.ShapeDtypeStruct((batch, N_HEADS, nt, FOX_VROWS, ROW_TILE), BF16),
        jax.ShapeDtypeStruct((batch, N_HEADS, HEAD_DIM, seq), BF16),
        jax.ShapeDtypeStruct((rows, d), F32),
        jax.ShapeDtypeStruct((rows, d), F32),
        jax.ShapeDtypeStruct((rows, d), F32),
        jax.ShapeDtypeStruct((rows, d), BF16),
        jax.ShapeDtypeStruct((batch * nt, N_HEADS, LANES), F32),
    )
    out_specs = (
        pl.BlockSpec((1, N_HEADS, 2 * HEAD_DIM, ROW_TILE), lambda b, t: (b, 0, 0, t)),
        pl.BlockSpec((1, N_HEADS, ROW_TILE, 2 * HEAD_DIM), lambda b, t: (b, 0, t, 0)),
        pl.BlockSpec((1, N_HEADS, 1, FOX_VROWS, ROW_TILE), lambda b, t: (b, 0, t, 0, 0)),
        pl.BlockSpec((1, N_HEADS, HEAD_DIM, ROW_TILE), lambda b, t: (b, 0, 0, t)),
        row(d), row(d), row(d), row(d),
        pl.BlockSpec((1, N_HEADS, LANES), lambda b, t: (b * nt + t, 0, 0)),
    )
    return pl.pallas_call(
        _proj_kernel,
        out_shape=out_shape,
        grid=(batch, nt),
        in_specs=[row(D_MODEL), pl.BlockSpec((1, ROW_TILE), lambda b, t: (0, b * nt + t)),
                  full(g1), w_spec, full(gcol), full(bf_col), full(invf), full(tri)],
        out_specs=out_specs,
        scratch_shapes=[pltpu.VMEM((F_ROWS, LANES), F32),
                        pltpu.VMEM((6 * d + F_ROWS, D_MODEL), BF16),
                        pltpu.VMEM((D_MODEL, 2 * d), BF16)],
        compiler_params=pltpu.CompilerParams(
            dimension_semantics=("arbitrary", "arbitrary"), vmem_limit_bytes=VMEM_LIMIT_BYTES),
        name="in_proj",
    )(h2, pos_row, g1, w_in, gcol, bf_col, invf, tri)


def _fox_kernel(plan_ref, q_ref, k_ref, v_ref, g_ref, o_ref):
    i = pl.program_id(2)
    plan = plan_ref[(pl.program_id(0) * N_HEADS + pl.program_id(1)) * pl.num_programs(2) + i]
    first = plan // len(FOX_EDGE_OFFSETS)
    edge = plan % len(FOX_EDGE_OFFSETS)
    uk, uq = FOX_UK, FOX_UQ
    n_q = FOX_T // uq
    q = [q_ref[0, 0, :, a * uq:(a + 1) * uq] for a in range(n_q)]
    rel = (lax.broadcasted_iota(jnp.int32, (uk, uq), 0)
           - lax.broadcasted_iota(jnp.int32, (uk, uq), 1))

    def scores(j, ks, a):
        start = pl.multiple_of(j * FOX_T, FOX_T)
        return _dot(k_ref[0, 0, pl.ds(start + ks * uk, uk), :], q[a])

    def consume(s, v_sub, state, max_rel):
        m, acc = state
        if max_rel is not None:
            s = jnp.where(rel <= max_rel, s, NEG)
        m_new = jnp.maximum(m, jnp.max(s, axis=0, keepdims=True))
        alpha = jnp.exp2(m - m_new)
        p = jnp.exp2((s - m_new).astype(BF16))
        acc = alpha * acc + _dot(v_sub, p)
        return m_new, acc

    def step(j, carry, diagonal=False, min_offset=None):
        s_first, states = carry
        states = list(states)
        units = []
        for ks in range(FOX_T // uk):
            for a in range(n_q):
                max_rel = None
                if min_offset is not None and ks - a < min_offset:
                    continue
                if diagonal:
                    if ks * uk >= (a + 1) * uq:
                        continue
                    if (ks + 1) * uk - 1 > a * uq:
                        max_rel = a * uq - ks * uk
                units.append((ks, a, max_rel))
        pending = {0: scores(j, units[0][0], units[0][1]) if s_first is None else s_first}
        queue = list(range(1, len(units))) + ([] if diagonal else ["next"])
        result = {}

        def issue(item):
            if item == "next":
                result["next"] = scores(j + 1, 0, 0)
            else:
                pending[item] = scores(j, units[item][0], units[item][1])

        for item in queue[:FOX_AHEAD - 1]:
            issue(item)
        for n, (ks, a, max_rel) in enumerate(units):
            if n + FOX_AHEAD - 1 < len(queue):
                issue(queue[n + FOX_AHEAD - 1])
            chunk, off = divmod(ks * uk, ROW_TILE)
            v_sub = v_ref[0, 0, j * (FOX_T // ROW_TILE) + chunk, :, off:off + uk]
            states[a] = consume(pending.pop(n), v_sub, states[a], max_rel)
        return result.get("next"), tuple(states)

    init = tuple((jnp.full((1, uq), NEG, F32), jnp.zeros((FOX_VROWS, uq), F32))
                 for _ in range(n_q))
    branches = [lambda: (scores(i, 0, 0), init)]
    for offset in FOX_EDGE_OFFSETS:
        branches.append(functools.partial(step, first, (None, init), min_offset=offset))
    carry = lax.switch(jnp.where(first >= i, 0, edge + 1), branches)
    carry = lax.fori_loop(first + 1, i, step, carry)
    _, states = step(i, carry, diagonal=True)
    out = jnp.concatenate([acc[:HEAD_DIM] / acc[HEAD_DIM:HEAD_DIM + 1] for _, acc in states],
                          axis=1)
    o_ref[0, 0] = (out * g_ref[0, 0].astype(F32)).astype(BF16)


def _fox_plan(stats, batch, seq):
    nt, nq = seq // ROW_TILE, seq // FOX_T
    sub_t, sub_q = ROW_TILE // FOX_UQ, FOX_T // FOX_UQ
    st = stats.reshape(batch, nt, N_HEADS, LANES)
    q_sq = st[..., 0].reshape(batch, nq, nt // nq, N_HEADS).max(axis=2)
    k_sq = st[..., 1].max(axis=1, keepdims=True)
    bound = jnp.sqrt(q_sq * k_sq) * 1.01 + 1.0
    edges = st[..., 2:2 + 2 * sub_t].reshape(batch, nt, N_HEADS, sub_t, 2)
    edges = edges.transpose(0, 1, 3, 2, 4).reshape(batch, nq, sub_q, N_HEADS, 2)
    c_first, c_last = edges[..., 0], edges[..., 1]
    gap = c_first[:, :, :, None, None, :] - c_last[:, None, None, :, :, :]
    skip = (2.0 * bound[:, :, None, None, None, :] + gap) < -FOX_SKIP_BITS
    steps = jnp.arange(nq, dtype=jnp.int32)
    step_skip = jnp.all(skip, axis=(2, 4))
    first = jnp.min(jnp.where(step_skip, nq, steps[None, None, :, None]), axis=2)
    first = jnp.minimum(first, steps[None, :, None])
    other_step = steps[None, None, None, :, None, None] != first[:, :, None, None, None, :]
    at_first = jnp.all(skip | other_step, axis=3)
    offset = (jnp.arange(sub_q)[None, :] - jnp.arange(sub_q)[:, None])
    variant = jnp.zeros_like(first)
    for v, min_offset in enumerate(FOX_EDGE_OFFSETS):
        if min_offset is not None:
            dropped = (offset < min_offset)[None, None, :, :, None]
            ok = jnp.all(at_first | ~dropped, axis=(2, 3))
            variant = jnp.where(ok, v, variant)
    plan = first * len(FOX_EDGE_OFFSETS) + variant
    return plan.transpose(0, 2, 1).reshape(-1).astype(jnp.int32)


def _fox(first, qt, k, vt, gt):
    b, _, _, s = qt.shape
    q_spec = pl.BlockSpec((1, 1, 2 * HEAD_DIM, FOX_T), lambda bi, h, i, f: (bi, h, 0, i))
    k_spec = pl.BlockSpec((1, 1, s, 2 * HEAD_DIM), lambda bi, h, i, f: (bi, h, 0, 0))
    v_spec = pl.BlockSpec((1, 1, s // ROW_TILE, FOX_VROWS, ROW_TILE),
                          lambda bi, h, i, f: (bi, h, 0, 0, 0))
    o_spec = pl.BlockSpec((1, 1, HEAD_DIM, FOX_T), lambda bi, h, i, f: (bi, h, 0, i))
    return pl.pallas_call(
        _fox_kernel,
        out_shape=jax.ShapeDtypeStruct((b, N_HEADS, HEAD_DIM, s), BF16),
        grid_spec=pltpu.PrefetchScalarGridSpec(
            num_scalar_prefetch=1,
            grid=(b, N_HEADS, s // FOX_T),
            in_specs=[q_spec, k_spec, v_spec, o_spec],
            out_specs=o_spec),
        compiler_params=pltpu.CompilerParams(
            dimension_semantics=("arbitrary",) * 3, vmem_limit_bytes=VMEM_LIMIT_BYTES),
        name="fox_attention",
    )(first, qt, k, vt, gt)


def _dil_kernel(q_ref, k_ref, v_ref, g_ref, o_ref, q4_s, k4_s, v4_s, acc_s, m_s, l_s, *, seq):
    tq, tw, sup = DIL_TQ, DIL_TW, DIL_SUPER
    part = seq // DIL_SPLIT

    def regroup(c, carry):
        for src, dst in ((q_ref, q4_s), (k_ref, k4_s), (v_ref, v4_s)):
            for r in range(DIL_SPLIT):
                rows = src[0, pl.ds(c * DIL_SPLIT * DIL_COPY + r, DIL_COPY, stride=DIL_SPLIT), :]
                dst[pl.ds(r * part + c * DIL_COPY, DIL_COPY), :] = rows
        return carry

    lax.fori_loop(0, part // DIL_COPY, regroup, 0)

    def stream_rows(nat_ref, split_ref, d, r, pos, n):
        if d % DIL_SPLIT:
            x = nat_ref[0, pl.ds(d * pos + r, n, stride=d), :]
        else:
            sd = d // DIL_SPLIT
            start = (r % DIL_SPLIT) * part + sd * pos + r // DIL_SPLIT
            x = split_ref[pl.ds(start, n, stride=sd), :]
        return x.astype(BF16)

    lane = lax.broadcasted_iota(jnp.int32, (1, LANES), 1)
    low = lane < HEAD_DIM
    rel = (lax.broadcasted_iota(jnp.int32, (tq, tw), 0)
           - lax.broadcasted_iota(jnp.int32, (tq, tw), 1))

    def band_bias(q_minus_w):
        dist = rel + q_minus_w
        return jnp.where(dist >= 0, jnp.where(dist <= N_BACK, 0.0, NEG), NEG)

    bias_inner = band_bias(N_BACK)
    bias_start = band_bias(0)

    def issue(d, base, idx, nblk):
        r = idx // nblk
        jb = idx % nblk
        qs = base // d + jb * tq
        ws = jnp.maximum(qs - N_BACK, 0)
        q = stream_rows(q_ref, q4_s, d, r, qs, tq)
        kw = stream_rows(k_ref, k4_s, d, r, ws, tw)
        zero = jnp.zeros_like(q)
        q2 = jnp.concatenate([jnp.where(low, q, zero), jnp.where(low, zero, q)], axis=0)
        return _dot_nt(q2, kw), r, qs, ws, d * (qs - base // d) + r

    def consume(g, d, s, r, qs, ws, out_row):
        vw = stream_rows(v_ref, v4_s, d, r, ws, tw)
        bias = jnp.where(qs == ws, bias_start, bias_inner)
        ps, ms = [], []
        for hh in range(2):
            sm = s[hh * tq:(hh + 1) * tq] + bias
            m = jnp.max(sm, axis=-1, keepdims=True)
            ps.append(jnp.exp2((sm - m).astype(BF16)))
            ms.append(jnp.broadcast_to(m, (tq, LANES)))
        vw1 = jnp.concatenate([vw, jnp.ones_like(vw)], axis=1)
        acc = _dot(jnp.concatenate(ps, axis=0), vw1)
        rows = pl.ds(out_row, tq, stride=d)
        acc_s[g, rows, :] = jnp.where(low, acc[:tq, :LANES], acc[tq:, :LANES])
        m_s[g, rows, :] = jnp.where(low, ms[0], ms[1])
        l_s[g, rows, :] = jnp.where(low, acc[:tq, LANES:], acc[tq:, LANES:])

    def superblock(sb, carry):
        base = pl.multiple_of(sb * sup, sup)
        for g, (_, d) in enumerate(DILATED_PATTERNS):
            nblk = sup // (d * tq)

            def body(it, c, g=g, d=d, nblk=nblk):
                first = it * DIL_GROUP
                pending = {n: issue(d, base, first + n, nblk) for n in range(DIL_AHEAD)}
                for n in range(DIL_GROUP):
                    if n + DIL_AHEAD < DIL_GROUP:
                        pending[n + DIL_AHEAD] = issue(d, base, first + n + DIL_AHEAD, nblk)
                    consume(g, d, *pending.pop(n))
                return c

            lax.fori_loop(0, d * nblk // DIL_GROUP, body, 0)

        rows = pl.ds(base, sup)
        m1, m2, m3 = m_s[0], m_s[1], m_s[2]
        mx = jnp.maximum(jnp.maximum(m1, m2), m3)
        e1, e2, e3 = jnp.exp2(m1 - mx), jnp.exp2(m2 - mx), jnp.exp2(m3 - mx)
        num = e1 * acc_s[0] + e2 * acc_s[1] + e3 * acc_s[2]
        den = e1 * l_s[0] + e2 * l_s[1] + e3 * l_s[2]
        o_ref[0, rows, :] = (num / den * g_ref[0, rows, :].astype(F32)).astype(BF16)
        return carry

    lax.fori_loop(0, seq // sup, superblock, 0)


def _dilated(qb, kb, vb, gb):
    b, s, _ = qb.shape
    spec = pl.BlockSpec((1, s, LANES), lambda bi, p: (bi, 0, p))
    regrouped = pltpu.VMEM((s, LANES), F32)
    scratch = pltpu.VMEM((len(DILATED_PATTERNS), DIL_SUPER, LANES), F32)
    return pl.pallas_call(
        functools.partial(_dil_kernel, seq=s),
        out_shape=jax.ShapeDtypeStruct((b, s, D_BRANCH), BF16),
        grid=(b, D_BRANCH // LANES),
        in_specs=[spec, spec, spec, spec],
        out_specs=spec,
        scratch_shapes=[regrouped, regrouped, regrouped, scratch, scratch, scratch],
        compiler_params=pltpu.CompilerParams(
            dimension_semantics=("arbitrary",) * 2, vmem_limit_bytes=VMEM_LIMIT_BYTES),
        name="dilated_attn",
    )(qb, kb, vb, gb)


def _post_kernel(h_ref, oat_ref, ob_ref, p_ref, wo_ref, g2_ref, wg_ref, wple_ref,
                 out_ref, wo_s, wg_s, wple_s):
    @pl.when((pl.program_id(0) == 0) & (pl.program_id(1) == 0))
    def _():
        wo_s[...] = wo_ref[0].astype(BF16)
        wg_s[...] = wg_ref[0].astype(BF16)
        wple_s[...] = wple_ref[0].astype(BF16)

    d = D_BRANCH
    h1 = h_ref[...] + _dot_tn(oat_ref[0], wo_s[:d, :]) + _dot(ob_ref[...], wo_s[d:, :])
    u2 = _rms_norm_rows(h1, g2_ref[...]).astype(BF16)
    gate = jax.nn.sigmoid(_dot(u2, wg_s[...]))
    ple = _dot(p_ref[0].astype(BF16), wple_s[...])
    out_ref[...] = h1 + ple * gate


def _post(h2, oat, ob, p3, layer, batch, seq, w_out, g2, w_gate, w_ple):
    nt = seq // POST_TILE
    row = lambda w: pl.BlockSpec((POST_TILE, w), lambda b, t: (b * nt + t, 0))
    full = lambda a: pl.BlockSpec(a.shape, lambda b, t: (0,) * a.ndim)
    once = lambda a: pl.BlockSpec((1,) + a.shape[1:], lambda b, t: (layer, 0, 0),
                                  pipeline_mode=pl.Buffered(1))
    return pl.pallas_call(
        _post_kernel,
        out_shape=jax.ShapeDtypeStruct((batch * seq, D_MODEL), F32),
        grid=(batch, nt),
        in_specs=[row(D_MODEL), pl.BlockSpec((1, D_BRANCH, POST_TILE), lambda b, t: (b, 0, t)),
                  row(D_BRANCH),
                  pl.BlockSpec((1, POST_TILE, PLE_DIM), lambda b, t: (layer, b * nt + t, 0)),
                  once(w_out), full(g2), once(w_gate), once(w_ple)],
        out_specs=row(D_MODEL),
        scratch_shapes=[pltpu.VMEM(w_out.shape[1:], BF16), pltpu.VMEM(w_gate.shape[1:], BF16),
                        pltpu.VMEM(w_ple.shape[1:], BF16)],
        compiler_params=pltpu.CompilerParams(
            dimension_semantics=("arbitrary", "arbitrary"), vmem_limit_bytes=VMEM_LIMIT_BYTES),
        name="out_stage",
    )(h2, oat, ob, p3, w_out, g2, w_gate, w_ple)


def _layer(h2, p3, layer, pos_row, invf, tri, batch, seq, norm_g, w_in, b_f, qk_g, w_out, w_ple,
           ple_norm_g, w_ple_gate):
    d = D_BRANCH
    bf_col = jnp.pad(b_f.astype(F32), (0, F_ROWS - N_HEADS)).reshape(F_ROWS, 1)
    scale = HEAD_DIM ** -0.5
    gcol = jnp.concatenate([jnp.tile(qk_g[0] * (scale * LOG2E), N_HEADS), jnp.tile(qk_g[1], N_HEADS),
                            jnp.tile(qk_g[2] * (scale * LOG2E), N_HEADS), jnp.tile(qk_g[3], N_HEADS)]
                           ).astype(F32).reshape(4 * d, 1)

    qt, k, vt, gt, qb, kb, vb, gb, stats = _proj(
        h2, pos_row, layer, batch, seq, norm_g.reshape(1, D_MODEL), w_in.astype(F32), gcol,
        bf_col, invf, tri)

    oat = _fox(_fox_plan(stats, batch, seq), qt, k, vt, gt).reshape(batch, d, seq)
    as3 = lambda a: a.reshape(batch, seq, d)
    ob = _dilated(as3(qb), as3(kb), as3(vb), as3(gb)).reshape(batch * seq, d)

    return _post(h2, oat, ob, p3, layer, batch, seq, w_out.astype(F32),
                 ple_norm_g[layer].reshape(1, D_MODEL), w_ple_gate.astype(F32),
                 w_ple.astype(F32))


def kernel(x, p, positions, norm_g, w_in, b_f, qk_norm_g, w_out, w_ple, ple_norm_g, w_ple_gate):
    batch, seq, _ = x.shape
    depth = p.shape[0]
    assert x.shape[2] == D_MODEL and p.shape[1:] == (batch, seq, PLE_DIM)
    assert w_in.shape[1:] == (D_MODEL, N_IN)
    assert all(w // dil == N_BACK for w, dil in DILATED_PATTERNS)
    assert all(seq % tile == 0 for tile in (ROW_TILE, POST_TILE, FOX_T, DIL_SUPER,
                                            DIL_SPLIT * DIL_COPY))
    assert FOX_UK == FOX_UQ and FOX_T % FOX_UQ == 0 and ROW_TILE % FOX_UQ == 0
    assert FOX_T % ROW_TILE == 0 and DIL_SUPER % (DIL_GROUP * DIL_TQ) == 0
    pos_row = positions.reshape(1, batch * seq)
    inv_freq = ROPE_THETA ** (-jnp.arange(ROPE_HALF, dtype=F32) / ROPE_HALF)
    invf = inv_freq.reshape(ROPE_HALF, 1)
    idx = np.arange(ROW_TILE)
    tri = jnp.asarray(idx[:, None] <= idx[None, :], BF16)
    h2 = x.reshape(batch * seq, D_MODEL)
    p3 = p.reshape(depth, batch * seq, PLE_DIM)
    for i in range(depth):
        h2 = _layer(h2, p3, i, pos_row, invf, tri, batch, seq,
                    norm_g[i], w_in, b_f[i], qk_norm_g[i], w_out, w_ple,
                    ple_norm_g, w_ple_gate)
    return h2.reshape(batch, seq, D_MODEL)
```

```python
import functools

import numpy as np
import jax
import jax.numpy as jnp
from jax import lax
from jax.experimental import pallas as pl
from jax.experimental.pallas import tpu as pltpu

D_MODEL = 1024
HEAD_DIM = 64
N_HEADS = 8
D_BRANCH = N_HEADS * HEAD_DIM
PLE_DIM = 256
ROPE_THETA = 500000.0
ROPE_DIM = HEAD_DIM // 4
ROPE_HALF = ROPE_DIM // 2
DILATED_PATTERNS = ((128, 1), (512, 4), (2048, 16))
N_BACK = 128
EPS = 1e-6
NEG = -1e30

LANES = 128
SUBLANES = 8
BF16_ROWS = 16
VMEM_LIMIT_BYTES = 56 * 1024 * 1024

ROW_TILE = 512
POST_TILE = 1024
FOX_T = 1024
FOX_UK = 256
FOX_UQ = 256
FOX_AHEAD = 8
FOX_VROWS = HEAD_DIM + BF16_ROWS
FOX_SKIP_BITS = 136.0
FOX_EDGE_OFFSETS = (None, 0, 1, 2)
LOG2E = 1.4426950408889634
DIL_TQ = N_BACK
DIL_TW = DIL_TQ + N_BACK
DIL_SUPER = max(d for _, d in DILATED_PATTERNS) * DIL_TQ
DIL_SPLIT = 4
DIL_COPY = 512
DIL_GROUP = 8
DIL_AHEAD = 4
F_ROWS = BF16_ROWS
W_IN_COLUMNS = dict(zip(
    ("qa", "ka", "va", "ga", "f", "qb", "kb", "vb", "gb"),
    np.cumsum([0] + [D_BRANCH] * 4 + [N_HEADS] + [D_BRANCH] * 3).tolist()))
N_IN = 8 * D_BRANCH + N_HEADS

F32 = jnp.float32
BF16 = jnp.bfloat16


def _dot(a, b):
    return jnp.dot(a, b, preferred_element_type=F32)


def _dot_nt(a, b):
    return lax.dot_general(a, b, (((1,), (1,)), ((), ())), preferred_element_type=F32)


def _dot_tn(a, b):
    return lax.dot_general(a, b, (((0,), (0,)), ((), ())), preferred_element_type=F32)


def _split3(x):
    hi = x.astype(BF16).astype(F32)
    r = x - hi
    mid = r.astype(BF16).astype(F32)
    lo = (r - mid).astype(BF16).astype(F32)
    return hi, mid, lo


def _rms_norm_rows(h, g_row):
    ms = jnp.mean(h * h, axis=-1, keepdims=True)
    return h * lax.rsqrt(ms + EPS) * g_row


def _proj_kernel(h_ref, pos_ref, g1_ref, win_ref, gcol_ref, bf_ref, invf_ref, tri_ref,
                 qt_ref, k_ref, vt_ref, gt_ref, qb_ref, kb_ref, vb_ref, gb_ref, stats_ref,
                 carry_ref, wt_ref, wn_ref):
    t = pl.program_id(1)
    tm = ROW_TILE
    d = D_BRANCH

    @pl.when((pl.program_id(0) == 0) & (t == 0))
    def _():
        def cols(name, width=d):
            lo = W_IN_COLUMNS[name]
            return win_ref[:, lo:lo + width]

        for g, name in enumerate(("qa", "ka", "va", "ga", "qb", "kb")):
            wt_ref[g * d:(g + 1) * d, :] = cols(name).T.astype(BF16)
        f_rows = cols("f", LANES).T[:F_ROWS]
        keep = lax.broadcasted_iota(jnp.int32, f_rows.shape, 0) < N_HEADS
        wt_ref[6 * d:6 * d + F_ROWS, :] = jnp.where(keep, f_rows, 0.0).astype(BF16)
        wn_ref[:, :d] = cols("vb").astype(BF16)
        wn_ref[:, d:] = cols("gb").astype(BF16)

    u = _rms_norm_rows(h_ref[...], g1_ref[...]).astype(BF16)

    def feat_major(group):
        return _dot_nt(wt_ref[group * d:(group + 1) * d, :], u)

    zn = _dot(u, wn_ref[...])
    f = _dot_nt(wt_ref[6 * d:6 * d + F_ROWS, :], u) + bf_ref[...]
    z_qa = feat_major(0)

    vb_ref[...] = zn[:, :d]
    gate_b = zn[:, d:]
    gb_ref[...] = (gate_b * jax.nn.sigmoid(gate_b)).astype(BF16)
    z_ka = feat_major(1)

    def head_norm(z, group):
        z3 = z.reshape(N_HEADS, HEAD_DIM, tm)
        ss = jnp.sum(z3 * z3, axis=1, keepdims=True)
        y3 = z3 * lax.rsqrt(ss * (1.0 / HEAD_DIM) + EPS)
        return y3.reshape(d, tm) * gcol_ref[group * d:(group + 1) * d, :]

    ang = invf_ref[...] * pos_ref[...].astype(F32)
    cos = jnp.cos(ang)
    sin = jnp.sin(ang)

    def rope(y):
        y3 = y.reshape(N_HEADS, HEAD_DIM, tm)
        x1 = y3[:, :ROPE_HALF, :]
        x2 = y3[:, ROPE_HALF:ROPE_DIM, :]
        rot = [x1 * cos - x2 * sin, x2 * cos + x1 * sin, y3[:, ROPE_DIM:, :]]
        return jnp.concatenate(rot, axis=1).reshape(d, tm)

    lf = jnp.minimum(f, 0.0) - jnp.log1p(jnp.exp(-jnp.abs(f)))
    stacked = jnp.concatenate([x.astype(BF16) for x in _split3(lf)], axis=0)
    cs = _dot(stacked, tri_ref[...])
    carry = jnp.where(t == 0, 0.0, carry_ref[:, 0:1])
    c = cs[:F_ROWS] + cs[F_ROWS:2 * F_ROWS] + cs[2 * F_ROWS:] + carry
    carry_ref[...] = jnp.broadcast_to(c[:, tm - 1:tm], (F_ROWS, LANES))
    c2 = c * LOG2E

    qa = head_norm(z_qa, 0)
    z_va = feat_major(2)
    ka = head_norm(z_ka, 1)
    z_ga = feat_major(3)
    row = lax.broadcasted_iota(jnp.int32, (SUBLANES, tm), 0)
    pad = jnp.zeros((HEAD_DIM - SUBLANES, tm), F32)
    for hd in range(N_HEADS):
        c_hi, c_mid, c_lo = _split3(c2[hd:hd + 1, :])
        q_aug = jnp.where(row == 0, c_hi, jnp.where(row == 1, c_mid, jnp.where(
            row == 2, c_lo, jnp.where(row < 6, 1.0, 0.0))))
        k_aug = jnp.where(row < 3, 1.0, jnp.where(row == 3, -c_hi, jnp.where(
            row == 4, -c_mid, jnp.where(row == 5, -c_lo, 0.0))))
        rows = slice(hd * HEAD_DIM, (hd + 1) * HEAD_DIM)
        qt_ref[0, hd] = jnp.concatenate([qa[rows], q_aug, pad], axis=0).astype(BF16)
        k_ref[0, hd] = jnp.concatenate([ka[rows], k_aug, pad], axis=0).T.astype(BF16)

    ones_row = jnp.where(lax.broadcasted_iota(jnp.int32, (N_HEADS, BF16_ROWS, tm), 1) == 0,
                         1.0, 0.0)
    z_qb = feat_major(4)
    va = z_va.reshape(N_HEADS, HEAD_DIM, tm)
    vt_ref[0, :, 0] = jnp.concatenate([va, ones_row], axis=1).astype(BF16)
    z_kb = feat_major(5)
    gt_ref[0] = (z_ga * jax.nn.sigmoid(z_ga)).reshape(N_HEADS, HEAD_DIM, tm).astype(BF16)

    qb_ref[...] = rope(head_norm(z_qb, 2)).T
    kb_ref[...] = rope(head_norm(z_kb, 3)).T

    def max_sq_norm(y):
        y3 = y.reshape(N_HEADS, HEAD_DIM, tm)
        return jnp.max(jnp.sum(y3 * y3, axis=1), axis=-1, keepdims=True)

    lane = lax.broadcasted_iota(jnp.int32, (N_HEADS, LANES), 1)
    stats = jnp.where(lane == 0, max_sq_norm(qa), jnp.where(lane == 1, max_sq_norm(ka), 0.0))
    for sub in range(tm // FOX_UQ):
        lo = sub * FOX_UQ
        hi = lo + FOX_UQ - 1
        stats = jnp.where(lane == 2 + 2 * sub, c2[:N_HEADS, lo:lo + 1],
                          jnp.where(lane == 3 + 2 * sub, c2[:N_HEADS, hi:hi + 1], stats))
    stats_ref[0] = stats


def _proj(h2, pos_row, layer, batch, seq, g1, w_in, gcol, bf_col, invf, tri):
    nt = seq // ROW_TILE
    rows = batch * seq
    d = D_BRANCH
    row = lambda w: pl.BlockSpec((ROW_TILE, w), lambda b, t: (b * nt + t, 0))
    full = lambda a: pl.BlockSpec(a.shape, lambda b, t: (0,) * a.ndim)
    w_spec = pl.BlockSpec((D_MODEL, N_IN), lambda b, t: (layer, 0),
                          pipeline_mode=pl.Buffered(1))
    out_shape = (
        jax.ShapeDtypeStruct((batch, N_HEADS, 2 * HEAD_DIM, seq), BF16),
        jax.ShapeDtypeStruct((batch, N_HEADS, seq, 2 * HEAD_DIM), BF16),
        jax.ShapeDtypeStruct((batch, N_HEADS, nt, FOX_VROWS, ROW_TILE), BF16),
        jax.ShapeDtypeStruct((batch, N_HEADS, HEAD_DIM, seq), BF16),
        jax.ShapeDtypeStruct((rows, d), F32),
        jax.ShapeDtypeStruct((rows, d), F32),
        jax.ShapeDtypeStruct((rows, d), F32),
        jax.ShapeDtypeStruct((rows, d), BF16),
        jax.ShapeDtypeStruct((batch * nt, N_HEADS, LANES), F32),
    )
    out_specs = (
        pl.BlockSpec((1, N_HEADS, 2 * HEAD_DIM, ROW_TILE), lambda b, t: (b, 0, 0, t)),
        pl.BlockSpec((1, N_HEADS, ROW_TILE, 2 * HEAD_DIM), lambda b, t: (b, 0, t, 0)),
        pl.BlockSpec((1, N_HEADS, 1, FOX_VROWS, ROW_TILE), lambda b, t: (b, 0, t, 0, 0)),
        pl.BlockSpec((1, N_HEADS, HEAD_DIM, ROW_TILE), lambda b, t: (b, 0, 0, t)),
        row(d), row(d), row(d), row(d),
        pl.BlockSpec((1, N_HEADS, LANES), lambda b, t: (b * nt + t, 0, 0)),
    )
    return pl.pallas_call(
        _proj_kernel,
        out_shape=out_shape,
        grid=(batch, nt),
        in_specs=[row(D_MODEL), pl.BlockSpec((1, ROW_TILE), lambda b, t: (0, b * nt + t)),
                  full(g1), w_spec, full(gcol), full(bf_col), full(invf), full(tri)],
        out_specs=out_specs,
        scratch_shapes=[pltpu.VMEM((F_ROWS, LANES), F32),
                        pltpu.VMEM((6 * d + F_ROWS, D_MODEL), BF16),
                        pltpu.VMEM((D_MODEL, 2 * d), BF16)],
        compiler_params=pltpu.CompilerParams(
            dimension_semantics=("arbitrary", "arbitrary"), vmem_limit_bytes=VMEM_LIMIT_BYTES),
        name="in_proj",
    )(h2, pos_row, g1, w_in, gcol, bf_col, invf, tri)


def _fox_kernel(plan_ref, q_ref, k_ref, v_ref, g_ref, o_ref):
    i = pl.program_id(2)
    plan = plan_ref[(pl.program_id(0) * N_HEADS + pl.program_id(1)) * pl.num_programs(2) + i]
    first = plan // len(FOX_EDGE_OFFSETS)
    edge = plan % len(FOX_EDGE_OFFSETS)
    uk, uq = FOX_UK, FOX_UQ
    n_q = FOX_T // uq
    q = [q_ref[0, 0, :, a * uq:(a + 1) * uq] for a in range(n_q)]
    rel = (lax.broadcasted_iota(jnp.int32, (uk, uq), 0)
           - lax.broadcasted_iota(jnp.int32, (uk, uq), 1))

    def scores(j, ks, a):
        start = pl.multiple_of(j * FOX_T, FOX_T)
        return _dot(k_ref[0, 0, pl.ds(start + ks * uk, uk), :], q[a])

    def consume(s, v_sub, state, max_rel):
        m, acc = state
        if max_rel is not None:
            s = jnp.where(rel <= max_rel, s, NEG)
        m_new = jnp.maximum(m, jnp.max(s, axis=0, keepdims=True))
        alpha = jnp.exp2(m - m_new)
        p = jnp.exp2((s - m_new).astype(BF16))
        acc = alpha * acc + _dot(v_sub, p)
        return m_new, acc

    def step(j, carry, diagonal=False, min_offset=None):
        s_first, states = carry
        states = list(states)
        units = []
        for ks in range(FOX_T // uk):
            for a in range(n_q):
                max_rel = None
                if min_offset is not None and ks - a < min_offset:
                    continue
                if diagonal:
                    if ks * uk >= (a + 1) * uq:
                        continue
                    if (ks + 1) * uk - 1 > a * uq:
                        max_rel = a * uq - ks * uk
                units.append((ks, a, max_rel))
        pending = {0: scores(j, units[0][0], units[0][1]) if s_first is None else s_first}
        queue = list(range(1, len(units))) + ([] if diagonal else ["next"])
        result = {}

        def issue(item):
            if item == "next":
                result["next"] = scores(j + 1, 0, 0)
            else:
                pending[item] = scores(j, units[item][0], units[item][1])

        for item in queue[:FOX_AHEAD - 1]:
            issue(item)
        for n, (ks, a, max_rel) in enumerate(units):
            if n + FOX_AHEAD - 1 < len(queue):
                issue(queue[n + FOX_AHEAD - 1])
            chunk, off = divmod(ks * uk, ROW_TILE)
            v_sub = v_ref[0, 0, j * (FOX_T // ROW_TILE) + chunk, :, off:off + uk]
            states[a] = consume(pending.pop(n), v_sub, states[a], max_rel)
        return result.get("next"), tuple(states)

    init = tuple((jnp.full((1, uq), NEG, F32), jnp.zeros((FOX_VROWS, uq), F32))
                 for _ in range(n_q))
    branches = [lambda: (scores(i, 0, 0), init)]
    for offset in FOX_EDGE_OFFSETS:
        branches.append(functools.partial(step, first, (None, init), min_offset=offset))
    carry = lax.switch(jnp.where(first >= i, 0, edge + 1), branches)
    carry = lax.fori_loop(first + 1, i, step, carry)
    _, states = step(i, carry, diagonal=True)
    out = jnp.concatenate([acc[:HEAD_DIM] / acc[HEAD_DIM:HEAD_DIM + 1] for _, acc in states],
                          axis=1)
    o_ref[0, 0] = (out * g_ref[0, 0].astype(F32)).astype(BF16)


def _fox_plan(stats, batch, seq):
    nt, nq = seq // ROW_TILE, seq // FOX_T
    sub_t, sub_q = ROW_TILE // FOX_UQ, FOX_T // FOX_UQ
    st = stats.reshape(batch, nt, N_HEADS, LANES)
    q_sq = st[..., 0].reshape(batch, nq, nt // nq, N_HEADS).max(axis=2)
    k_sq = st[..., 1].max(axis=1, keepdims=True)
    bound = jnp.sqrt(q_sq * k_sq) * 1.01 + 1.0
    edges = st[..., 2:2 + 2 * sub_t].reshape(batch, nt, N_HEADS, sub_t, 2)
    edges = edges.transpose(0, 1, 3, 2, 4).reshape(batch, nq, sub_q, N_HEADS, 2)
    c_first, c_last = edges[..., 0], edges[..., 1]
    gap = c_first[:, :, :, None, None, :] - c_last[:, None, None, :, :, :]
    skip = (2.0 * bound[:, :, None, None, None, :] + gap) < -FOX_SKIP_BITS
    steps = jnp.arange(nq, dtype=jnp.int32)
    step_skip = jnp.all(skip, axis=(2, 4))
    first = jnp.min(jnp.where(step_skip, nq, steps[None, None, :, None]), axis=2)
    first = jnp.minimum(first, steps[None, :, None])
    other_step = steps[None, None, None, :, None, None] != first[:, :, None, None, None, :]
    at_first = jnp.all(skip | other_step, axis=3)
    offset = (jnp.arange(sub_q)[None, :] - jnp.arange(sub_q)[:, None])
    variant = jnp.zeros_like(first)
    for v, min_offset in enumerate(FOX_EDGE_OFFSETS):
        if min_offset is not None:
            dropped = (offset < min_offset)[None, None, :, :, None]
            ok = jnp.all(at_first | ~dropped, axis=(2, 3))
            variant = jnp.where(ok, v, variant)
    plan = first * len(FOX_EDGE_OFFSETS) + variant
    return plan.transpose(0, 2, 1).reshape(-1).astype(jnp.int32)


def _fox(first, qt, k, vt, gt):
    b, _, _, s = qt.shape
    q_spec = pl.BlockSpec((1, 1, 2 * HEAD_DIM, FOX_T), lambda bi, h, i, f: (bi, h, 0, i))
    k_spec = pl.BlockSpec((1, 1, s, 2 * HEAD_DIM), lambda bi, h, i, f: (bi, h, 0, 0))
    v_spec = pl.BlockSpec((1, 1, s // ROW_TILE, FOX_VROWS, ROW_TILE),
                          lambda bi, h, i, f: (bi, h, 0, 0, 0))
    o_spec = pl.BlockSpec((1, 1, HEAD_DIM, FOX_T), lambda bi, h, i, f: (bi, h, 0, i))
    return pl.pallas_call(
        _fox_kernel,
        out_shape=jax.ShapeDtypeStruct((b, N_HEADS, HEAD_DIM, s), BF16),
        grid_spec=pltpu.PrefetchScalarGridSpec(
            num_scalar_prefetch=1,
            grid=(b, N_HEADS, s // FOX_T),
            in_specs=[q_spec, k_spec, v_spec, o_spec],
            out_specs=o_spec),
        compiler_params=pltpu.CompilerParams(
            dimension_semantics=("arbitrary",) * 3, vmem_limit_bytes=VMEM_LIMIT_BYTES),
        name="fox_attention",
    )(first, qt, k, vt, gt)


def _dil_kernel(q_ref, k_ref, v_ref, g_ref, o_ref, q4_s, k4_s, v4_s, acc_s, m_s, l_s, *, seq):
    tq, tw, sup = DIL_TQ, DIL_TW, DIL_SUPER
    part = seq // DIL_SPLIT

    def regroup(c, carry):
        for src, dst in ((q_ref, q4_s), (k_ref, k4_s), (v_ref, v4_s)):
            for r in range(DIL_SPLIT):
                rows = src[0, pl.ds(c * DIL_SPLIT * DIL_COPY + r, DIL_COPY, stride=DIL_SPLIT), :]
                dst[pl.ds(r * part + c * DIL_COPY, DIL_COPY), :] = rows
        return carry

    lax.fori_loop(0, part // DIL_COPY, regroup, 0)

    def stream_rows(nat_ref, split_ref, d, r, pos, n):
        if d % DIL_SPLIT:
            x = nat_ref[0, pl.ds(d * pos + r, n, stride=d), :]
        else:
            sd = d // DIL_SPLIT
            start = (r % DIL_SPLIT) * part + sd * pos + r // DIL_SPLIT
            x = split_ref[pl.ds(start, n, stride=sd), :]
        return x.astype(BF16)

    lane = lax.broadcasted_iota(jnp.int32, (1, LANES), 1)
    low = lane < HEAD_DIM
    rel = (lax.broadcasted_iota(jnp.int32, (tq, tw), 0)
           - lax.broadcasted_iota(jnp.int32, (tq, tw), 1))

    def band_bias(q_minus_w):
        dist = rel + q_minus_w
        return jnp.where(dist >= 0, jnp.where(dist <= N_BACK, 0.0, NEG), NEG)

    bias_inner = band_bias(N_BACK)
    bias_start = band_bias(0)

    def issue(d, base, idx, nblk):
        r = idx // nblk
        jb = idx % nblk
        qs = base // d + jb * tq
        ws = jnp.maximum(qs - N_BACK, 0)
        q = stream_rows(q_ref, q4_s, d, r, qs, tq)
        kw = stream_rows(k_ref, k4_s, d, r, ws, tw)
        zero = jnp.zeros_like(q)
        q2 = jnp.concatenate([jnp.where(low, q, zero), jnp.where(low, zero, q)], axis=0)
        return _dot_nt(q2, kw), r, qs, ws, d * (qs - base // d) + r

    def consume(g, d, s, r, qs, ws, out_row):
        vw = stream_rows(v_ref, v4_s, d, r, ws, tw)
        bias = jnp.where(qs == ws, bias_start, bias_inner)
        ps, ms = [], []
        for hh in range(2):
            sm = s[hh * tq:(hh + 1) * tq] + bias
            m = jnp.max(sm, axis=-1, keepdims=True)
            ps.append(jnp.exp2((sm - m).astype(BF16)))
            ms.append(jnp.broadcast_to(m, (tq, LANES)))
        vw1 = jnp.concatenate([vw, jnp.ones_like(vw)], axis=1)
        acc = _dot(jnp.concatenate(ps, axis=0), vw1)
        rows = pl.ds(out_row, tq, stride=d)
        acc_s[g, rows, :] = jnp.where(low, acc[:tq, :LANES], acc[tq:, :LANES])
        m_s[g, rows, :] = jnp.where(low, ms[0], ms[1])
        l_s[g, rows, :] = jnp.where(low, acc[:tq, LANES:], acc[tq:, LANES:])

    def superblock(sb, carry):
        base = pl.multiple_of(sb * sup, sup)
        for g, (_, d) in enumerate(DILATED_PATTERNS):
            nblk = sup // (d * tq)

            def body(it, c, g=g, d=d, nblk=nblk):
                first = it * DIL_GROUP
                pending = {n: issue(d, base, first + n, nblk) for n in range(DIL_AHEAD)}
                for n in range(DIL_GROUP):
                    if n + DIL_AHEAD < DIL_GROUP:
                        pending[n + DIL_AHEAD] = issue(d, base, first + n + DIL_AHEAD, nblk)
                    consume(g, d, *pending.pop(n))
                return c

            lax.fori_loop(0, d * nblk // DIL_GROUP, body, 0)

        rows = pl.ds(base, sup)
        m1, m2, m3 = m_s[0], m_s[1], m_s[2]
        mx = jnp.maximum(jnp.maximum(m1, m2), m3)
        e1, e2, e3 = jnp.exp2(m1 - mx), jnp.exp2(m2 - mx), jnp.exp2(m3 - mx)
        num = e1 * acc_s[0] + e2 * acc_s[1] + e3 * acc_s[2]
        den = e1 * l_s[0] + e2 * l_s[1] + e3 * l_s[2]
        o_ref[0, rows, :] = (num / den * g_ref[0, rows, :].astype(F32)).astype(BF16)
        return carry

    lax.fori_loop(0, seq // sup, superblock, 0)


def _dilated(qb, kb, vb, gb):
    b, s, _ = qb.shape
    spec = pl.BlockSpec((1, s, LANES), lambda bi, p: (bi, 0, p))
    regrouped = pltpu.VMEM((s, LANES), F32)
    scratch = pltpu.VMEM((len(DILATED_PATTERNS), DIL_SUPER, LANES), F32)
    return pl.pallas_call(
        functools.partial(_dil_kernel, seq=s),
        out_shape=jax.ShapeDtypeStruct((b, s, D_BRANCH), BF16),
        grid=(b, D_BRANCH // LANES),
        in_specs=[spec, spec, spec, spec],
        out_specs=spec,
        scratch_shapes=[regrouped, regrouped, regrouped, scratch, scratch, scratch],
        compiler_params=pltpu.CompilerParams(
            dimension_semantics=("arbitrary",) * 2, vmem_limit_bytes=VMEM_LIMIT_BYTES),
        name="dilated_attn",
    )(qb, kb, vb, gb)


def _post_kernel(h_ref, oat_ref, ob_ref, p_ref, wo_ref, g2_ref, wg_ref, wple_ref,
                 out_ref, wo_s, wg_s, wple_s):
    @pl.when((pl.program_id(0) == 0) & (pl.program_id(1) == 0))
    def _():
        wo_s[...] = wo_ref[0].astype(BF16)
        wg_s[...] = wg_ref[0].astype(BF16)
        wple_s[...] = wple_ref[0].astype(BF16)

    d = D_BRANCH
    h1 = h_ref[...] + _dot_tn(oat_ref[0], wo_s[:d, :]) + _dot(ob_ref[...], wo_s[d:, :])
    u2 = _rms_norm_rows(h1, g2_ref[...]).astype(BF16)
    gate = jax.nn.sigmoid(_dot(u2, wg_s[...]))
    ple = _dot(p_ref[0].astype(BF16), wple_s[...])
    out_ref[...] = h1 + ple * gate


def _post(h2, oat, ob, p3, layer, batch, seq, w_out, g2, w_gate, w_ple):
    nt = seq // POST_TILE
    row = lambda w: pl.BlockSpec((POST_TILE, w), lambda b, t: (b * nt + t, 0))
    full = lambda a: pl.BlockSpec(a.shape, lambda b, t: (0,) * a.ndim)
    once = lambda a: pl.BlockSpec((1,) + a.shape[1:], lambda b, t: (layer, 0, 0),
                                  pipeline_mode=pl.Buffered(1))
    return pl.pallas_call(
        _post_kernel,
        out_shape=jax.ShapeDtypeStruct((batch * seq, D_MODEL), F32),
        grid=(batch, nt),
        in_specs=[row(D_MODEL), pl.BlockSpec((1, D_BRANCH, POST_TILE), lambda b, t: (b, 0, t)),
                  row(D_BRANCH),
                  pl.BlockSpec((1, POST_TILE, PLE_DIM), lambda b, t: (layer, b * nt + t, 0)),
                  once(w_out), full(g2), once(w_gate), once(w_ple)],
        out_specs=row(D_MODEL),
        scratch_shapes=[pltpu.VMEM(w_out.shape[1:], BF16), pltpu.VMEM(w_gate.shape[1:], BF16),
                        pltpu.VMEM(w_ple.shape[1:], BF16)],
        compiler_params=pltpu.CompilerParams(
            dimension_semantics=("arbitrary", "arbitrary"), vmem_limit_bytes=VMEM_LIMIT_BYTES),
        name="out_stage",
    )(h2, oat, ob, p3, w_out, g2, w_gate, w_ple)


def _layer(h2, p3, layer, pos_row, invf, tri, batch, seq, norm_g, w_in, b_f, qk_g, w_out, w_ple,
           ple_norm_g, w_ple_gate):
    d = D_BRANCH
    bf_col = jnp.pad(b_f.astype(F32), (0, F_ROWS - N_HEADS)).reshape(F_ROWS, 1)
    scale = HEAD_DIM ** -0.5
    gcol = jnp.concatenate([jnp.tile(qk_g[0] * (scale * LOG2E), N_HEADS), jnp.tile(qk_g[1], N_HEADS),
                            jnp.tile(qk_g[2] * (scale * LOG2E), N_HEADS), jnp.tile(qk_g[3], N_HEADS)]
                           ).astype(F32).reshape(4 * d, 1)

    qt, k, vt, gt, qb, kb, vb, gb, stats = _proj(
        h2, pos_row, layer, batch, seq, norm_g.reshape(1, D_MODEL),
        w_in.astype(F32).reshape(-1, N_IN), gcol,
        bf_col, invf, tri)

    oat = _fox(_fox_plan(stats, batch, seq), qt, k, vt, gt).reshape(batch, d, seq)
    as3 = lambda a: a.reshape(batch, seq, d)
    ob = _dilated(as3(qb), as3(kb), as3(vb), as3(gb)).reshape(batch * seq, d)

    return _post(h2, oat, ob, p3, layer, batch, seq, w_out.astype(F32),
                 ple_norm_g[layer].reshape(1, D_MODEL), w_ple_gate.astype(F32),
                 w_ple.astype(F32))


def kernel(x, p, positions, norm_g, w_in, b_f, qk_norm_g, w_out, w_ple, ple_norm_g, w_ple_gate):
    batch, seq, _ = x.shape
    depth = p.shape[0]
    assert x.shape[2] == D_MODEL and p.shape[1:] == (batch, seq, PLE_DIM)
    assert w_in.shape[1:] == (D_MODEL, N_IN)
    assert all(w // dil == N_BACK for w, dil in DILATED_PATTERNS)
    assert all(seq % tile == 0 for tile in (ROW_TILE, POST_TILE, FOX_T, DIL_SUPER,
                                            DIL_SPLIT * DIL_COPY))
    assert FOX_UK == FOX_UQ and FOX_T % FOX_UQ == 0 and ROW_TILE % FOX_UQ == 0
    assert FOX_T % ROW_TILE == 0 and DIL_SUPER % (DIL_GROUP * DIL_TQ) == 0
    pos_row = positions.reshape(1, batch * seq)
    inv_freq = ROPE_THETA ** (-jnp.arange(ROPE_HALF, dtype=F32) / ROPE_HALF)
    invf = inv_freq.reshape(ROPE_HALF, 1)
    idx = np.arange(ROW_TILE)
    tri = jnp.asarray(idx[:, None] <= idx[None, :], BF16)
    h2 = x.reshape(batch * seq, D_MODEL)
    p3 = p.reshape(depth, batch * seq, PLE_DIM)
    for i in range(depth):
        h2 = _layer(h2, p3, i, pos_row, invf, tri, batch, seq,
                    norm_g[i], w_in, b_f[i], qk_norm_g[i], w_out, w_ple,
                    ple_norm_g, w_ple_gate)
    return h2.reshape(batch, seq, D_MODEL)
```

```python
import functools

import numpy as np
import jax
import jax.numpy as jnp
from jax import lax
from jax.experimental import pallas as pl
from jax.experimental.pallas import tpu as pltpu

D_MODEL = 1024
HEAD_DIM = 64
N_HEADS = 8
D_BRANCH = N_HEADS * HEAD_DIM
PLE_DIM = 256
ROPE_THETA = 500000.0
ROPE_DIM = HEAD_DIM // 4
ROPE_HALF = ROPE_DIM // 2
DILATED_PATTERNS = ((128, 1), (512, 4), (2048, 16))
N_BACK = 128
EPS = 1e-6
NEG = -1e30

LANES = 128
SUBLANES = 8
BF16_ROWS = 16
VMEM_LIMIT_BYTES = 56 * 1024 * 1024

ROW_TILE = 512
POST_TILE = 1024
FOX_T = 1024
FOX_UK = 256
FOX_UQ = 256
FOX_AHEAD = 8
FOX_VROWS = HEAD_DIM + BF16_ROWS
FOX_SKIP_BITS = 136.0
FOX_EDGE_OFFSETS = (None, 0, 1, 2)
LOG2E = 1.4426950408889634
DIL_TQ = N_BACK
DIL_TW = DIL_TQ + N_BACK
DIL_SUPER = max(d for _, d in DILATED_PATTERNS) * DIL_TQ
DIL_SPLIT = 4
DIL_COPY = 512
DIL_GROUP = 8
DIL_AHEAD = 4
F_ROWS = BF16_ROWS
W_IN_COLUMNS = dict(zip(
    ("qa", "ka", "va", "ga", "f", "qb", "kb", "vb", "gb"),
    np.cumsum([0] + [D_BRANCH] * 4 + [N_HEADS] + [D_BRANCH] * 3).tolist()))
N_IN = 8 * D_BRANCH + N_HEADS

F32 = jnp.float32
BF16 = jnp.bfloat16


def _dot(a, b):
    return jnp.dot(a, b, preferred_element_type=F32)


def _dot_nt(a, b):
    return lax.dot_general(a, b, (((1,), (1,)), ((), ())), preferred_element_type=F32)


def _dot_tn(a, b):
    return lax.dot_general(a, b, (((0,), (0,)), ((), ())), preferred_element_type=F32)


def _split3(x):
    hi = x.astype(BF16).astype(F32)
    r = x - hi
    mid = r.astype(BF16).astype(F32)
    lo = (r - mid).astype(BF16).astype(F32)
    return hi, mid, lo


def _rms_norm_rows(h, g_row):
    ms = jnp.mean(h * h, axis=-1, keepdims=True)
    return h * lax.rsqrt(ms + EPS) * g_row


def _proj_kernel(h_ref, pos_ref, g1_ref, win_ref, gcol_ref, bf_ref, invf_ref, tri_ref,
                 qt_ref, k_ref, vt_ref, gt_ref, qb_ref, kb_ref, vb_ref, gb_ref, stats_ref,
                 carry_ref, wt_ref, wn_ref):
    t = pl.program_id(1)
    tm = ROW_TILE
    d = D_BRANCH

    @pl.when((pl.program_id(0) == 0) & (t == 0))
    def _():
        def cols(name, width=d):
            lo = W_IN_COLUMNS[name]
            return win_ref[0, :, lo:lo + width]

        for g, name in enumerate(("qa", "ka", "va", "ga", "qb", "kb")):
            wt_ref[g * d:(g + 1) * d, :] = cols(name).T.astype(BF16)
        f_rows = cols("f", LANES).T[:F_ROWS]
        keep = lax.broadcasted_iota(jnp.int32, f_rows.shape, 0) < N_HEADS
        wt_ref[6 * d:6 * d + F_ROWS, :] = jnp.where(keep, f_rows, 0.0).astype(BF16)
        wn_ref[:, :d] = cols("vb").astype(BF16)
        wn_ref[:, d:] = cols("gb").astype(BF16)

    u = _rms_norm_rows(h_ref[...], g1_ref[...]).astype(BF16)

    def feat_major(group):
        return _dot_nt(wt_ref[group * d:(group + 1) * d, :], u)

    zn = _dot(u, wn_ref[...])
    f = _dot_nt(wt_ref[6 * d:6 * d + F_ROWS, :], u) + bf_ref[...]
    z_qa = feat_major(0)

    vb_ref[...] = zn[:, :d]
    gate_b = zn[:, d:]
    gb_ref[...] = (gate_b * jax.nn.sigmoid(gate_b)).astype(BF16)
    z_ka = feat_major(1)

    def head_norm(z, group):
        z3 = z.reshape(N_HEADS, HEAD_DIM, tm)
        ss = jnp.sum(z3 * z3, axis=1, keepdims=True)
        y3 = z3 * lax.rsqrt(ss * (1.0 / HEAD_DIM) + EPS)
        return y3.reshape(d, tm) * gcol_ref[group * d:(group + 1) * d, :]

    ang = invf_ref[...] * pos_ref[...].astype(F32)
    cos = jnp.cos(ang)
    sin = jnp.sin(ang)

    def rope(y):
        y3 = y.reshape(N_HEADS, HEAD_DIM, tm)
        x1 = y3[:, :ROPE_HALF, :]
        x2 = y3[:, ROPE_HALF:ROPE_DIM, :]
        rot = [x1 * cos - x2 * sin, x2 * cos + x1 * sin, y3[:, ROPE_DIM:, :]]
        return jnp.concatenate(rot, axis=1).reshape(d, tm)

    lf = jnp.minimum(f, 0.0) - jnp.log1p(jnp.exp(-jnp.abs(f)))
    stacked = jnp.concatenate([x.astype(BF16) for x in _split3(lf)], axis=0)
    cs = _dot(stacked, tri_ref[...])
    carry = jnp.where(t == 0, 0.0, carry_ref[:, 0:1])
    c = cs[:F_ROWS] + cs[F_ROWS:2 * F_ROWS] + cs[2 * F_ROWS:] + carry
    carry_ref[...] = jnp.broadcast_to(c[:, tm - 1:tm], (F_ROWS, LANES))
    c2 = c * LOG2E

    qa = head_norm(z_qa, 0)
    z_va = feat_major(2)
    ka = head_norm(z_ka, 1)
    z_ga = feat_major(3)
    row = lax.broadcasted_iota(jnp.int32, (SUBLANES, tm), 0)
    pad = jnp.zeros((HEAD_DIM - SUBLANES, tm), F32)
    for hd in range(N_HEADS):
        c_hi, c_mid, c_lo = _split3(c2[hd:hd + 1, :])
        q_aug = jnp.where(row == 0, c_hi, jnp.where(row == 1, c_mid, jnp.where(
            row == 2, c_lo, jnp.where(row < 6, 1.0, 0.0))))
        k_aug = jnp.where(row < 3, 1.0, jnp.where(row == 3, -c_hi, jnp.where(
            row == 4, -c_mid, jnp.where(row == 5, -c_lo, 0.0))))
        rows = slice(hd * HEAD_DIM, (hd + 1) * HEAD_DIM)
        qt_ref[0, hd] = jnp.concatenate([qa[rows], q_aug, pad], axis=0).astype(BF16)
        k_ref[0, hd] = jnp.concatenate([ka[rows], k_aug, pad], axis=0).T.astype(BF16)

    ones_row = jnp.where(lax.broadcasted_iota(jnp.int32, (N_HEADS, BF16_ROWS, tm), 1) == 0,
                         1.0, 0.0)
    z_qb = feat_major(4)
    va = z_va.reshape(N_HEADS, HEAD_DIM, tm)
    vt_ref[0, :, 0] = jnp.concatenate([va, ones_row], axis=1).astype(BF16)
    z_kb = feat_major(5)
    gt_ref[0] = (z_ga * jax.nn.sigmoid(z_ga)).reshape(N_HEADS, HEAD_DIM, tm).astype(BF16)

    qb_ref[...] = rope(head_norm(z_qb, 2)).T
    kb_ref[...] = rope(head_norm(z_kb, 3)).T

    def max_sq_norm(y):
        y3 = y.reshape(N_HEADS, HEAD_DIM, tm)
        return jnp.max(jnp.sum(y3 * y3, axis=1), axis=-1, keepdims=True)

    lane = lax.broadcasted_iota(jnp.int32, (N_HEADS, LANES), 1)
    stats = jnp.where(lane == 0, max_sq_norm(qa), jnp.where(lane == 1, max_sq_norm(ka), 0.0))
    for sub in range(tm // FOX_UQ):
        lo = sub * FOX_UQ
        hi = lo + FOX_UQ - 1
        stats = jnp.where(lane == 2 + 2 * sub, c2[:N_HEADS, lo:lo + 1],
                          jnp.where(lane == 3 + 2 * sub, c2[:N_HEADS, hi:hi + 1], stats))
    stats_ref[0] = stats


def _proj(h2, pos_row, layer, batch, seq, g1, w_in, gcol, bf_col, invf, tri):
    nt = seq // ROW_TILE
    rows = batch * seq
    d = D_BRANCH
    row = lambda w: pl.BlockSpec((ROW_TILE, w), lambda b, t: (b * nt + t, 0))
    full = lambda a: pl.BlockSpec(a.shape, lambda b, t: (0,) * a.ndim)
    w_spec = pl.BlockSpec((1,) + w_in.shape[1:], lambda b, t: (layer, 0, 0),
                          pipeline_mode=pl.Buffered(1))
    out_shape = (
        jax.ShapeDtypeStruct((batch, N_HEADS, 2 * HEAD_DIM, seq), BF16),
        jax.ShapeDtypeStruct((batch, N_HEADS, seq, 2 * HEAD_DIM), BF16),
        jax.ShapeDtypeStruct((batch, N_HEADS, nt, FOX_VROWS, ROW_TILE), BF16),
        jax.ShapeDtypeStruct((batch, N_HEADS, HEAD_DIM, seq), BF16),
        jax.ShapeDtypeStruct((rows, d), F32),
        jax.ShapeDtypeStruct((rows, d), F32),
        jax.ShapeDtypeStruct((rows, d), F32),
        jax.ShapeDtypeStruct((rows, d), BF16),
        jax.ShapeDtypeStruct((batch * nt, N_HEADS, LANES), F32),
    )
    out_specs = (
        pl.BlockSpec((1, N_HEADS, 2 * HEAD_DIM, ROW_TILE), lambda b, t: (b, 0, 0, t)),
        pl.BlockSpec((1, N_HEADS, ROW_TILE, 2 * HEAD_DIM), lambda b, t: (b, 0, t, 0)),
        pl.BlockSpec((1, N_HEADS, 1, FOX_VROWS, ROW_TILE), lambda b, t: (b, 0, t, 0, 0)),
        pl.BlockSpec((1, N_HEADS, HEAD_DIM, ROW_TILE), lambda b, t: (b, 0, 0, t)),
        row(d), row(d), row(d), row(d),
        pl.BlockSpec((1, N_HEADS, LANES), lambda b, t: (b * nt + t, 0, 0)),
    )
    return pl.pallas_call(
        _proj_kernel,
        out_shape=out_shape,
        grid=(batch, nt),
        in_specs=[row(D_MODEL), pl.BlockSpec((1, ROW_TILE), lambda b, t: (0, b * nt + t)),
                  full(g1), w_spec, full(gcol), full(bf_col), full(invf), full(tri)],
        out_specs=out_specs,
        scratch_shapes=[pltpu.VMEM((F_ROWS, LANES), F32),
                        pltpu.VMEM((6 * d + F_ROWS, D_MODEL), BF16),
                        pltpu.VMEM((D_MODEL, 2 * d), BF16)],
        compiler_params=pltpu.CompilerParams(
            dimension_semantics=("arbitrary", "arbitrary"), vmem_limit_bytes=VMEM_LIMIT_BYTES),
        name="in_proj",
    )(h2, pos_row, g1, w_in, gcol, bf_col, invf, tri)


def _fox_kernel(plan_ref, q_ref, k_ref, v_ref, g_ref, o_ref):
    i = pl.program_id(2)
    plan = plan_ref[(pl.program_id(0) * N_HEADS + pl.program_id(1)) * pl.num_programs(2) + i]
    first = plan // len(FOX_EDGE_OFFSETS)
    edge = plan % len(FOX_EDGE_OFFSETS)
    uk, uq = FOX_UK, FOX_UQ
    n_q = FOX_T // uq
    q = [q_ref[0, 0, :, a * uq:(a + 1) * uq] for a in range(n_q)]
    rel = (lax.broadcasted_iota(jnp.int32, (uk, uq), 0)
           - lax.broadcasted_iota(jnp.int32, (uk, uq), 1))

    def scores(j, ks, a):
        start = pl.multiple_of(j * FOX_T, FOX_T)
        return _dot(k_ref[0, 0, pl.ds(start + ks * uk, uk), :], q[a])

    def consume(s, v_sub, state, max_rel):
        m, acc = state
        if max_rel is not None:
            s = jnp.where(rel <= max_rel, s, NEG)
        m_new = jnp.maximum(m, jnp.max(s, axis=0, keepdims=True))
        alpha = jnp.exp2(m - m_new)
        p = jnp.exp2((s - m_new).astype(BF16))
        acc = alpha * acc + _dot(v_sub, p)
        return m_new, acc

    def step(j, carry, diagonal=False, min_offset=None):
        s_first, states = carry
        states = list(states)
        units = []
        for ks in range(FOX_T // uk):
            for a in range(n_q):
                max_rel = None
                if min_offset is not None and ks - a < min_offset:
                    continue
                if diagonal:
                    if ks * uk >= (a + 1) * uq:
                        continue
                    if (ks + 1) * uk - 1 > a * uq:
                        max_rel = a * uq - ks * uk
                units.append((ks, a, max_rel))
        pending = {0: scores(j, units[0][0], units[0][1]) if s_first is None else s_first}
        queue = list(range(1, len(units))) + ([] if diagonal else ["next"])
        result = {}

        def issue(item):
            if item == "next":
                result["next"] = scores(j + 1, 0, 0)
            else:
                pending[item] = scores(j, units[item][0], units[item][1])

        for item in queue[:FOX_AHEAD - 1]:
            issue(item)
        for n, (ks, a, max_rel) in enumerate(units):
            if n + FOX_AHEAD - 1 < len(queue):
                issue(queue[n + FOX_AHEAD - 1])
            chunk, off = divmod(ks * uk, ROW_TILE)
            v_sub = v_ref[0, 0, j * (FOX_T // ROW_TILE) + chunk, :, off:off + uk]
            states[a] = consume(pending.pop(n), v_sub, states[a], max_rel)
        return result.get("next"), tuple(states)

    init = tuple((jnp.full((1, uq), NEG, F32), jnp.zeros((FOX_VROWS, uq), F32))
                 for _ in range(n_q))
    branches = [lambda: (scores(i, 0, 0), init)]
    for offset in FOX_EDGE_OFFSETS:
        branches.append(functools.partial(step, first, (None, init), min_offset=offset))
    carry = lax.switch(jnp.where(first >= i, 0, edge + 1), branches)
    carry = lax.fori_loop(first + 1, i, step, carry)
    _, states = step(i, carry, diagonal=True)
    out = jnp.concatenate([acc[:HEAD_DIM] * (1.0 / acc[HEAD_DIM:HEAD_DIM + 1])
                           for _, acc in states], axis=1)
    o_ref[0, 0] = (out * g_ref[0, 0].astype(F32)).astype(BF16)


def _fox_plan(stats, batch, seq):
    nt, nq = seq // ROW_TILE, seq // FOX_T
    sub_t, sub_q = ROW_TILE // FOX_UQ, FOX_T // FOX_UQ
    st = stats.reshape(batch, nt, N_HEADS, LANES)
    q_sq = st[..., 0].reshape(batch, nq, nt // nq, N_HEADS).max(axis=2)
    k_sq = st[..., 1].max(axis=1, keepdims=True)
    bound = jnp.sqrt(q_sq * k_sq) * 1.01 + 1.0
    edges = st[..., 2:2 + 2 * sub_t].reshape(batch, nt, N_HEADS, sub_t, 2)
    edges = edges.transpose(0, 1, 3, 2, 4).reshape(batch, nq, sub_q, N_HEADS, 2)
    c_first, c_last = edges[..., 0], edges[..., 1]
    gap = c_first[:, :, :, None, None, :] - c_last[:, None, None, :, :, :]
    skip = (2.0 * bound[:, :, None, None, None, :] + gap) < -FOX_SKIP_BITS
    steps = jnp.arange(nq, dtype=jnp.int32)
    step_skip = jnp.all(skip, axis=(2, 4))
    first = jnp.min(jnp.where(step_skip, nq, steps[None, None, :, None]), axis=2)
    first = jnp.minimum(first, steps[None, :, None])
    other_step = steps[None, None, None, :, None, None] != first[:, :, None, None, None, :]
    at_first = jnp.all(skip | other_step, axis=3)
    offset = (jnp.arange(sub_q)[None, :] - jnp.arange(sub_q)[:, None])
    variant = jnp.zeros_like(first)
    for v, min_offset in enumerate(FOX_EDGE_OFFSETS):
        if min_offset is not None:
            dropped = (offset < min_offset)[None, None, :, :, None]
            ok = jnp.all(at_first | ~dropped, axis=(2, 3))
            variant = jnp.where(ok, v, variant)
    plan = first * len(FOX_EDGE_OFFSETS) + variant
    return plan.transpose(0, 2, 1).reshape(-1).astype(jnp.int32)


def _fox(first, qt, k, vt, gt):
    b, _, _, s = qt.shape
    q_spec = pl.BlockSpec((1, 1, 2 * HEAD_DIM, FOX_T), lambda bi, h, i, f: (bi, h, 0, i))
    k_spec = pl.BlockSpec((1, 1, s, 2 * HEAD_DIM), lambda bi, h, i, f: (bi, h, 0, 0))
    v_spec = pl.BlockSpec((1, 1, s // ROW_TILE, FOX_VROWS, ROW_TILE),
                          lambda bi, h, i, f: (bi, h, 0, 0, 0))
    o_spec = pl.BlockSpec((1, 1, HEAD_DIM, FOX_T), lambda bi, h, i, f: (bi, h, 0, i))
    return pl.pallas_call(
        _fox_kernel,
        out_shape=jax.ShapeDtypeStruct((b, N_HEADS, HEAD_DIM, s), BF16),
        grid_spec=pltpu.PrefetchScalarGridSpec(
            num_scalar_prefetch=1,
            grid=(b, N_HEADS, s // FOX_T),
            in_specs=[q_spec, k_spec, v_spec, o_spec],
            out_specs=o_spec),
        compiler_params=pltpu.CompilerParams(
            dimension_semantics=("arbitrary",) * 3, vmem_limit_bytes=VMEM_LIMIT_BYTES),
        name="fox_attention",
    )(first, qt, k, vt, gt)


def _dil_kernel(q_ref, k_ref, v_ref, g_ref, o_ref, q4_s, k4_s, v4_s, acc_s, m_s, l_s, *, seq):
    tq, tw, sup = DIL_TQ, DIL_TW, DIL_SUPER
    part = seq // DIL_SPLIT

    def regroup(c, carry):
        for src, dst in ((q_ref, q4_s), (k_ref, k4_s), (v_ref, v4_s)):
            for r in range(DIL_SPLIT):
                rows = src[0, pl.ds(c * DIL_SPLIT * DIL_COPY + r, DIL_COPY, stride=DIL_SPLIT), :]
                dst[pl.ds(r * part + c * DIL_COPY, DIL_COPY), :] = rows
        return carry

    lax.fori_loop(0, part // DIL_COPY, regroup, 0)

    def stream_rows(nat_ref, split_ref, d, r, pos, n):
        if d % DIL_SPLIT:
            x = nat_ref[0, pl.ds(d * pos + r, n, stride=d), :]
        else:
            sd = d // DIL_SPLIT
            start = (r % DIL_SPLIT) * part + sd * pos + r // DIL_SPLIT
            x = split_ref[pl.ds(start, n, stride=sd), :]
        return x.astype(BF16)

    lane = lax.broadcasted_iota(jnp.int32, (1, LANES), 1)
    low = lane < HEAD_DIM
    rel = (lax.broadcasted_iota(jnp.int32, (tq, tw), 0)
           - lax.broadcasted_iota(jnp.int32, (tq, tw), 1))

    def band_bias(q_minus_w):
        dist = rel + q_minus_w
        return jnp.where(dist >= 0, jnp.where(dist <= N_BACK, 0.0, NEG), NEG)

    bias_inner = band_bias(N_BACK)
    bias_start = band_bias(0)

    def issue(d, base, idx, nblk):
        r = idx // nblk
        jb = idx % nblk
        qs = base // d + jb * tq
        ws = jnp.maximum(qs - N_BACK, 0)
        q = stream_rows(q_ref, q4_s, d, r, qs, tq)
        kw = stream_rows(k_ref, k4_s, d, r, ws, tw)
        zero = jnp.zeros_like(q)
        q2 = jnp.concatenate([jnp.where(low, q, zero), jnp.where(low, zero, q)], axis=0)
        return _dot_nt(q2, kw), r, qs, ws, d * (qs - base // d) + r

    def consume(g, d, s, r, qs, ws, out_row):
        vw = stream_rows(v_ref, v4_s, d, r, ws, tw)
        bias = jnp.where(qs == ws, bias_start, bias_inner)
        ps, ms = [], []
        for hh in range(2):
            sm = s[hh * tq:(hh + 1) * tq] + bias
            m = jnp.max(sm, axis=-1, keepdims=True)
            ps.append(jnp.exp2((sm - m).astype(BF16)))
            ms.append(jnp.broadcast_to(m, (tq, LANES)))
        vw1 = jnp.concatenate([vw, jnp.ones_like(vw)], axis=1)
        acc = _dot(jnp.concatenate(ps, axis=0), vw1)
        rows = pl.ds(out_row, tq, stride=d)
        acc_s[g, rows, :] = jnp.where(low, acc[:tq, :LANES], acc[tq:, :LANES])
        m_s[g, rows, :] = jnp.where(low, ms[0], ms[1])
        l_s[g, rows, :] = jnp.where(low, acc[:tq, LANES:], acc[tq:, LANES:])

    def superblock(sb, carry):
        base = pl.multiple_of(sb * sup, sup)
        for g, (_, d) in enumerate(DILATED_PATTERNS):
            nblk = sup // (d * tq)

            def body(it, c, g=g, d=d, nblk=nblk):
                first = it * DIL_GROUP
                pending = {n: issue(d, base, first + n, nblk) for n in range(DIL_AHEAD)}
                for n in range(DIL_GROUP):
                    if n + DIL_AHEAD < DIL_GROUP:
                        pending[n + DIL_AHEAD] = issue(d, base, first + n + DIL_AHEAD, nblk)
                    consume(g, d, *pending.pop(n))
                return c

            lax.fori_loop(0, d * nblk // DIL_GROUP, body, 0)

        rows = pl.ds(base, sup)
        m1, m2, m3 = m_s[0], m_s[1], m_s[2]
        mx = jnp.maximum(jnp.maximum(m1, m2), m3)
        e1, e2, e3 = jnp.exp2(m1 - mx), jnp.exp2(m2 - mx), jnp.exp2(m3 - mx)
        num = e1 * acc_s[0] + e2 * acc_s[1] + e3 * acc_s[2]
        den = e1 * l_s[0] + e2 * l_s[1] + e3 * l_s[2]
        o_ref[0, rows, :] = (num / den * g_ref[0, rows, :].astype(F32)).astype(BF16)
        return carry

    lax.fori_loop(0, seq // sup, superblock, 0)


def _dilated(qb, kb, vb, gb):
    b, s, _ = qb.shape
    spec = pl.BlockSpec((1, s, LANES), lambda bi, p: (bi, 0, p))
    regrouped = pltpu.VMEM((s, LANES), F32)
    scratch = pltpu.VMEM((len(DILATED_PATTERNS), DIL_SUPER, LANES), F32)
    return pl.pallas_call(
        functools.partial(_dil_kernel, seq=s),
        out_shape=jax.ShapeDtypeStruct((b, s, D_BRANCH), BF16),
        grid=(b, D_BRANCH // LANES),
        in_specs=[spec, spec, spec, spec],
        out_specs=spec,
        scratch_shapes=[regrouped, regrouped, regrouped, scratch, scratch, scratch],
        compiler_params=pltpu.CompilerParams(
            dimension_semantics=("arbitrary",) * 2, vmem_limit_bytes=VMEM_LIMIT_BYTES),
        name="dilated_attn",
    )(qb, kb, vb, gb)


def _post_kernel(h_ref, oat_ref, ob_ref, p_ref, wo_ref, g2_ref, wg_ref, wple_ref,
                 out_ref, wo_s, wg_s, wple_s):
    @pl.when((pl.program_id(0) == 0) & (pl.program_id(1) == 0))
    def _():
        wo_s[...] = wo_ref[0].astype(BF16)
        wg_s[...] = wg_ref[0].astype(BF16)
        wple_s[...] = wple_ref[0].astype(BF16)

    d = D_BRANCH
    h1 = h_ref[...] + _dot_tn(oat_ref[0], wo_s[:d, :]) + _dot(ob_ref[...], wo_s[d:, :])
    u2 = _rms_norm_rows(h1, g2_ref[...]).astype(BF16)
    gate = jax.nn.sigmoid(_dot(u2, wg_s[...]))
    ple = _dot(p_ref[0].astype(BF16), wple_s[...])
    out_ref[...] = h1 + ple * gate


def _post(h2, oat, ob, p3, layer, batch, seq, w_out, g2, w_gate, w_ple):
    nt = seq // POST_TILE
    row = lambda w: pl.BlockSpec((POST_TILE, w), lambda b, t: (b * nt + t, 0))
    full = lambda a: pl.BlockSpec(a.shape, lambda b, t: (0,) * a.ndim)
    once = lambda a: pl.BlockSpec((1,) + a.shape[1:], lambda b, t: (layer, 0, 0),
                                  pipeline_mode=pl.Buffered(1))
    return pl.pallas_call(
        _post_kernel,
        out_shape=jax.ShapeDtypeStruct((batch * seq, D_MODEL), F32),
        grid=(batch, nt),
        in_specs=[row(D_MODEL), pl.BlockSpec((1, D_BRANCH, POST_TILE), lambda b, t: (b, 0, t)),
                  row(D_BRANCH),
                  pl.BlockSpec((1, POST_TILE, PLE_DIM), lambda b, t: (layer, b * nt + t, 0)),
                  once(w_out), full(g2), once(w_gate), once(w_ple)],
        out_specs=row(D_MODEL),
        scratch_shapes=[pltpu.VMEM(w_out.shape[1:], BF16), pltpu.VMEM(w_gate.shape[1:], BF16),
                        pltpu.VMEM(w_ple.shape[1:], BF16)],
        compiler_params=pltpu.CompilerParams(
            dimension_semantics=("arbitrary", "arbitrary"), vmem_limit_bytes=VMEM_LIMIT_BYTES),
        name="out_stage",
    )(h2, oat, ob, p3, w_out, g2, w_gate, w_ple)


def _layer(h2, p3, layer, pos_row, invf, tri, batch, seq, norm_g, w_in, b_f, qk_g, w_out, w_ple,
           ple_norm_g, w_ple_gate):
    d = D_BRANCH
    bf_col = jnp.pad(b_f.astype(F32), (0, F_ROWS - N_HEADS)).reshape(F_ROWS, 1)
    scale = HEAD_DIM ** -0.5
    gcol = jnp.concatenate([jnp.tile(qk_g[0] * (scale * LOG2E), N_HEADS), jnp.tile(qk_g[1], N_HEADS),
                            jnp.tile(qk_g[2] * (scale * LOG2E), N_HEADS), jnp.tile(qk_g[3], N_HEADS)]
                           ).astype(F32).reshape(4 * d, 1)

    qt, k, vt, gt, qb, kb, vb, gb, stats = _proj(
        h2, pos_row, layer, batch, seq, norm_g.reshape(1, D_MODEL), w_in.astype(F32), gcol,
        bf_col, invf, tri)

    oat = _fox(_fox_plan(stats, batch, seq), qt, k, vt, gt).reshape(batch, d, seq)
    as3 = lambda a: a.reshape(batch, seq, d)
    ob = _dilated(as3(qb), as3(kb), as3(vb), as3(gb)).reshape(batch * seq, d)

    return _post(h2, oat, ob, p3, layer, batch, seq, w_out.astype(F32),
                 ple_norm_g[layer].reshape(1, D_MODEL), w_ple_gate.astype(F32),
                 w_ple.astype(F32))


def kernel(x, p, positions, norm_g, w_in, b_f, qk_norm_g, w_out, w_ple, ple_norm_g, w_ple_gate):
    batch, seq, _ = x.shape
    depth = p.shape[0]
    assert x.shape[2] == D_MODEL and p.shape[1:] == (batch, seq, PLE_DIM)
    assert w_in.shape[1:] == (D_MODEL, N_IN)
    assert all(w // dil == N_BACK for w, dil in DILATED_PATTERNS)
    assert all(seq % tile == 0 for tile in (ROW_TILE, POST_TILE, FOX_T, DIL_SUPER,
                                            DIL_SPLIT * DIL_COPY))
    assert FOX_UK == FOX_UQ and FOX_T % FOX_UQ == 0 and ROW_TILE % FOX_UQ == 0
    assert FOX_T % ROW_TILE == 0 and DIL_SUPER % (DIL_GROUP * DIL_TQ) == 0
    pos_row = positions.reshape(1, batch * seq)
    inv_freq = ROPE_THETA ** (-jnp.arange(ROPE_HALF, dtype=F32) / ROPE_HALF)
    invf = inv_freq.reshape(ROPE_HALF, 1)
    idx = np.arange(ROW_TILE)
    tri = jnp.asarray(idx[:, None] <= idx[None, :], BF16)
    h2 = x.reshape(batch * seq, D_MODEL)
    p3 = p.reshape(depth, batch * seq, PLE_DIM)
    for i in range(depth):
        h2 = _layer(h2, p3, i, pos_row, invf, tri, batch, seq,
                    norm_g[i], w_in, b_f[i], qk_norm_g[i], w_out, w_ple,
                    ple_norm_g, w_ple_gate)
    return h2.reshape(batch, seq, D_MODEL)
```

```python
import functools

import numpy as np
import jax
import jax.numpy as jnp
from jax import lax
from jax.experimental import pallas as pl
from jax.experimental.pallas import tpu as pltpu

D_MODEL = 1024
HEAD_DIM = 64
N_HEADS = 8
D_BRANCH = N_HEADS * HEAD_DIM
PLE_DIM = 256
ROPE_THETA = 500000.0
ROPE_DIM = HEAD_DIM // 4
ROPE_HALF = ROPE_DIM // 2
DILATED_PATTERNS = ((128, 1), (512, 4), (2048, 16))
N_BACK = 128
EPS = 1e-6
NEG = -1e30

LANES = 128
SUBLANES = 8
BF16_ROWS = 16
VMEM_LIMIT_BYTES = 56 * 1024 * 1024

ROW_TILE = 512
POST_TILE = 1024
FOX_T = 1024
FOX_UK = 256
FOX_UQ = 256
FOX_AHEAD = 8
FOX_VROWS = HEAD_DIM + BF16_ROWS
FOX_SKIP_BITS = 136.0
FOX_EDGE_OFFSETS = (None, 0, 1, 2, 3)
LOG2E = 1.4426950408889634
DIL_TQ = N_BACK
DIL_TW = DIL_TQ + N_BACK
DIL_SUPER = max(d for _, d in DILATED_PATTERNS) * DIL_TQ
DIL_SPLIT = 4
DIL_COPY = 512
DIL_GROUP = 8
DIL_AHEAD = 4
F_ROWS = BF16_ROWS
W_IN_COLUMNS = dict(zip(
    ("qa", "ka", "va", "ga", "f", "qb", "kb", "vb", "gb"),
    np.cumsum([0] + [D_BRANCH] * 4 + [N_HEADS] + [D_BRANCH] * 3).tolist()))
N_IN = 8 * D_BRANCH + N_HEADS

F32 = jnp.float32
BF16 = jnp.bfloat16


def _dot(a, b):
    return jnp.dot(a, b, preferred_element_type=F32)


def _dot_nt(a, b):
    return lax.dot_general(a, b, (((1,), (1,)), ((), ())), preferred_element_type=F32)


def _dot_tn(a, b):
    return lax.dot_general(a, b, (((0,), (0,)), ((), ())), preferred_element_type=F32)


def _split3(x):
    hi = x.astype(BF16).astype(F32)
    r = x - hi
    mid = r.astype(BF16).astype(F32)
    lo = (r - mid).astype(BF16).astype(F32)
    return hi, mid, lo


def _rms_norm_rows(h, g_row):
    ms = jnp.mean(h * h, axis=-1, keepdims=True)
    return h * lax.rsqrt(ms + EPS) * g_row


def _proj_kernel(h_ref, pos_ref, g1_ref, win_ref, gcol_ref, bf_ref, invf_ref, tri_ref,
                 qt_ref, k_ref, vt_ref, gt_ref, qb_ref, kb_ref, vb_ref, gb_ref, stats_ref,
                 carry_ref, wt_ref, wn_ref):
    t = pl.program_id(1)
    tm = ROW_TILE
    d = D_BRANCH

    @pl.when((pl.program_id(0) == 0) & (t == 0))
    def _():
        def cols(name, width=d):
            lo = W_IN_COLUMNS[name]
            return win_ref[0, :, lo:lo + width]

        for g, name in enumerate(("qa", "ka", "va", "ga", "qb", "kb")):
            wt_ref[g * d:(g + 1) * d, :] = cols(name).T.astype(BF16)
        f_rows = cols("f", LANES).T[:F_ROWS]
        keep = lax.broadcasted_iota(jnp.int32, f_rows.shape, 0) < N_HEADS
        wt_ref[6 * d:6 * d + F_ROWS, :] = jnp.where(keep, f_rows, 0.0).astype(BF16)
        wn_ref[:, :d] = cols("vb").astype(BF16)
        wn_ref[:, d:] = cols("gb").astype(BF16)

    u = _rms_norm_rows(h_ref[...], g1_ref[...]).astype(BF16)

    def feat_major(group):
        return _dot_nt(wt_ref[group * d:(group + 1) * d, :], u)

    zn = _dot(u, wn_ref[...])
    f = _dot_nt(wt_ref[6 * d:6 * d + F_ROWS, :], u) + bf_ref[...]
    z_qa = feat_major(0)

    vb_ref[...] = zn[:, :d]
    gate_b = zn[:, d:]
    gb_ref[...] = (gate_b * jax.nn.sigmoid(gate_b)).astype(BF16)
    z_ka = feat_major(1)

    def head_norm(z, group):
        z3 = z.reshape(N_HEADS, HEAD_DIM, tm)
        ss = jnp.sum(z3 * z3, axis=1, keepdims=True)
        y3 = z3 * lax.rsqrt(ss * (1.0 / HEAD_DIM) + EPS)
        return y3.reshape(d, tm) * gcol_ref[group * d:(group + 1) * d, :]

    ang = invf_ref[...] * pos_ref[...].astype(F32)
    cos = jnp.cos(ang)
    sin = jnp.sin(ang)

    def rope(y):
        y3 = y.reshape(N_HEADS, HEAD_DIM, tm)
        x1 = y3[:, :ROPE_HALF, :]
        x2 = y3[:, ROPE_HALF:ROPE_DIM, :]
        rot = [x1 * cos - x2 * sin, x2 * cos + x1 * sin, y3[:, ROPE_DIM:, :]]
        return jnp.concatenate(rot, axis=1).reshape(d, tm)

    lf = jnp.minimum(f, 0.0) - jnp.log1p(jnp.exp(-jnp.abs(f)))
    stacked = jnp.concatenate([x.astype(BF16) for x in _split3(lf)], axis=0)
    cs = _dot(stacked, tri_ref[...])
    carry = jnp.where(t == 0, 0.0, carry_ref[:, 0:1])
    c = cs[:F_ROWS] + cs[F_ROWS:2 * F_ROWS] + cs[2 * F_ROWS:] + carry
    carry_ref[...] = jnp.broadcast_to(c[:, tm - 1:tm], (F_ROWS, LANES))
    c2 = c * LOG2E

    qa = head_norm(z_qa, 0)
    z_va = feat_major(2)
    ka = head_norm(z_ka, 1)
    z_ga = feat_major(3)
    row = lax.broadcasted_iota(jnp.int32, (SUBLANES, tm), 0)
    pad = jnp.zeros((HEAD_DIM - SUBLANES, tm), F32)
    for hd in range(N_HEADS):
        c_hi, c_mid, c_lo = _split3(c2[hd:hd + 1, :])
        q_aug = jnp.where(row == 0, c_hi, jnp.where(row == 1, c_mid, jnp.where(
            row == 2, c_lo, jnp.where(row < 6, 1.0, 0.0))))
        k_aug = jnp.where(row < 3, 1.0, jnp.where(row == 3, -c_hi, jnp.where(
            row == 4, -c_mid, jnp.where(row == 5, -c_lo, 0.0))))
        rows = slice(hd * HEAD_DIM, (hd + 1) * HEAD_DIM)
        qt_ref[0, hd] = jnp.concatenate([qa[rows], q_aug, pad], axis=0).astype(BF16)
        k_ref[0, hd] = jnp.concatenate([ka[rows], k_aug, pad], axis=0).T.astype(BF16)

    ones_row = jnp.where(lax.broadcasted_iota(jnp.int32, (N_HEADS, BF16_ROWS, tm), 1) == 0,
                         1.0, 0.0)
    z_qb = feat_major(4)
    va = z_va.reshape(N_HEADS, HEAD_DIM, tm)
    vt_ref[0, :, 0] = jnp.concatenate([va, ones_row], axis=1).astype(BF16)
    z_kb = feat_major(5)
    gt_ref[0] = (z_ga * jax.nn.sigmoid(z_ga)).reshape(N_HEADS, HEAD_DIM, tm).astype(BF16)

    qb_ref[...] = rope(head_norm(z_qb, 2)).T
    kb_ref[...] = rope(head_norm(z_kb, 3)).T

    def max_sq_norm(y):
        y3 = y.reshape(N_HEADS, HEAD_DIM, tm)
        return jnp.max(jnp.sum(y3 * y3, axis=1), axis=-1, keepdims=True)

    lane = lax.broadcasted_iota(jnp.int32, (N_HEADS, LANES), 1)
    stats = jnp.where(lane == 0, max_sq_norm(qa), jnp.where(lane == 1, max_sq_norm(ka), 0.0))
    for sub in range(tm // FOX_UQ):
        lo = sub * FOX_UQ
        hi = lo + FOX_UQ - 1
        stats = jnp.where(lane == 2 + 2 * sub, c2[:N_HEADS, lo:lo + 1],
                          jnp.where(lane == 3 + 2 * sub, c2[:N_HEADS, hi:hi + 1], stats))
    stats_ref[0] = stats


def _proj(h2, pos_row, layer, batch, seq, g1, w_in, gcol, bf_col, invf, tri):
    nt = seq // ROW_TILE
    rows = batch * seq
    d = D_BRANCH
    row = lambda w: pl.BlockSpec((ROW_TILE, w), lambda b, t: (b * nt + t, 0))
    full = lambda a: pl.BlockSpec(a.shape, lambda b, t: (0,) * a.ndim)
    w_spec = pl.BlockSpec((1,) + w_in.shape[1:], lambda b, t: (layer, 0, 0),
                          pipeline_mode=pl.Buffered(1))
    out_shape = (
        jax.ShapeDtypeStruct((batch, N_HEADS, 2 * HEAD_DIM, seq), BF16),
        jax.ShapeDtypeStruct((batch, N_HEADS, seq, 2 * HEAD_DIM), BF16),
        jax.ShapeDtypeStruct((batch, N_HEADS, nt, FOX_VROWS, ROW_TILE), BF16),
        jax.ShapeDtypeStruct((batch, N_HEADS, HEAD_DIM, seq), BF16),
        jax.ShapeDtypeStruct((rows, d), F32),
        jax.ShapeDtypeStruct((rows, d), F32),
        jax.ShapeDtypeStruct((rows, d), F32),
        jax.ShapeDtypeStruct((rows, d), BF16),
        jax.ShapeDtypeStruct((batch * nt, N_HEADS, LANES), F32),
    )
    out_specs = (
        pl.BlockSpec((1, N_HEADS, 2 * HEAD_DIM, ROW_TILE), lambda b, t: (b, 0, 0, t)),
        pl.BlockSpec((1, N_HEADS, ROW_TILE, 2 * HEAD_DIM), lambda b, t: (b, 0, t, 0)),
        pl.BlockSpec((1, N_HEADS, 1, FOX_VROWS, ROW_TILE), lambda b, t: (b, 0, t, 0, 0)),
        pl.BlockSpec((1, N_HEADS, HEAD_DIM, ROW_TILE), lambda b, t: (b, 0, 0, t)),
        row(d), row(d), row(d), row(d),
        pl.BlockSpec((1, N_HEADS, LANES), lambda b, t: (b * nt + t, 0, 0)),
    )
    return pl.pallas_call(
        _proj_kernel,
        out_shape=out_shape,
        grid=(batch, nt),
        in_specs=[row(D_MODEL), pl.BlockSpec((1, ROW_TILE), lambda b, t: (0, b * nt + t)),
                  full(g1), w_spec, full(gcol), full(bf_col), full(invf), full(tri)],
        out_specs=out_specs,
        scratch_shapes=[pltpu.VMEM((F_ROWS, LANES), F32),
                        pltpu.VMEM((6 * d + F_ROWS, D_MODEL), BF16),
                        pltpu.VMEM((D_MODEL, 2 * d), BF16)],
        compiler_params=pltpu.CompilerParams(
            dimension_semantics=("arbitrary", "arbitrary"), vmem_limit_bytes=VMEM_LIMIT_BYTES),
        name="in_proj",
    )(h2, pos_row, g1, w_in, gcol, bf_col, invf, tri)


def _fox_kernel(plan_ref, q_ref, k_ref, v_ref, g_ref, o_ref):
    i = pl.program_id(2)
    plan = plan_ref[(pl.program_id(0) * N_HEADS + pl.program_id(1)) * pl.num_programs(2) + i]
    first = plan // len(FOX_EDGE_OFFSETS)
    edge = plan % len(FOX_EDGE_OFFSETS)
    uk, uq = FOX_UK, FOX_UQ
    n_q = FOX_T // uq
    q = [q_ref[0, 0, :, a * uq:(a + 1) * uq] for a in range(n_q)]
    rel = (lax.broadcasted_iota(jnp.int32, (uk, uq), 0)
           - lax.broadcasted_iota(jnp.int32, (uk, uq), 1))

    def scores(j, ks, a):
        start = pl.multiple_of(j * FOX_T, FOX_T)
        return _dot(k_ref[0, 0, pl.ds(start + ks * uk, uk), :], q[a])

    def consume(s, v_sub, state, max_rel):
        m, acc = state
        if max_rel is not None:
            s = jnp.where(rel <= max_rel, s, NEG)
        m_new = jnp.maximum(m, jnp.max(s, axis=0, keepdims=True))
        alpha = jnp.exp2(m - m_new)
        p = jnp.exp2((s - m_new).astype(BF16))
        acc = alpha * acc + _dot(v_sub, p)
        return m_new, acc

    def step(j, carry, diagonal=False, min_offset=None):
        s_first, states = carry
        states = list(states)
        units = []
        for ks in range(FOX_T // uk):
            for a in range(n_q):
                max_rel = None
                if min_offset is not None and ks - a < min_offset:
                    continue
                if diagonal:
                    if ks * uk >= (a + 1) * uq:
                        continue
                    if (ks + 1) * uk - 1 > a * uq:
                        max_rel = a * uq - ks * uk
                units.append((ks, a, max_rel))
        pending = {0: scores(j, units[0][0], units[0][1]) if s_first is None else s_first}
        queue = list(range(1, len(units))) + ([] if diagonal else ["next"])
        result = {}

        def issue(item):
            if item == "next":
                result["next"] = scores(j + 1, 0, 0)
            else:
                pending[item] = scores(j, units[item][0], units[item][1])

        for item in queue[:FOX_AHEAD - 1]:
            issue(item)
        for n, (ks, a, max_rel) in enumerate(units):
            if n + FOX_AHEAD - 1 < len(queue):
                issue(queue[n + FOX_AHEAD - 1])
            chunk, off = divmod(ks * uk, ROW_TILE)
            v_sub = v_ref[0, 0, j * (FOX_T // ROW_TILE) + chunk, :, off:off + uk]
            states[a] = consume(pending.pop(n), v_sub, states[a], max_rel)
        return result.get("next"), tuple(states)

    init = tuple((jnp.full((1, uq), NEG, F32), jnp.zeros((FOX_VROWS, uq), F32))
                 for _ in range(n_q))
    branches = [lambda: (scores(i, 0, 0), init)]
    for offset in FOX_EDGE_OFFSETS:
        branches.append(functools.partial(step, first, (None, init), min_offset=offset))
    carry = lax.switch(jnp.where(first >= i, 0, edge + 1), branches)
    carry = lax.fori_loop(first + 1, i, step, carry)
    _, states = step(i, carry, diagonal=True)
    out = jnp.concatenate([acc[:HEAD_DIM] / acc[HEAD_DIM:HEAD_DIM + 1] for _, acc in states],
                          axis=1)
    o_ref[0, 0] = (out * g_ref[0, 0].astype(F32)).astype(BF16)


def _fox_plan(stats, batch, seq):
    nt, nq = seq // ROW_TILE, seq // FOX_T
    sub_t, sub_q = ROW_TILE // FOX_UQ, FOX_T // FOX_UQ
    st = stats.reshape(batch, nt, N_HEADS, LANES)
    q_sq = st[..., 0].reshape(batch, nq, nt // nq, N_HEADS).max(axis=2)
    k_sq = st[..., 1].max(axis=1, keepdims=True)
    bound = jnp.sqrt(q_sq * k_sq) * 1.01 + 1.0
    edges = st[..., 2:2 + 2 * sub_t].reshape(batch, nt, N_HEADS, sub_t, 2)
    edges = edges.transpose(0, 1, 3, 2, 4).reshape(batch, nq, sub_q, N_HEADS, 2)
    c_first, c_last = edges[..., 0], edges[..., 1]
    gap = c_first[:, :, :, None, None, :] - c_last[:, None, None, :, :, :]
    skip = (2.0 * bound[:, :, None, None, None, :] + gap) < -FOX_SKIP_BITS
    steps = jnp.arange(nq, dtype=jnp.int32)
    step_skip = jnp.all(skip, axis=(2, 4))
    first = jnp.min(jnp.where(step_skip, nq, steps[None, None, :, None]), axis=2)
    first = jnp.minimum(first, steps[None, :, None])
    other_step = steps[None, None, None, :, None, None] != first[:, :, None, None, None, :]
    at_first = jnp.all(skip | other_step, axis=3)
    offset = (jnp.arange(sub_q)[None, :] - jnp.arange(sub_q)[:, None])
    variant = jnp.zeros_like(first)
    for v, min_offset in enumerate(FOX_EDGE_OFFSETS):
        if min_offset is not None:
            dropped = (offset < min_offset)[None, None, :, :, None]
            ok = jnp.all(at_first | ~dropped, axis=(2, 3))
            variant = jnp.where(ok, v, variant)
    plan = first * len(FOX_EDGE_OFFSETS) + variant
    return plan.transpose(0, 2, 1).reshape(-1).astype(jnp.int32)


def _fox(first, qt, k, vt, gt):
    b, _, _, s = qt.shape
    q_spec = pl.BlockSpec((1, 1, 2 * HEAD_DIM, FOX_T), lambda bi, h, i, f: (bi, h, 0, i))
    k_spec = pl.BlockSpec((1, 1, s, 2 * HEAD_DIM), lambda bi, h, i, f: (bi, h, 0, 0))
    v_spec = pl.BlockSpec((1, 1, s // ROW_TILE, FOX_VROWS, ROW_TILE),
                          lambda bi, h, i, f: (bi, h, 0, 0, 0))
    o_spec = pl.BlockSpec((1, 1, HEAD_DIM, FOX_T), lambda bi, h, i, f: (bi, h, 0, i))
    return pl.pallas_call(
        _fox_kernel,
        out_shape=jax.ShapeDtypeStruct((b, N_HEADS, HEAD_DIM, s), BF16),
        grid_spec=pltpu.PrefetchScalarGridSpec(
            num_scalar_prefetch=1,
            grid=(b, N_HEADS, s // FOX_T),
            in_specs=[q_spec, k_spec, v_spec, o_spec],
            out_specs=o_spec),
        compiler_params=pltpu.CompilerParams(
            dimension_semantics=("arbitrary",) * 3, vmem_limit_bytes=VMEM_LIMIT_BYTES),
        name="fox_attention",
    )(first, qt, k, vt, gt)


def _dil_kernel(q_ref, k_ref, v_ref, g_ref, o_ref, q4_s, k4_s, v4_s, acc_s, m_s, l_s, *, seq):
    tq, tw, sup = DIL_TQ, DIL_TW, DIL_SUPER
    part = seq // DIL_SPLIT

    def regroup(c, carry):
        for src, dst in ((q_ref, q4_s), (k_ref, k4_s), (v_ref, v4_s)):
            for r in range(DIL_SPLIT):
                rows = src[0, pl.ds(c * DIL_SPLIT * DIL_COPY + r, DIL_COPY, stride=DIL_SPLIT), :]
                dst[pl.ds(r * part + c * DIL_COPY, DIL_COPY), :] = rows
        return carry

    lax.fori_loop(0, part // DIL_COPY, regroup, 0)

    def stream_rows(nat_ref, split_ref, d, r, pos, n):
        if d % DIL_SPLIT:
            x = nat_ref[0, pl.ds(d * pos + r, n, stride=d), :]
        else:
            sd = d // DIL_SPLIT
            start = (r % DIL_SPLIT) * part + sd * pos + r // DIL_SPLIT
            x = split_ref[pl.ds(start, n, stride=sd), :]
        return x.astype(BF16)

    lane = lax.broadcasted_iota(jnp.int32, (1, LANES), 1)
    low = lane < HEAD_DIM
    rel = (lax.broadcasted_iota(jnp.int32, (tq, tw), 0)
           - lax.broadcasted_iota(jnp.int32, (tq, tw), 1))

    def band_bias(q_minus_w):
        dist = rel + q_minus_w
        return jnp.where(dist >= 0, jnp.where(dist <= N_BACK, 0.0, NEG), NEG)

    bias_inner = band_bias(N_BACK)
    bias_start = band_bias(0)

    def issue(d, base, idx, nblk):
        r = idx // nblk
        jb = idx % nblk
        qs = base // d + jb * tq
        ws = jnp.maximum(qs - N_BACK, 0)
        q = stream_rows(q_ref, q4_s, d, r, qs, tq)
        kw = stream_rows(k_ref, k4_s, d, r, ws, tw)
        zero = jnp.zeros_like(q)
        q2 = jnp.concatenate([jnp.where(low, q, zero), jnp.where(low, zero, q)], axis=0)
        return _dot_nt(q2, kw), r, qs, ws, d * (qs - base // d) + r

    def consume(g, d, s, r, qs, ws, out_row):
        vw = stream_rows(v_ref, v4_s, d, r, ws, tw)
        bias = jnp.where(qs == ws, bias_start, bias_inner)
        ps, ms = [], []
        for hh in range(2):
            sm = s[hh * tq:(hh + 1) * tq] + bias
            m = jnp.max(sm, axis=-1, keepdims=True)
            ps.append(jnp.exp2((sm - m).astype(BF16)))
            ms.append(jnp.broadcast_to(m, (tq, LANES)))
        vw1 = jnp.concatenate([vw, jnp.ones_like(vw)], axis=1)
        acc = _dot(jnp.concatenate(ps, axis=0), vw1)
        rows = pl.ds(out_row, tq, stride=d)
        acc_s[g, rows, :] = jnp.where(low, acc[:tq, :LANES], acc[tq:, :LANES])
        m_s[g, rows, :] = jnp.where(low, ms[0], ms[1])
        l_s[g, rows, :] = jnp.where(low, acc[:tq, LANES:], acc[tq:, LANES:])

    def superblock(sb, carry):
        base = pl.multiple_of(sb * sup, sup)
        for g, (_, d) in enumerate(DILATED_PATTERNS):
            nblk = sup // (d * tq)

            def body(it, c, g=g, d=d, nblk=nblk):
                first = it * DIL_GROUP
                pending = {n: issue(d, base, first + n, nblk) for n in range(DIL_AHEAD)}
                for n in range(DIL_GROUP):
                    if n + DIL_AHEAD < DIL_GROUP:
                        pending[n + DIL_AHEAD] = issue(d, base, first + n + DIL_AHEAD, nblk)
                    consume(g, d, *pending.pop(n))
                return c

            lax.fori_loop(0, d * nblk // DIL_GROUP, body, 0)

        rows = pl.ds(base, sup)
        m1, m2, m3 = m_s[0], m_s[1], m_s[2]
        mx = jnp.maximum(jnp.maximum(m1, m2), m3)
        e1, e2, e3 = jnp.exp2(m1 - mx), jnp.exp2(m2 - mx), jnp.exp2(m3 - mx)
        num = e1 * acc_s[0] + e2 * acc_s[1] + e3 * acc_s[2]
        den = e1 * l_s[0] + e2 * l_s[1] + e3 * l_s[2]
        o_ref[0, rows, :] = (num / den * g_ref[0, rows, :].astype(F32)).astype(BF16)
        return carry

    lax.fori_loop(0, seq // sup, superblock, 0)


def _dilated(qb, kb, vb, gb):
    b, s, _ = qb.shape
    spec = pl.BlockSpec((1, s, LANES), lambda bi, p: (bi, 0, p))
    regrouped = pltpu.VMEM((s, LANES), F32)
    scratch = pltpu.VMEM((len(DILATED_PATTERNS), DIL_SUPER, LANES), F32)
    return pl.pallas_call(
        functools.partial(_dil_kernel, seq=s),
        out_shape=jax.ShapeDtypeStruct((b, s, D_BRANCH), BF16),
        grid=(b, D_BRANCH // LANES),
        in_specs=[spec, spec, spec, spec],
        out_specs=spec,
        scratch_shapes=[regrouped, regrouped, regrouped, scratch, scratch, scratch],
        compiler_params=pltpu.CompilerParams(
            dimension_semantics=("arbitrary",) * 2, vmem_limit_bytes=VMEM_LIMIT_BYTES),
        name="dilated_attn",
    )(qb, kb, vb, gb)


def _post_kernel(h_ref, oat_ref, ob_ref, p_ref, wo_ref, g2_ref, wg_ref, wple_ref,
                 out_ref, wo_s, wg_s, wple_s):
    @pl.when((pl.program_id(0) == 0) & (pl.program_id(1) == 0))
    def _():
        wo_s[...] = wo_ref[0].astype(BF16)
        wg_s[...] = wg_ref[0].astype(BF16)
        wple_s[...] = wple_ref[0].astype(BF16)

    d = D_BRANCH
    h1 = h_ref[...] + _dot_tn(oat_ref[0], wo_s[:d, :]) + _dot(ob_ref[...], wo_s[d:, :])
    u2 = _rms_norm_rows(h1, g2_ref[...]).astype(BF16)
    gate = jax.nn.sigmoid(_dot(u2, wg_s[...]))
    ple = _dot(p_ref[0].astype(BF16), wple_s[...])
    out_ref[...] = h1 + ple * gate


def _post(h2, oat, ob, p3, layer, batch, seq, w_out, g2, w_gate, w_ple):
    nt = seq // POST_TILE
    row = lambda w: pl.BlockSpec((POST_TILE, w), lambda b, t: (b * nt + t, 0))
    full = lambda a: pl.BlockSpec(a.shape, lambda b, t: (0,) * a.ndim)
    once = lambda a: pl.BlockSpec((1,) + a.shape[1:], lambda b, t: (layer, 0, 0),
                                  pipeline_mode=pl.Buffered(1))
    return pl.pallas_call(
        _post_kernel,
        out_shape=jax.ShapeDtypeStruct((batch * seq, D_MODEL), F32),
        grid=(batch, nt),
        in_specs=[row(D_MODEL), pl.BlockSpec((1, D_BRANCH, POST_TILE), lambda b, t: (b, 0, t)),
                  row(D_BRANCH),
                  pl.BlockSpec((1, POST_TILE, PLE_DIM), lambda b, t: (layer, b * nt + t, 0)),
                  once(w_out), full(g2), once(w_gate), once(w_ple)],
        out_specs=row(D_MODEL),
        scratch_shapes=[pltpu.VMEM(w_out.shape[1:], BF16), pltpu.VMEM(w_gate.shape[1:], BF16),
                        pltpu.VMEM(w_ple.shape[1:], BF16)],
        compiler_params=pltpu.CompilerParams(
            dimension_semantics=("arbitrary", "arbitrary"), vmem_limit_bytes=VMEM_LIMIT_BYTES),
        name="out_stage",
    )(h2, oat, ob, p3, w_out, g2, w_gate, w_ple)


def _layer(h2, p3, layer, pos_row, invf, tri, batch, seq, norm_g, w_in, b_f, qk_g, w_out, w_ple,
           ple_norm_g, w_ple_gate):
    d = D_BRANCH
    bf_col = jnp.pad(b_f.astype(F32), (0, F_ROWS - N_HEADS)).reshape(F_ROWS, 1)
    scale = HEAD_DIM ** -0.5
    gcol = jnp.concatenate([jnp.tile(qk_g[0] * (scale * LOG2E), N_HEADS), jnp.tile(qk_g[1], N_HEADS),
                            jnp.tile(qk_g[2] * (scale * LOG2E), N_HEADS), jnp.tile(qk_g[3], N_HEADS)]
                           ).astype(F32).reshape(4 * d, 1)

    qt, k, vt, gt, qb, kb, vb, gb, stats = _proj(
        h2, pos_row, layer, batch, seq, norm_g.reshape(1, D_MODEL), w_in.astype(F32), gcol,
        bf_col, invf, tri)

    oat = _fox(_fox_plan(stats, batch, seq), qt, k, vt, gt).reshape(batch, d, seq)
    as3 = lambda a: a.reshape(batch, seq, d)
    ob = _dilated(as3(qb), as3(kb), as3(vb), as3(gb)).reshape(batch * seq, d)

    return _post(h2, oat, ob, p3, layer, batch, seq, w_out.astype(F32),
                 ple_norm_g[layer].reshape(1, D_MODEL), w_ple_gate.astype(F32),
                 w_ple.astype(F32))


def kernel(x, p, positions, norm_g, w_in, b_f, qk_norm_g, w_out, w_ple, ple_norm_g, w_ple_gate):
    batch, seq, _ = x.shape
    depth = p.shape[0]
    assert x.shape[2] == D_MODEL and p.shape[1:] == (batch, seq, PLE_DIM)
    assert w_in.shape[1:] == (D_MODEL, N_IN)
    assert all(w // dil == N_BACK for w, dil in DILATED_PATTERNS)
    assert all(seq % tile == 0 for tile in (ROW_TILE, POST_TILE, FOX_T, DIL_SUPER,
                                            DIL_SPLIT * DIL_COPY))
    assert FOX_UK == FOX_UQ and FOX_T % FOX_UQ == 0 and ROW_TILE % FOX_UQ == 0
    assert FOX_T % ROW_TILE == 0 and DIL_SUPER % (DIL_GROUP * DIL_TQ) == 0
    pos_row = positions.reshape(1, batch * seq)
    inv_freq = ROPE_THETA ** (-jnp.arange(ROPE_HALF, dtype=F32) / ROPE_HALF)
    invf = inv_freq.reshape(ROPE_HALF, 1)
    idx = np.arange(ROW_TILE)
    tri = jnp.asarray(idx[:, None] <= idx[None, :], BF16)
    h2 = x.reshape(batch * seq, D_MODEL)
    p3 = p.reshape(depth, batch * seq, PLE_DIM)
    for i in range(depth):
        h2 = _layer(h2, p3, i, pos_row, invf, tri, batch, seq,
                    norm_g[i], w_in, b_f[i], qk_norm_g[i], w_out, w_ple,
                    ple_norm_g, w_ple_gate)
    return h2.reshape(batch, seq, D_MODEL)
```
